```python
import jax
import jax.numpy as jnp
from jax import lax
import numpy as np

D_MODEL = 2048
BATCH = 8
SEQ = 2048
DEPTH = 1
DEC_BATCH = 32
DEC_SEQ = 4
PAST_LEN = 16384
PAGE_SIZE = 128

N_HEADS = 16
HEAD_DIM = 128
KV_GROUPS = 4
HEADS_PER_GROUP = N_HEADS // KV_GROUPS
ROPE_DIM = HEAD_DIM // 4
ROPE_THETA = 500000.0
CMP_STRIDE = 16
CMP_BLOCK = 2 * CMP_STRIDE
SLC_BLOCK = 64
N_SELECT = 16
WINDOW = 512
NSA_Q_BLOCK = 16
ATT_SCALE = HEAD_DIM ** -0.5
FORCE_SCORE = 1e9
M_HEADS = 4
M_INNER = D_MODEL
M_HEAD_DIM = M_INNER // M_HEADS
M_CHUNK = 64
CONV_W = 4
NEG_BIG = -1e30
D_FF = 4 * D_MODEL
EPS = 1e-6

KV_WIDTH = 2 * KV_GROUPS * HEAD_DIM
SPLIT_SIZES = (N_HEADS * HEAD_DIM, KV_WIDTH, KV_WIDTH, KV_WIDTH, 3 * N_HEADS, 2 * M_INNER, M_INNER, M_INNER, 2 * M_HEADS, 2 * D_MODEL)
SPLIT_POINTS = tuple(sum(SPLIT_SIZES[:i + 1]) for i in range(len(SPLIT_SIZES) - 1))
PROJ_WIDTH = sum(SPLIT_SIZES)
FORGET_OFFSET = SPLIT_POINTS[7] + M_HEADS

kernel_name = 'nsa_mlstm_griffin_merge_step'


def rms_norm(x, g):
    xf = x.astype(jnp.float32)
    y = xf * lax.rsqrt(jnp.mean(xf * xf, axis=-1, keepdims=True) + EPS)
    return (y * g.astype(jnp.float32)).astype(x.dtype)


def rope(x, pos):
    half = ROPE_DIM // 2
    inv = ROPE_THETA ** (-jnp.arange(half, dtype=jnp.float32) / half)
    ang = pos.astype(jnp.float32)[:, None] * inv
    shp = (ang.shape[0],) + (1,) * (x.ndim - 3) + (half,)
    cos, sin = jnp.cos(ang).reshape(shp), jnp.sin(ang).reshape(shp)
    xr = x[..., :ROPE_DIM].astype(jnp.float32)
    x1, x2 = xr[..., :half], xr[..., half:]
    rot = jnp.concatenate([x1 * cos - x2 * sin, x2 * cos + x1 * sin], axis=-1)
    return jnp.concatenate([rot.astype(x.dtype), x[..., ROPE_DIM:]], axis=-1)


def masked_softmax(s, mask):
    s = jnp.where(mask, s.astype(jnp.float32), -jnp.inf)
    m = jnp.max(s, axis=-1, keepdims=True)
    m = jnp.where(jnp.isfinite(m), m, 0.0)
    e = jnp.exp(s - m)
    return e / jnp.maximum(jnp.sum(e, axis=-1, keepdims=True), 1e-30)


def mixer_inputs(x, pos, g_pre, w_in, b_in):
    B, T = x.shape[:2]
    h = rms_norm(x, g_pre)
    z = jnp.einsum('btd,de->bte', h, w_in) + b_in
    q, kvc, kvs, kvw, att_g, mqk, mv, mo, mif, merge = jnp.split(z, SPLIT_POINTS, axis=-1)
    q = rope(q.reshape(B, T, KV_GROUPS, HEADS_PER_GROUP, HEAD_DIM), pos)

    def kv_heads(kv):
        kv = kv.reshape(B, T, 2, KV_GROUPS, HEAD_DIM)
        return jnp.stack([rope(kv[:, :, 0], pos), kv[:, :, 1]], axis=2)

    return (q, kv_heads(kvc), kv_heads(kvs), kv_heads(kvw), att_g, mqk, mv, mo, mif, merge)


def cmp_half_proj(rows, w1):
    B, N = rows.shape[:2]
    ch = rows.reshape(B, N // CMP_STRIDE, CMP_STRIDE, 2, KV_GROUPS, HEAD_DIM)
    a = jnp.einsum('bclkgd,kldh->bckgh', ch, w1[:, :CMP_STRIDE])
    b = jnp.einsum('bclkgd,kldh->bckgh', ch, w1[:, CMP_STRIDE:])
    return a, b


def cmp_finish(a, b, pe, w1, w2):
    pe_bias = jnp.einsum('kld,kldh->kh', pe, w1)
    hid = jax.nn.gelu(a[:, :-1] + b[:, 1:] + pe_bias[:, None, :])
    return jnp.einsum('bnkgh,khd->bnkgd', hid, w2)


def cmp_attend(q, t_pos, kvc):
    nc = kvc.shape[1]
    end = jnp.arange(nc) * CMP_STRIDE + CMP_BLOCK - 1
    s = jnp.einsum('btghd,bngd->btghn', q, kvc[:, :, 0]) * ATT_SCALE
    mask = (end[None, :] <= t_pos[:, None])[None, :, None, None, :]
    p = masked_softmax(s, mask)
    o = jnp.einsum('btghn,bngd->btghd', p.astype(kvc.dtype), kvc[:, :, 1])
    return o, jnp.sum(p, axis=3)


def select_blocks(p_grp, t_pos, n_sb):
    nc = p_grp.shape[-1]
    i = jnp.arange(nc)[:, None]
    j = jnp.arange(n_sb)[None, :]
    overlap = ((i * CMP_STRIDE < (j + 1) * SLC_BLOCK) & (i * CMP_STRIDE + CMP_BLOCK > j * SLC_BLOCK)).astype(jnp.float32)
    score = jnp.einsum('btgn,nj->btgj', p_grp, overlap)
    cur = (t_pos // SLC_BLOCK)[:, None]
    valid = j <= cur
    forced = (j == 0) | (j == cur) | (j == cur - 1)
    score = jnp.where(forced[None, :, None, :], FORCE_SCORE, score)
    score = jnp.where(valid[None, :, None, :], score, -jnp.inf)
    val, idx = lax.top_k(score, min(N_SELECT, n_sb))
    return idx, jnp.isfinite(val)


def slc_attend(q, t_pos, kv_sel, idx, ok):
    B, T, G, HPG = q.shape[:4]
    s = jnp.einsum('btghd,btgsld->btghsl', q, kv_sel[..., 0, :]) * ATT_SCALE
    kpos = idx[..., None] * SLC_BLOCK + jnp.arange(SLC_BLOCK)
    mask = ok[..., None] & (kpos <= t_pos[None, :, None, None, None])
    p = masked_softmax(s.reshape(B, T, G, HPG, -1), mask.reshape(B, T, G, 1, -1)).reshape(s.shape)
    return jnp.einsum('btghsl,btgsld->btghd', p.astype(kv_sel.dtype), kv_sel[..., 1, :])


def win_attend(q, t_pos, kvw, k_pos):
    s = jnp.einsum('btghd,bngd->btghn', q, kvw[:, :, 0]) * ATT_SCALE
    kp, tp = k_pos[None, :], t_pos[:, None]
    mask = (kp <= tp) & (kp > tp - WINDOW) & (kp >= 0)
    p = masked_softmax(s, mask[None, :, None, None, :])
    return jnp.einsum('btghn,bngd->btghd', p.astype(kvw.dtype), kvw[:, :, 1])


def nsa_gate(o_c, o_s, o_w, gate_logits):
    B, T = gate_logits.shape[:2]
    g = jax.nn.sigmoid(gate_logits.astype(jnp.float32)).reshape(B, T, 3, KV_GROUPS, HEADS_PER_GROUP, 1)
    o = g[:, :, 0] * o_c + g[:, :, 1] * o_s + g[:, :, 2] * o_w
    return o.reshape(B, T, N_HEADS * HEAD_DIM)


def nsa_prompt(q, kvc_rows, kvs_rows, kvw_rows, gate_logits, cmp_pe, cmp_w1, cmp_w2):
    B, S = q.shape[:2]
    a, b = cmp_half_proj(kvc_rows, cmp_w1)
    kvc = cmp_finish(a, b, cmp_pe, cmp_w1, cmp_w2)
    n_sb = S // SLC_BLOCK
    blocks = kvs_rows.reshape(B, n_sb, SLC_BLOCK, 2, KV_GROUPS, HEAD_DIM)
    kvw_pad = jnp.concatenate([jnp.zeros((B, WINDOW) + kvw_rows.shape[2:], kvw_rows.dtype), kvw_rows], axis=1)
    bi = jnp.arange(B)[:, None, None, None]
    gi = jnp.arange(KV_GROUPS)[None, None, :, None]

    def query_block(args):
        qb, s0 = args
        t_pos = s0 + jnp.arange(NSA_Q_BLOCK)
        o_c, p_grp = cmp_attend(qb, t_pos, kvc)
        idx, ok = select_blocks(p_grp, t_pos, n_sb)
        kv_sel = blocks[bi, idx, :, :, gi, :]
        o_s = slc_attend(qb, t_pos, kv_sel, idx, ok)
        band = lax.dynamic_slice_in_dim(kvw_pad, s0, WINDOW + NSA_Q_BLOCK, axis=1)
        k_pos = s0 - WINDOW + jnp.arange(WINDOW + NSA_Q_BLOCK)
        o_w = win_attend(qb, t_pos, band, k_pos)
        return o_c, o_s, o_w

    n_qb = S // NSA_Q_BLOCK
    q_blocks = jnp.moveaxis(q.reshape(B, n_qb, NSA_Q_BLOCK, KV_GROUPS, HEADS_PER_GROUP, HEAD_DIM), 1, 0)
    starts = jnp.arange(n_qb) * NSA_Q_BLOCK
    o_c, o_s, o_w = lax.map(query_block, (q_blocks, starts))

    def unblock(o):
        return jnp.moveaxis(o, 0, 1).reshape(B, S, KV_GROUPS, HEADS_PER_GROUP, HEAD_DIM)

    return nsa_gate(unblock(o_c), unblock(o_s), unblock(o_w), gate_logits)


def nsa_sample(q, kvc_new, kvs_new, kvw_new, gate_logits, cache_kv_cmp, cache_kv_slc, cache_win_kv, page_table,
               cmp_pe, cmp_w1, cmp_w2):
    DB, T = q.shape[:2]
    n_pages = page_table.shape[1]
    past = n_pages * PAGE_SIZE
    t_pos = past + jnp.arange(T)
    t_pad = -(-T // SLC_BLOCK) * SLC_BLOCK
    pad = ((0, 0), (0, t_pad - T), (0, 0), (0, 0), (0, 0))

    def seq_half_proj(pt_row):
        rows = cache_kv_cmp[pt_row].reshape(1, past, 2, KV_GROUPS, HEAD_DIM)
        a_, b_ = cmp_half_proj(rows, cmp_w1)
        return a_[0], b_[0]

    a_past, b_past = lax.map(seq_half_proj, page_table)
    a_new, b_new = cmp_half_proj(jnp.pad(kvc_new, pad), cmp_w1)
    kvc = cmp_finish(jnp.concatenate([a_past, a_new], axis=1), jnp.concatenate([b_past, b_new], axis=1),
                     cmp_pe, cmp_w1, cmp_w2)
    o_c, p_grp = cmp_attend(q, t_pos, kvc)

    nb_past = past // SLC_BLOCK
    nb_tail = t_pad // SLC_BLOCK
    idx, ok = select_blocks(p_grp, t_pos, nb_past + nb_tail)
    bpp = PAGE_SIZE // SLC_BLOCK
    pool = cache_kv_slc.reshape((cache_kv_slc.shape[0], bpp, SLC_BLOCK) + cache_kv_slc.shape[2:])
    bi = jnp.arange(DB)[:, None, None, None]
    gi = jnp.arange(KV_GROUPS)[None, None, :, None]
    jp = jnp.minimum(idx, nb_past - 1)
    page = page_table[bi, jp // bpp]
    from_pool = pool[page, jp % bpp, :, :, gi, :]
    tail = jnp.pad(kvs_new, pad).reshape(DB, nb_tail, SLC_BLOCK, 2, KV_GROUPS, HEAD_DIM)
    jt = jnp.clip(idx - nb_past, 0, nb_tail - 1)
    from_tail = tail[bi, jt, :, :, gi, :]
    kv_sel = jnp.where((idx < nb_past)[..., None, None, None], from_pool, from_tail)
    o_s = slc_attend(q, t_pos, kv_sel, idx, ok)

    wb = cache_win_kv.shape[1]
    kvw_all = jnp.concatenate([cache_win_kv, kvw_new.astype(cache_win_kv.dtype)], axis=1)
    k_pos = past - wb + jnp.arange(wb + T)
    o_w = win_attend(q, t_pos, kvw_all, k_pos)
    return nsa_gate(o_c, o_s, o_w, gate_logits), kvw_all[:, T:]


def causal_conv(x, buf, w, b):
    T = x.shape[1]
    xx = jnp.concatenate([buf.astype(x.dtype), x], axis=1)
    out = b
    for j in range(CONV_W):
        out = out + xx[:, j:j + T] * w[j]
    return out, xx[:, T:]


def mlstm_chunkwise(q, k, v, i_pre, f_pre, c0, n0, m0):
    B, T, H, DK = q.shape
    f32 = jnp.float32
    L = min(M_CHUNK, T)
    n_ch = -(-T // L)
    pad = n_ch * L - T
    q = q.astype(f32)
    k = k.astype(f32) * (DK ** -0.5)
    v = v.astype(f32)
    log_f = jax.nn.log_sigmoid(f_pre.astype(f32))
    ig = i_pre.astype(f32)
    if pad:
        p4 = ((0, 0), (0, pad), (0, 0), (0, 0))
        q, k, v = jnp.pad(q, p4), jnp.pad(k, p4), jnp.pad(v, p4)
        log_f = jnp.pad(log_f, p4[:3])
        ig = jnp.pad(ig, p4[:3], constant_values=NEG_BIG)

    def to_chunks(x):
        x = x.reshape((B, n_ch, L) + x.shape[2:])
        return jnp.transpose(x, (1, 0, 3, 2) + tuple(range(4, x.ndim)))

    tri = jnp.tril(jnp.ones((L, L), dtype=bool))

    def step(carry, xs):
        c, n, m = carry
        qc, kc, vc, ic, fc = xs
        b = jnp.cumsum(fc, axis=-1)
        d = jnp.where(tri, b[..., :, None] - b[..., None, :] + ic[..., None, :], NEG_BIG)
        m_t = jnp.maximum(m[..., None] + b, jnp.max(d, axis=-1))
        w = jnp.exp(d - m_t[..., None])
        inter = jnp.exp(m[..., None] + b - m_t)
        s = jnp.einsum('bhtd,bhsd->bhts', qc, kc) * w
        num = inter[..., None] * jnp.einsum('bhtd,bhdv->bhtv', qc, c) + jnp.einsum('bhts,bhsv->bhtv', s, vc)
        den = inter * jnp.einsum('bhtd,bhd->bht', qc, n) + jnp.sum(s, axis=-1)
        h = num / jnp.maximum(jnp.abs(den), jnp.exp(-m_t))[..., None]
        m_new = m_t[..., -1]
        decay = jnp.exp(m + b[..., -1] - m_new)
        g = jnp.exp(ic + b[..., -1:] - b - m_new[..., None])
        c_new = decay[..., None, None] * c + jnp.einsum('bhs,bhsd,bhsv->bhdv', g, kc, vc)
        n_new = decay[..., None] * n + jnp.einsum('bhs,bhsd->bhd', g, kc)
        return (c_new, n_new, m_new), h

    xs = (to_chunks(q), to_chunks(k), to_chunks(v), to_chunks(ig), to_chunks(log_f))
    (c, n, m), hs = lax.scan(step, (c0.astype(f32), n0.astype(f32), m0.astype(f32)), xs)
    hs = jnp.transpose(hs, (1, 0, 3, 2, 4)).reshape(B, n_ch * L, H, v.shape[-1])[:, :T]
    return hs, c, n, m


def mlstm_branch(mqk, mv, mo, mif, conv_buf, conv_w, conv_b, norm_w, c0, n0, m0):
    B, T = mqk.shape[:2]
    qk, new_buf = causal_conv(mqk, conv_buf, conv_w, conv_b)
    qk = jax.nn.silu(qk)
    q = qk[..., :M_INNER].reshape(B, T, M_HEADS, M_HEAD_DIM)
    k = qk[..., M_INNER:].reshape(B, T, M_HEADS, M_HEAD_DIM)
    v = mv.reshape(B, T, M_HEADS, M_HEAD_DIM)
    h, c, n, m = mlstm_chunkwise(q, k, v, mif[..., :M_HEADS], mif[..., M_HEADS:], c0, n0, m0)
    h = jax.nn.sigmoid(mo.astype(jnp.float32)).reshape(B, T, M_HEADS, M_HEAD_DIM) * h
    mu = jnp.mean(h, axis=-1, keepdims=True)
    var = jnp.mean(jnp.square(h - mu), axis=-1, keepdims=True)
    h = ((h - mu) * lax.rsqrt(var + EPS)).reshape(B, T, M_INNER) * norm_w.astype(jnp.float32)
    return h, new_buf, c, n, m


def layer_tail(x, att, mh, merge, w_out, g_mix_post, g_mlp_pre, w_up, w_down, g_mlp_post):
    gates = jax.nn.sigmoid(merge.astype(jnp.float32))
    mixed = (gates[..., :D_MODEL] * att + gates[..., D_MODEL:] * mh).astype(x.dtype)
    x = x + rms_norm(jnp.einsum('btd,de->bte', mixed, w_out), g_mix_post)
    h = rms_norm(x, g_mlp_pre)
    u = jnp.square(jax.nn.relu(jnp.einsum('btd,df->btf', h, w_up)))
    return x + rms_norm(jnp.einsum('btf,fd->btd', u, w_down), g_mlp_post)


def setup_inputs(seed: int = 0) -> dict:
    key = jax.random.key(seed)
    ks = jax.random.split(key, 32)
    f32 = jnp.float32
    n_pages = PAST_LEN // PAGE_SIZE
    n_used = DEC_BATCH * n_pages
    n_phys = (5 * n_used + 3) // 4
    wb = min(WINDOW, PAST_LEN)
    nrm = lambda k, shape, s=1.0: s * jax.random.normal(k, shape, f32)
    b_in = nrm(ks[12], (PROJ_WIDTH,), 0.02)
    b_in = b_in.at[FORGET_OFFSET:FORGET_OFFSET + M_HEADS].add(jnp.linspace(3.0, 6.0, M_HEADS))
    page_table = jax.random.permutation(ks[10], n_phys)[:n_used].reshape(DEC_BATCH, n_pages).astype(jnp.int32)
    return {
        'x_prompt': nrm(ks[0], (BATCH, SEQ, D_MODEL)),
        'x_sample': nrm(ks[1], (DEC_BATCH, DEC_SEQ, D_MODEL)),
        'cache_kv_cmp': nrm(ks[2], (n_phys, PAGE_SIZE, 2, KV_GROUPS, HEAD_DIM)),
        'cache_kv_slc': nrm(ks[3], (n_phys, PAGE_SIZE, 2, KV_GROUPS, HEAD_DIM)),
        'cache_win_kv': nrm(ks[4], (DEC_BATCH, wb, 2, KV_GROUPS, HEAD_DIM)),
        'state_mlstm_c': nrm(ks[5], (DEC_BATCH, M_HEADS, M_HEAD_DIM, M_HEAD_DIM), 0.05),
        'state_mlstm_n': nrm(ks[6], (DEC_BATCH, M_HEADS, M_HEAD_DIM), 0.05),
        'state_mlstm_m': nrm(ks[7], (DEC_BATCH, M_HEADS)),
        'state_mlstm_conv': nrm(ks[8], (DEC_BATCH, CONV_W - 1, 2 * M_INNER)),
        'page_table': page_table,
        'g_mix_pre': 1.0 + nrm(ks[9], (D_MODEL,), 0.02),
        'w_in': nrm(ks[11], (D_MODEL, PROJ_WIDTH), D_MODEL ** -0.5),
        'b_in': b_in,
        'cmp_pe': nrm(ks[13], (2, CMP_BLOCK, HEAD_DIM), 0.1),
        'cmp_w1': nrm(ks[14], (2, CMP_BLOCK, HEAD_DIM, HEAD_DIM), (CMP_BLOCK * HEAD_DIM) ** -0.5),
        'cmp_w2': nrm(ks[15], (2, HEAD_DIM, HEAD_DIM), HEAD_DIM ** -0.5),
        'conv_w': nrm(ks[16], (CONV_W, 2 * M_INNER), CONV_W ** -0.5),
        'conv_b': nrm(ks[17], (2 * M_INNER,), 0.02),
        'mlstm_norm_w': 1.0 + nrm(ks[18], (M_INNER,), 0.02),
        'w_out': nrm(ks[19], (D_MODEL, D_MODEL), D_MODEL ** -0.5),
        'g_mix_post': 1.0 + nrm(ks[20], (D_MODEL,), 0.02),
        'g_mlp_pre': 1.0 + nrm(ks[21], (D_MODEL,), 0.02),
        'w_up': nrm(ks[22], (D_MODEL, D_FF), D_MODEL ** -0.5),
        'w_down': nrm(ks[23], (D_FF, D_MODEL), D_FF ** -0.5),
        'g_mlp_post': 1.0 + nrm(ks[24], (D_MODEL,), 0.02),
    }


def reference(x_prompt, x_sample, cache_kv_cmp, cache_kv_slc, cache_win_kv, state_mlstm_c, state_mlstm_n,
              state_mlstm_m, state_mlstm_conv, page_table, g_mix_pre, w_in, b_in, cmp_pe, cmp_w1, cmp_w2,
              conv_w, conv_b, mlstm_norm_w, w_out, g_mix_post, g_mlp_pre, w_up, w_down, g_mlp_post):
    f32 = jnp.float32
    B, S = x_prompt.shape[:2]
    DB, T = x_sample.shape[:2]
    past = page_table.shape[1] * PAGE_SIZE
    pos_p = jnp.arange(S)
    pos_s = past + jnp.arange(T)
    for _ in range(DEPTH):
        q, kvc, kvs, kvw, att_g, mqk, mv, mo, mif, merge = mixer_inputs(x_prompt, pos_p, g_mix_pre, w_in, b_in)
        att = nsa_prompt(q, kvc, kvs, kvw, att_g, cmp_pe, cmp_w1, cmp_w2)
        mh, conv_p, c_p, n_p, m_p = mlstm_branch(
            mqk, mv, mo, mif, jnp.zeros((B, CONV_W - 1, 2 * M_INNER), mqk.dtype), conv_w, conv_b, mlstm_norm_w,
            jnp.zeros((B, M_HEADS, M_HEAD_DIM, M_HEAD_DIM), f32), jnp.zeros((B, M_HEADS, M_HEAD_DIM), f32),
            jnp.zeros((B, M_HEADS), f32))
        y_prompt = layer_tail(x_prompt, att, mh, merge, w_out, g_mix_post, g_mlp_pre, w_up, w_down, g_mlp_post)
        kv_cmp_p, kv_slc_p, win_p = kvc, kvs, kvw[:, S - min(WINDOW, S):]

        q, kvc, kvs, kvw, att_g, mqk, mv, mo, mif, merge = mixer_inputs(x_sample, pos_s, g_mix_pre, w_in, b_in)
        att, win_s = nsa_sample(q, kvc, kvs, kvw, att_g, cache_kv_cmp, cache_kv_slc, cache_win_kv, page_table,
                                cmp_pe, cmp_w1, cmp_w2)
        mh, conv_s, c_s, n_s, m_s = mlstm_branch(mqk, mv, mo, mif, state_mlstm_conv, conv_w, conv_b, mlstm_norm_w,
                                                 state_mlstm_c, state_mlstm_n, state_mlstm_m)
        y_sample = layer_tail(x_sample, att, mh, merge, w_out, g_mix_post, g_mlp_pre, w_up, w_down, g_mlp_post)
        kv_cmp_s, kv_slc_s = kvc, kvs
        x_prompt, x_sample = y_prompt, y_sample
    return (y_prompt, y_sample, kv_cmp_p, kv_cmp_s, kv_slc_p, kv_slc_s, win_p, win_s,
            c_p, c_s, n_p, n_s, m_p, m_s, conv_p, conv_s)
```

```python
import functools

import jax
import jax.numpy as jnp
from jax import lax
from jax.experimental import pallas as pl
from jax.experimental.pallas import tpu as pltpu

F32 = jnp.float32
BF16 = jnp.bfloat16

D_MODEL = 2048
N_HEADS = 16
HEAD_DIM = 128
KV_GROUPS = 4
HEADS_PER_GROUP = N_HEADS // KV_GROUPS
ROPE_DIM = HEAD_DIM // 4
ROPE_HALF = ROPE_DIM // 2
ROPE_THETA = 500000.0
CMP_STRIDE = 16
CMP_BLOCK = 2 * CMP_STRIDE
SLC_BLOCK = 64
N_SELECT = 16
WINDOW = 512
ATT_SCALE = HEAD_DIM ** -0.5
FORCE_SCORE = 1e9
M_HEADS = 4
M_INNER = D_MODEL
M_HEAD_DIM = M_INNER // M_HEADS
CONV_W = 4
NEG_BIG = -1e30
D_FF = 4 * D_MODEL
EPS = 1e-6
PAGE_SIZE = 128
KV_WIDTH = 2 * KV_GROUPS * HEAD_DIM
KG = 2 * KV_GROUPS

LANES = 128
VMEM_LIMIT = 56 * 1024 * 1024

OFF_Q = 0
OFF_KVC = OFF_Q + N_HEADS * HEAD_DIM
OFF_KVS = OFF_KVC + KV_WIDTH
OFF_KVW = OFF_KVS + KV_WIDTH
OFF_REST = OFF_KVW + KV_WIDTH
REST_WIDTH = 2 * M_INNER + M_INNER + M_INNER + 2 * D_MODEL
OFF_SMALL = OFF_REST + REST_WIDTH
PACKED_WIDTH = OFF_SMALL + LANES
N_GATE = 3 * N_HEADS
R_MQK, R_MV, R_MO, R_MERGE = 0, 2 * M_INNER, 3 * M_INNER, 4 * M_INNER


def _cparams(sem):
    return pltpu.CompilerParams(dimension_semantics=sem, vmem_limit_bytes=VMEM_LIMIT)


def _split3(x):
    hi = x.astype(BF16)
    r1 = x - hi.astype(F32)
    mid = r1.astype(BF16)
    lo = (r1 - mid.astype(F32)).astype(BF16)
    return hi, mid, lo


def _dot_exact01(x, e):
    hi, mid, lo = _split3(x)
    d = lambda a: jnp.dot(a, e, preferred_element_type=F32)
    return d(hi) + d(mid) + d(lo)


def _proj_kernel(*refs, tn, rope_mod, precise):
    if rope_mod:
        x_ref, g_ref, w_ref, b_ref, c_ref, s1_ref, s2_ref, o_ref, h_scr = refs
    else:
        x_ref, g_ref, w_ref, b_ref, o_ref, h_scr = refs
    j = pl.program_id(1)

    @pl.when(j == 0)
    def _():
        xf = x_ref[...]
        ms = jnp.mean(xf * xf, axis=-1, keepdims=True)
        h_scr[...] = (xf * lax.rsqrt(ms + EPS) * g_ref[...]).astype(h_scr.dtype)

    if precise:
        acc = jnp.dot(h_scr[...], w_ref[...], preferred_element_type=F32,
                      precision=lax.Precision.HIGHEST)
    else:
        acc = jnp.dot(h_scr[...], w_ref[...], preferred_element_type=F32)
    acc = acc + b_ref[...]

    if not rope_mod:
        o_ref[...] = acc.astype(o_ref.dtype)
        return

    def roped():
        c, s1, s2 = c_ref[...], s1_ref[...], s2_ref[...]
        for hh in range(tn // LANES):
            a = acc[:, hh * LANES:(hh + 1) * LANES]
            r = a * c + pltpu.roll(a, ROPE_HALF, 1) * s1 + pltpu.roll(a, LANES - ROPE_HALF, 1) * s2
            o_ref[:, hh * LANES:(hh + 1) * LANES] = r.astype(o_ref.dtype)

    if rope_mod == 1:
        roped()
    else:
        pl.when(j % rope_mod == 0)(roped)

        @pl.when(j % rope_mod != 0)
        def _():
            o_ref[...] = acc.astype(o_ref.dtype)


def _norm_proj(x, g, w, b, *, col0, ncols, tm, tn, out_dtype, rope=None, rope_mod=0, precise=False):
    m, d = x.shape
    assert m % tm == 0 and ncols % tn == 0 and col0 % tn == 0
    jb = col0 // tn
    in_specs = [
        pl.BlockSpec((tm, d), lambda i, j: (i, 0)),
        pl.BlockSpec((1, d), lambda i, j: (0, 0)),
        pl.BlockSpec((d, tn), lambda i, j: (0, j + jb)),
        pl.BlockSpec((1, tn), lambda i, j: (0, j + jb)),
    ]
    args = [x, g, w, b]
    if rope_mod:
        period = rope[0].shape[0] // tm
        assert rope[0].shape[0] % tm == 0
        for tab in rope:
            in_specs.append(pl.BlockSpec((tm, LANES), lambda i, j: (i % period, 0)))
            args.append(tab)
    return pl.pallas_call(
        functools.partial(_proj_kernel, tn=tn, rope_mod=rope_mod, precise=precise),
        grid=(m // tm, ncols // tn),
        in_specs=in_specs,
        out_specs=pl.BlockSpec((tm, tn), lambda i, j: (i, j)),
        out_shape=jax.ShapeDtypeStruct((m, ncols), out_dtype),
        scratch_shapes=[pltpu.VMEM((tm, d), F32 if precise else BF16)],
        compiler_params=_cparams(("parallel", "arbitrary")),
    )(*args)


def _rope_tables(pos):
    inv = ROPE_THETA ** (-jnp.arange(ROPE_HALF, dtype=F32) / ROPE_HALF)
    ang = pos.astype(F32)[:, None] * inv
    cos, sin = jnp.cos(ang), jnp.sin(ang)
    p = pos.shape[0]
    rest = HEAD_DIM - ROPE_DIM
    c = jnp.concatenate([cos, cos, jnp.ones((p, rest), F32)], axis=1)
    s1 = jnp.concatenate([jnp.zeros((p, ROPE_HALF), F32), sin, jnp.zeros((p, rest), F32)], axis=1)
    s2 = jnp.concatenate([-sin, jnp.zeros((p, ROPE_HALF + rest), F32)], axis=1)
    return c, s1, s2


def _log_sigmoid(x):
    return jnp.minimum(x, 0.0) - jnp.log1p(jnp.exp(-jnp.abs(x)))


def _mlstm_kernel(*refs, L, zero_init):
    if zero_init:
        (q_ref, k_ref, v_ref, o_ref, gc_ref, gr_ref, cwq_ref, cwk_ref, cbq_ref, cbk_ref, cvq_ref, cvk_ref, nw_ref,
         h_ref, c_ref, n_ref, m_ref, xq_scr, xk_scr) = refs
    else:
        (q_ref, k_ref, v_ref, o_ref, gc_ref, gr_ref, cwq_ref, cwk_ref, cbq_ref, cbk_ref, cvq_ref, cvk_ref, nw_ref,
         c0_ref, n0_ref, m0_ref, h_ref, c_ref, n_ref, m_ref, xq_scr, xk_scr) = refs
    ci = pl.program_id(2)
    PAD = 8

    @pl.when(ci == 0)
    def _():
        if zero_init:
            c_ref[...] = jnp.zeros_like(c_ref)
            n_ref[...] = jnp.zeros_like(n_ref)
            m_ref[...] = jnp.zeros_like(m_ref)
        else:
            c_ref[...] = c0_ref[...]
            n_ref[...] = n0_ref[...]
            m_ref[...] = m0_ref[...]
        xq_scr[0:PAD, :] = cvq_ref[0]
        xk_scr[0:PAD, :] = cvk_ref[0]

    xq_scr[PAD:PAD + L, :] = q_ref[...]
    xk_scr[PAD:PAD + L, :] = k_ref[...]

    def conv_silu(x_scr, w_ref, b_ref):
        acc = b_ref[...] + jnp.zeros((L, M_HEAD_DIM), F32)
        for j in range(CONV_W):
            acc = acc + x_scr[pl.ds(PAD - (CONV_W - 1) + j, L), :] * w_ref[j:j + 1, :]
        return acc * jax.nn.sigmoid(acc)

    qs = conv_silu(xq_scr, cwq_ref, cbq_ref)
    ks = conv_silu(xk_scr, cwk_ref, cbk_ref) * (M_HEAD_DIM ** -0.5)
    xq_scr[0:PAD, :] = xq_scr[L:L + PAD, :]
    xk_scr[0:PAD, :] = xk_scr[L:L + PAD, :]

    gc = gc_ref[0, 0]
    gr = gr_ref[0, 0]
    i_col, f_col = gc[:, 0:1], _log_sigmoid(gc[:, 1:2])
    i_row, f_row = gr[0:1, :], _log_sigmoid(gr[1:2, :])
    t_idx = lax.broadcasted_iota(jnp.int32, (L, L), 0)
    s_idx = lax.broadcasted_iota(jnp.int32, (L, L), 1)
    tril = s_idx <= t_idx
    b_col = jnp.sum(jnp.where(tril, f_row, 0.0), axis=1, keepdims=True)
    b_row = jnp.sum(jnp.where(t_idx <= s_idx, f_col, 0.0), axis=0, keepdims=True)
    m_prev = m_ref[0, 0]
    dmat = jnp.where(tril, b_col - b_row + i_row, NEG_BIG)
    m_t = jnp.maximum(m_prev + b_col, jnp.max(dmat, axis=1, keepdims=True))
    w = jnp.exp(dmat - m_t)
    inter = jnp.exp(m_prev + b_col - m_t)

    c_prev = c_ref[0, 0]
    n_prev = n_ref[0, 0]
    qb, kb, vb = qs.astype(BF16), ks.astype(BF16), v_ref[...].astype(BF16)
    s = lax.dot_general(qb, kb, (((1,), (1,)), ((), ())), preferred_element_type=F32) * w
    num = inter * jnp.dot(qb, c_prev.astype(BF16), preferred_element_type=F32) \
        + jnp.dot(s.astype(BF16), vb, preferred_element_type=F32)
    den = inter * jnp.sum(qs * n_prev, axis=1, keepdims=True) + jnp.sum(s, axis=1, keepdims=True)
    h = num / jnp.maximum(jnp.abs(den), jnp.exp(-m_t))

    m_new = m_t[L - 1:L, :]
    b_last = b_col[L - 1:L, :]
    decay = jnp.exp(m_prev + b_last - m_new)
    g_col = jnp.exp(i_col + b_last - b_col - m_new)
    kg = ks * g_col
    c_ref[0, 0] = decay * c_prev + lax.dot_general(kg.astype(BF16), vb, (((0,), (0,)), ((), ())),
                                                   preferred_element_type=F32)
    n_ref[0, 0] = decay * n_prev + jnp.sum(kg, axis=0, keepdims=True)
    m_ref[0, 0] = m_new

    ho = jax.nn.sigmoid(o_ref[...]) * h
    mu = jnp.mean(ho, axis=1, keepdims=True)
    var = jnp.mean(jnp.square(ho - mu), axis=1, keepdims=True)
    h_ref[...] = ((ho - mu) * lax.rsqrt(var + EPS) * nw_ref[...]).astype(h_ref.dtype)


def _mlstm(rest, gates_col, gates_row, conv_w, conv_b, conv0, norm_w, state, *, nb, t, L):
    nc = t // L
    hb = M_HEAD_DIM
    zero_init = state is None
    row = lambda b, h, c: b * nc + c
    in_specs = [
        pl.BlockSpec((L, hb), lambda b, h, c: (row(b, h, c), R_MQK // hb + h)),
        pl.BlockSpec((L, hb), lambda b, h, c: (row(b, h, c), (R_MQK + M_INNER) // hb + h)),
        pl.BlockSpec((L, hb), lambda b, h, c: (row(b, h, c), R_MV // hb + h)),
        pl.BlockSpec((L, hb), lambda b, h, c: (row(b, h, c), R_MO // hb + h)),
        pl.BlockSpec((1, 1, L, 2), lambda b, h, c: (b, h, c, 0)),
        pl.BlockSpec((1, 1, 2, L), lambda b, h, c: (b, h, 0, c)),
        pl.BlockSpec((CONV_W, hb), lambda b, h, c: (0, h)),
        pl.BlockSpec((CONV_W, hb), lambda b, h, c: (0, M_HEADS + h)),
        pl.BlockSpec((1, hb), lambda b, h, c: (0, h)),
        pl.BlockSpec((1, hb), lambda b, h, c: (0, M_HEADS + h)),
        pl.BlockSpec((1, 8, hb), lambda b, h, c: (b, 0, h)),
        pl.BlockSpec((1, 8, hb), lambda b, h, c: (b, 0, M_HEADS + h)),
        pl.BlockSpec((1, hb), lambda b, h, c: (0, h)),
    ]
    args = [rest, rest, rest, rest, gates_col, gates_row, conv_w, conv_w, conv_b, conv_b, conv0, conv0, norm_w]
    st_specs = [
        pl.BlockSpec((1, 1, hb, hb), lambda b, h, c: (b, h, 0, 0)),
        pl.BlockSpec((1, 1, 1, hb), lambda b, h, c: (b, h, 0, 0)),
        pl.BlockSpec((1, 1, 1, 1), lambda b, h, c: (b, h, 0, 0)),
    ]
    if not zero_init:
        in_specs += st_specs
        args += list(state)
    out_shape = [
        jax.ShapeDtypeStruct((nb * t, M_INNER), F32),
        jax.ShapeDtypeStruct((nb, M_HEADS, hb, hb), F32),
        jax.ShapeDtypeStruct((nb, M_HEADS, 1, hb), F32),
        jax.ShapeDtypeStruct((nb, M_HEADS, 1, 1), F32),
    ]
    out_specs = [pl.BlockSpec((L, hb), lambda b, h, c: (row(b, h, c), h))] + st_specs
    return pl.pallas_call(
        functools.partial(_mlstm_kernel, L=L, zero_init=zero_init),
        grid=(nb, M_HEADS, nc),
        in_specs=in_specs,
        out_specs=out_specs,
        out_shape=out_shape,
        scratch_shapes=[pltpu.VMEM((L + 8, hb), F32), pltpu.VMEM((L + 8, hb), F32)],
        compiler_params=_cparams(("parallel", "parallel", "arbitrary")),
    )(*args)


def _rms(y, g):
    return y * lax.rsqrt(jnp.mean(y * y, axis=-1, keepdims=True) + EPS) * g


def _merge_out_kernel(ga_ref, gm_ref, att_ref, mh_ref, x_ref, w_ref, g_ref, o_ref):
    mixed = jax.nn.sigmoid(ga_ref[...]) * att_ref[...] + jax.nn.sigmoid(gm_ref[...]) * mh_ref[...]
    y = jnp.dot(mixed.astype(BF16), w_ref[...], preferred_element_type=F32)
    o_ref[...] = x_ref[...] + _rms(y, g_ref[...])


def _merge_out(rest, att, mh, x, w_out, g_post, *, tm):
    m, d = x.shape
    cb = R_MERGE // d
    row = pl.BlockSpec((tm, d), lambda i: (i, 0))
    return pl.pallas_call(
        _merge_out_kernel,
        grid=(m // tm,),
        in_specs=[
            pl.BlockSpec((tm, d), lambda i: (i, cb)),
            pl.BlockSpec((tm, d), lambda i: (i, cb + 1)),
            row, row, row,
            pl.BlockSpec((d, d), lambda i: (0, 0)),
            pl.BlockSpec((1, d), lambda i: (0, 0)),
        ],
        out_specs=row,
        out_shape=jax.ShapeDtypeStruct((m, d), F32),
        compiler_params=_cparams(("parallel",)),
    )(rest, rest, att, mh, x, w_out, g_post)


def _mlp_kernel(x_ref, gpre_ref, wu_ref, wd_ref, gpost_ref, o_ref, h_scr, acc_scr):
    f = pl.program_id(1)

    @pl.when(f == 0)
    def _():
        h_scr[...] = _rms(x_ref[...], gpre_ref[...]).astype(BF16)
        acc_scr[...] = jnp.zeros_like(acc_scr)

    u = jnp.dot(h_scr[...], wu_ref[...], preferred_element_type=F32)
    u = jnp.square(jnp.maximum(u, 0.0))
    acc_scr[...] += jnp.dot(u.astype(BF16), wd_ref[...], preferred_element_type=F32)

    @pl.when(f == pl.num_programs(1) - 1)
    def _():
        o_ref[...] = x_ref[...] + _rms(acc_scr[...], gpost_ref[...])


def _mlp(x, g_pre, w_up, w_down, g_post, *, tm, tf):
    m, d = x.shape
    ff = w_up.shape[1]
    return pl.pallas_call(
        _mlp_kernel,
        grid=(m // tm, ff // tf),
        in_specs=[
            pl.BlockSpec((tm, d), lambda i, f: (i, 0)),
            pl.BlockSpec((1, d), lambda i, f: (0, 0)),
            pl.BlockSpec((d, tf), lambda i, f: (0, f)),
            pl.BlockSpec((tf, d), lambda i, f: (f, 0)),
            pl.BlockSpec((1, d), lambda i, f: (0, 0)),
        ],
        out_specs=pl.BlockSpec((tm, d), lambda i, f: (i, 0)),
        out_shape=jax.ShapeDtypeStruct((m, d), F32),
        scratch_shapes=[pltpu.VMEM((tm, d), BF16), pltpu.VMEM((tm, d), F32)],
        compiler_params=_cparams(("parallel", "arbitrary")),
    )(x, g_pre, w_up, w_down, g_post)


def _pe_bias_kernel(pet_ref, w1_ref, o_ref):
    rows = []
    for k in range(2):
        acc = jnp.zeros((1, HEAD_DIM), F32)
        for l in range(CMP_BLOCK):
            acc = acc + jnp.sum(pet_ref[k][:, l:l + 1] * w1_ref[k, l], axis=0, keepdims=True)
        rows += [acc] * KV_GROUPS
    o_ref[...] = jnp.concatenate(rows, axis=0)


def _pe_bias(cmp_pe, cmp_w1):
    pet = jnp.transpose(cmp_pe, (0, 2, 1))
    return pl.pallas_call(
        _pe_bias_kernel,
        out_shape=jax.ShapeDtypeStruct((KG, HEAD_DIM), F32),
        compiler_params=pltpu.CompilerParams(vmem_limit_bytes=VMEM_LIMIT),
    )(pet, cmp_w1)


def _cmp_kernel(pt_ref, *refs, P, has_tail):
    page_refs = refs[:P]
    if has_tail:
        tail_ref, wc_ref, w2_ref, peb_ref, o_ref, ot_ref, carry = refs[P:]
    else:
        wc_ref, w2_ref, peb_ref, o_ref, carry = refs[P:]
    p = pl.program_id(1)

    @pl.when(p == 0)
    def _():
        carry[...] = jnp.zeros_like(carry)

    def is_k(rows):
        return (lax.broadcasted_iota(jnp.int32, (rows, 1), 0) % KG) < KV_GROUPS

    def half_proj(xs):
        rows = xs[0].shape[0]
        acc = jnp.zeros((rows, 4 * HEAD_DIM), F32)
        for lp in range(CMP_STRIDE // 2):
            lhs = jnp.concatenate([xs[lp], xs[lp + CMP_STRIDE // 2]], axis=1).astype(BF16)
            acc = acc + jnp.dot(lhs, wc_ref[lp], preferred_element_type=F32)
        ik = is_k(rows)
        a = jnp.where(ik, acc[:, 0:128], acc[:, 256:384])
        b = jnp.where(ik, acc[:, 128:256], acc[:, 384:512])
        return a, b

    def finish(a_prev, b):
        rows = b.shape[0]
        peb = jnp.concatenate([peb_ref[...]] * (rows // KG), axis=0)
        hid = jax.nn.gelu(a_prev + b + peb)
        o2 = jnp.dot(hid.astype(BF16), w2_ref[...], preferred_element_type=F32)
        return jnp.where(is_k(rows), o2[:, 0:128], o2[:, 128:256])

    xs = [jnp.concatenate([r[0, :, l].reshape(CMP_STRIDE // 2 * KG, HEAD_DIM) for r in page_refs], axis=0)
          for l in range(CMP_STRIDE)]
    a, b = half_proj(xs)
    a_prev = jnp.concatenate([carry[...], a[:-KG]], axis=0)
    o_ref[0] = finish(a_prev, b).reshape(o_ref.shape[1:])
    carry[...] = a[-KG:]

    if has_tail:
        @pl.when(p == pl.num_programs(1) - 1)
        def _():
            nct = tail_ref.shape[1]
            xt = [tail_ref[0, :, l].reshape(nct * KG, HEAD_DIM) for l in range(CMP_STRIDE)]
            at, bt = half_proj(xt)
            ap = jnp.concatenate([a[-KG:], at[:-KG]], axis=0)
            res = finish(ap, bt).reshape(nct, KG, HEAD_DIM)
            ot_ref[0] = jnp.concatenate([res, jnp.zeros((ot_ref.shape[1] - nct, KG, HEAD_DIM), F32)], axis=0)


def _cmp_summaries(pool, page_table, tail, wc, w2c, peb, *, P):
    ns, npg = page_table.shape
    cpp = PAGE_SIZE // CMP_STRIDE
    assert npg % P == 0
    has_tail = tail is not None
    blk = (1, cpp, CMP_STRIDE, KG, HEAD_DIM)
    in_specs = [pl.BlockSpec(blk, functools.partial(lambda s, p, pt, u: (pt[s * npg + p * P + u], 0, 0, 0, 0), u=u))
                for u in range(P)]
    args = [pool] * P
    if has_tail:
        nct = tail.shape[1]
        in_specs.append(pl.BlockSpec((1, nct, CMP_STRIDE, KG, HEAD_DIM), lambda s, p, pt: (s, 0, 0, 0, 0)))
        args.append(tail)
    in_specs += [
        pl.BlockSpec(wc.shape, lambda s, p, pt: (0, 0, 0)),
        pl.BlockSpec(w2c.shape, lambda s, p, pt: (0, 0)),
        pl.BlockSpec(peb.shape, lambda s, p, pt: (0, 0)),
    ]
    args += [wc, w2c, peb]
    out_shape = [jax.ShapeDtypeStruct((ns, npg * cpp, KG, HEAD_DIM), F32)]
    out_specs = [pl.BlockSpec((1, P * cpp, KG, HEAD_DIM), lambda s, p, pt: (s, p, 0, 0))]
    if has_tail:
        out_shape.append(jax.ShapeDtypeStruct((ns, 8, KG, HEAD_DIM), F32))
        out_specs.append(pl.BlockSpec((1, 8, KG, HEAD_DIM), lambda s, p, pt: (s, 0, 0, 0)))
    return pl.pallas_call(
        functools.partial(_cmp_kernel, P=P, has_tail=has_tail),
        grid_spec=pltpu.PrefetchScalarGridSpec(
            num_scalar_prefetch=1, grid=(ns, npg // P), in_specs=in_specs, out_specs=out_specs,
            scratch_shapes=[pltpu.VMEM((KG, HEAD_DIM), F32)]),
        out_shape=out_shape,
        compiler_params=_cparams(("parallel", "arbitrary")),
    )(page_table.reshape(-1), *args)


def _pack_cmp_weights(cmp_w1, cmp_w2):
    half = CMP_STRIDE // 2
    def cols(l):
        return jnp.concatenate([cmp_w1[0, l], cmp_w1[0, CMP_STRIDE + l], cmp_w1[1, l], cmp_w1[1, CMP_STRIDE + l]], axis=1)
    wc = jnp.stack([jnp.concatenate([cols(lp), cols(lp + half)], axis=0) for lp in range(half)]).astype(BF16)
    w2c = jnp.concatenate([cmp_w2[0], cmp_w2[1]], axis=1).astype(BF16)
    return wc, w2c


def _overlap_matrix(n_slots, n_sb, width):
    i = jnp.arange(n_slots)[:, None] - 1
    j = jnp.arange(width)[None, :]
    ov = (i >= 0) & (j < n_sb) & (i * CMP_STRIDE < (j + 1) * SLC_BLOCK) & (i * CMP_STRIDE + CMP_BLOCK > j * SLC_BLOCK)
    return ov.astype(BF16)


def _masked_softmax(s, mask):
    s = jnp.where(mask, s, -jnp.inf)
    m = jnp.max(s, axis=-1, keepdims=True)
    m = jnp.where(m > -jnp.inf, m, 0.0)
    e = jnp.exp(s - m)
    return e / jnp.maximum(jnp.sum(e, axis=-1, keepdims=True), 1e-30)


def _dot_nt(a, b):
    return lax.dot_general(a, b, (((1,), (1,)), ((), ())), preferred_element_type=F32)


def _attn_prompt_kernel(q_ref, kc_ref, vc_ref, ks_ref, vs_ref, kw_ref, vw_ref, gate_ref, ov_ref, e_ref, ge_ref,
                        o_ref, ksb, vsb, kwb, vwb, *, tq, n_sb, ck, cw):
    i = pl.program_id(2)
    hpg = HEADS_PER_GROUP
    rows = hpg * tq

    @pl.when(i == 0)
    def _():
        ksb[...] = ks_ref[0].astype(BF16)
        vsb[...] = vs_ref[0].astype(BF16)
        kwb[...] = kw_ref[0].astype(BF16)
        vwb[...] = vw_ref[0].astype(BF16)

    q = jnp.concatenate([q_ref[:, h * HEAD_DIM:(h + 1) * HEAD_DIM] for h in range(hpg)], axis=0)
    t0 = i * tq
    t_row = t0 + lax.broadcasted_iota(jnp.int32, (rows, 1), 0) % tq
    t_tok = t0 + lax.broadcasted_iota(jnp.int32, (tq, 1), 0)

    n_slots = kc_ref.shape[1]
    s_c = _dot_nt(q, kc_ref[0].astype(BF16)) * ATT_SCALE
    slot = lax.broadcasted_iota(jnp.int32, (1, n_slots), 1)
    mask_c = (slot >= 1) & ((slot - 1) * CMP_STRIDE + CMP_BLOCK - 1 <= t_row)
    p_c = _masked_softmax(s_c, mask_c)
    o_c = jnp.dot(p_c.astype(BF16), vc_ref[0].astype(BF16), preferred_element_type=F32)
    p_grp = p_c[0:tq]
    for h in range(1, hpg):
        p_grp = p_grp + p_c[h * tq:(h + 1) * tq]

    score = _dot_exact01(p_grp, ov_ref[...])
    jl = lax.broadcasted_iota(jnp.int32, (1, LANES), 1)
    cur = t_tok // SLC_BLOCK
    valid = jl <= cur
    forced = (jl == 0) | (jl == cur) | (jl == cur - 1)
    score = jnp.where(forced, FORCE_SCORE, score)
    score = jnp.where(valid, score, -jnp.inf)
    rank = jnp.zeros((tq, LANES), F32)
    for j2 in range(n_sb):
        col = score[:, j2:j2 + 1]
        beats = (col > score) | ((col == score) & (jl > j2))
        rank = rank + jnp.where(beats, 1.0, 0.0)
    sel = jnp.where((rank < N_SELECT) & valid, 1.0, 0.0).astype(BF16)

    def flash(n_lo, n_hi, kb, vb, width, mask_fn):
        def body(c, carry):
            m, l, acc = carry
            start = pl.multiple_of(c * width, width)
            k = kb[pl.ds(start, width), :]
            v = vb[pl.ds(start, width), :]
            s = _dot_nt(q, k) * ATT_SCALE
            kpos = start + lax.broadcasted_iota(jnp.int32, (1, width), 1)
            mask = mask_fn(c, kpos)
            s = jnp.where(mask, s, NEG_BIG)
            m_new = jnp.maximum(m, jnp.max(s, axis=1, keepdims=True))
            alpha = jnp.exp(m - m_new)
            p = jnp.where(mask, jnp.exp(s - m_new), 0.0)
            l = alpha * l + jnp.sum(p, axis=1, keepdims=True)
            acc = alpha * acc + jnp.dot(p.astype(BF16), v, preferred_element_type=F32)
            return m_new, l, acc
        init = (jnp.full((rows, 1), NEG_BIG, F32), jnp.zeros((rows, 1), F32), jnp.zeros((rows, HEAD_DIM), F32))
        _, l, acc = lax.fori_loop(n_lo, n_hi, body, init)
        return acc / jnp.maximum(l, 1e-30)

    def sel_mask(c, kpos):
        selx = jnp.dot(sel, e_ref[c], preferred_element_type=F32)
        selx = jnp.concatenate([selx] * hpg, axis=0)
        return (selx > 0.5) & (kpos <= t_row)

    o_s = flash(0, (t0 + tq + ck - 1) // ck, ksb, vsb, ck, sel_mask)

    def win_mask(c, kpos):
        return (kpos <= t_row) & (kpos > t_row - WINDOW)

    o_w = flash(jnp.maximum(t0 - WINDOW + 1, 0) // cw, (t0 + tq + cw - 1) // cw, kwb, vwb, cw, win_mask)

    sig = jax.nn.sigmoid(gate_ref[...])
    g_c, g_s, g_w = (_dot_exact01(sig, ge_ref[k, 0]) for k in range(3))
    for h in range(hpg):
        hs = slice(h * HEAD_DIM, (h + 1) * HEAD_DIM)
        rs = slice(h * tq, (h + 1) * tq)
        o_ref[:, hs] = (g_c[:, hs] * o_c[rs] + g_s[:, hs] * o_s[rs] + g_w[:, hs] * o_w[rs]).astype(o_ref.dtype)


def _gate_expand(width_heads):
    c = jnp.arange(LANES)[None, :, None]
    k = jnp.arange(3)[:, None, None]
    hh = (jnp.arange(width_heads * HEAD_DIM) // HEAD_DIM)[None, None, :]
    return (c == k * N_HEADS + hh).astype(BF16)


def _attn_prompt(q, kvc_slots, kvs, kvw, small, *, nb, s_len, tq=128, ck=256, cw=128):
    nq = s_len // tq
    n_slots = kvc_slots.shape[1]
    n_sb = s_len // SLC_BLOCK
    assert n_sb <= LANES and n_slots <= LANES and s_len % ck == 0 and tq % cw == 0
    hd = HEAD_DIM
    gw = HEADS_PER_GROUP * hd
    ov = _overlap_matrix(n_slots, n_sb, LANES)
    key_blk = (jnp.arange(s_len) // SLC_BLOCK).reshape(s_len // ck, 1, ck)
    e = (jnp.arange(LANES)[None, :, None] == key_blk).astype(BF16)
    ge = _gate_expand(N_HEADS).reshape(3, LANES, KV_GROUPS, gw).transpose(0, 2, 1, 3)
    kspec = lambda kv: pl.BlockSpec((1, s_len, hd), lambda b, g, i: (b, 0, kv * KV_GROUPS + g))
    cspec = lambda kv: pl.BlockSpec((1, n_slots, hd), lambda b, g, i: (b, 0, kv * KV_GROUPS + g))
    return pl.pallas_call(
        functools.partial(_attn_prompt_kernel, tq=tq, n_sb=n_sb, ck=ck, cw=cw),
        grid=(nb, KV_GROUPS, nq),
        in_specs=[
            pl.BlockSpec((tq, gw), lambda b, g, i: (b * nq + i, g)),
            cspec(0), cspec(1), kspec(0), kspec(1), kspec(0), kspec(1),
            pl.BlockSpec((tq, LANES), lambda b, g, i: (b * nq + i, 0)),
            pl.BlockSpec(ov.shape, lambda b, g, i: (0, 0)),
            pl.BlockSpec(e.shape, lambda b, g, i: (0, 0, 0)),
            pl.BlockSpec((3, 1, LANES, gw), lambda b, g, i: (0, g, 0, 0)),
        ],
        out_specs=pl.BlockSpec((tq, gw), lambda b, g, i: (b * nq + i, g)),
        out_shape=jax.ShapeDtypeStruct((nb * s_len, N_HEADS * hd), F32),
        scratch_shapes=[pltpu.VMEM((s_len, hd), BF16)] * 4,
        compiler_params=_cparams(("parallel", "parallel", "arbitrary")),
    )(q, kvc_slots, kvc_slots, kvs, kvs, kvw, kvw, small, ov, e, ge)


T_PAD = 8


def _attn_sample_a_kernel(q_ref, kcm_ref, kct_ref, wc_ref, wn_ref, ov_ref, oc_ref, ow_ref, idx_ref,
                          *, past, t_new, n_blocks, n_sb, wb):
    hpg = HEADS_PER_GROUP
    rows = hpg * T_PAD
    t_pos = past + lax.broadcasted_iota(jnp.int32, (rows, 1), 0) % T_PAD
    tp8 = past + lax.broadcasted_iota(jnp.int32, (T_PAD, 1), 0)
    n_slots = kcm_ref.shape[1] + kct_ref.shape[1]
    slot = lax.broadcasted_iota(jnp.int32, (1, n_slots), 1)
    mask_c = (slot >= 1) & (slot <= n_blocks) & ((slot - 1) * CMP_STRIDE + CMP_BLOCK - 1 <= t_pos)
    width = ov_ref.shape[1]
    jl = lax.broadcasted_iota(jnp.int32, (1, width), 1)
    jf = jl.astype(F32)
    cur = tp8 // SLC_BLOCK
    valid = (jl <= cur) & (jl < n_sb)
    forced = (jl == 0) | (jl == cur) | (jl == cur - 1)
    lane = lax.broadcasted_iota(jnp.int32, (1, LANES), 1)
    wj = lax.broadcasted_iota(jnp.int32, (1, wb + T_PAD), 1)
    kp = past - wb + wj
    mask_w = (kp <= t_pos) & (kp > t_pos - WINDOW) & (kp >= 0) & (wj < wb + t_new)

    for g in range(KV_GROUPS):
        q = q_ref[0, g].astype(BF16)
        ksl = slice(g * HEAD_DIM, (g + 1) * HEAD_DIM)
        vsl = slice((KV_GROUPS + g) * HEAD_DIM, (KV_GROUPS + g + 1) * HEAD_DIM)
        kc = jnp.concatenate([kcm_ref[0, :, ksl], kct_ref[0, :, ksl]], axis=0).astype(BF16)
        vc = jnp.concatenate([kcm_ref[0, :, vsl], kct_ref[0, :, vsl]], axis=0).astype(BF16)
        p_c = _masked_softmax(_dot_nt(q, kc) * ATT_SCALE, mask_c)
        oc_ref[0, g] = jnp.dot(p_c.astype(BF16), vc, preferred_element_type=F32)
        p_grp = p_c[0:T_PAD]
        for h in range(1, hpg):
            p_grp = p_grp + p_c[h * T_PAD:(h + 1) * T_PAD]
        score = _dot_exact01(p_grp, ov_ref[...])
        score = jnp.where(forced, FORCE_SCORE, score)
        score = jnp.where(valid, score, -jnp.inf)
        idx = jnp.zeros((T_PAD, LANES), jnp.int32)
        for k in range(N_SELECT):
            mx = jnp.max(score, axis=1, keepdims=True)
            am = jnp.min(jnp.where(score == mx, jf, 1e9), axis=1, keepdims=True)
            pick = jnp.where(mx > -jnp.inf, am, -1.0).astype(jnp.int32)
            idx = jnp.where(lane == k, pick, idx)
            score = jnp.where(jf == am, -jnp.inf, score)
        idx_ref[0, g] = idx

        kw = jnp.concatenate([wc_ref[0, :, ksl], wn_ref[0, :, ksl]], axis=0).astype(BF16)
        vw = jnp.concatenate([wc_ref[0, :, vsl], wn_ref[0, :, vsl]], axis=0).astype(BF16)
        p_w = _masked_softmax(_dot_nt(q, kw) * ATT_SCALE, mask_w)
        ow_ref[0, g] = jnp.dot(p_w.astype(BF16), vw, preferred_element_type=F32)


def _attn_sample_a(q4, slots_main, slots_tail, win_cache, win_new, *, past, t_new, n_blocks, n_sb):
    ns = q4.shape[0]
    wb = win_cache.shape[1]
    n_slots = slots_main.shape[1] + slots_tail.shape[1]
    width = -(-n_sb // LANES) * LANES
    ov = _overlap_matrix(n_slots, n_sb, width)
    rows = HEADS_PER_GROUP * T_PAD
    full = lambda a: pl.BlockSpec((1,) + a.shape[1:], lambda s: (s,) + (0,) * (a.ndim - 1))
    o_sds = jax.ShapeDtypeStruct((ns, KV_GROUPS, rows, HEAD_DIM), F32)
    o_spec = pl.BlockSpec((1, KV_GROUPS, rows, HEAD_DIM), lambda s: (s, 0, 0, 0))
    return pl.pallas_call(
        functools.partial(_attn_sample_a_kernel, past=past, t_new=t_new, n_blocks=n_blocks, n_sb=n_sb, wb=wb),
        grid=(ns,),
        in_specs=[full(q4), full(slots_main), full(slots_tail), full(win_cache), full(win_new),
                  pl.BlockSpec(ov.shape, lambda s: (0, 0))],
        out_specs=[o_spec, o_spec, pl.BlockSpec((1, KV_GROUPS, T_PAD, LANES), lambda s: (s, 0, 0, 0))],
        out_shape=[o_sds, o_sds, jax.ShapeDtypeStruct((ns, KV_GROUPS, T_PAD, LANES), jnp.int32)],
        compiler_params=_cparams(("parallel",)),
    )(q4, slots_main, slots_tail, win_cache, win_new, ov)


def _attn_sample_b_kernel(idx_ref, pt_ref, q_ref, kt_ref, vt_ref, *refs, past, nb_past, t_new):
    n = N_SELECT
    kp_refs, vp_refs = refs[:n], refs[n:2 * n]
    o_ref, ksc, vsc = refs[2 * n:]
    s, t, g = pl.program_id(0), pl.program_id(1), pl.program_id(2)
    base = ((s * t_new + t) * KV_GROUPS + g) * n
    jl = lax.broadcasted_iota(jnp.int32, (1, n * SLC_BLOCK), 1)
    kslot = jl // SLC_BLOCK
    start = jnp.zeros((1, n * SLC_BLOCK), jnp.int32)
    ok = jnp.zeros((1, n * SLC_BLOCK), jnp.int32)
    for k in range(n):
        jj = idx_ref[base + k]
        from_pool = jj < nb_past
        rs = slice(k * SLC_BLOCK, (k + 1) * SLC_BLOCK)
        ksc[rs, :] = jnp.where(from_pool, kp_refs[k][0], kt_ref[0]).astype(BF16)
        vsc[rs, :] = jnp.where(from_pool, vp_refs[k][0], vt_ref[0]).astype(BF16)
        start = jnp.where(kslot == k, jj * SLC_BLOCK, start)
        ok = jnp.where(kslot == k, (jj >= 0).astype(jnp.int32), ok)
    kpos = start + jl % SLC_BLOCK
    mask = (ok > 0) & (kpos <= past + t)
    q = q_ref[0, 0, 0].astype(BF16)
    p = _masked_softmax(_dot_nt(q, ksc[...]) * ATT_SCALE, mask)
    o_ref[0, 0, 0] = jnp.dot(p.astype(BF16), vsc[...], preferred_element_type=F32)


def _attn_sample_b(idx_flat, pt_flat, q5, tail, pool, *, past, npg):
    ns, t_new = q5.shape[:2]
    n = N_SELECT
    bpp = PAGE_SIZE // SLC_BLOCK
    nb_past = past // SLC_BLOCK
    G = KV_GROUPS

    def pool_map(s, t, g, idx, pt, *, k, kv):
        jj = idx[((s * t_new + t) * G + g) * n + k]
        jp = jnp.clip(jj, 0, nb_past - 1)
        return (pt[s * npg + jp // bpp] * bpp + jp % bpp, 0, kv * G + g)

    blk = (1, SLC_BLOCK, HEAD_DIM)
    in_specs = [
        pl.BlockSpec((1, 1, 1, 8, HEAD_DIM), lambda s, t, g, idx, pt: (s, t, g, 0, 0)),
        pl.BlockSpec(blk, lambda s, t, g, idx, pt: (s, 0, g)),
        pl.BlockSpec(blk, lambda s, t, g, idx, pt: (s, 0, G + g)),
    ]
    in_specs += [pl.BlockSpec(blk, functools.partial(pool_map, k=k, kv=0)) for k in range(n)]
    in_specs += [pl.BlockSpec(blk, functools.partial(pool_map, k=k, kv=1)) for k in range(n)]
    return pl.pallas_call(
        functools.partial(_attn_sample_b_kernel, past=past, nb_past=nb_past, t_new=t_new),
        grid_spec=pltpu.PrefetchScalarGridSpec(
            num_scalar_prefetch=2, grid=(ns, t_new, G), in_specs=in_specs,
            out_specs=pl.BlockSpec((1, 1, 1, 8, HEAD_DIM), lambda s, t, g, idx, pt: (s, t, g, 0, 0)),
            scratch_shapes=[pltpu.VMEM((n * SLC_BLOCK, HEAD_DIM), BF16)] * 2),
        out_shape=jax.ShapeDtypeStruct((ns, t_new, G, 8, HEAD_DIM), F32),
        compiler_params=_cparams(("parallel", "parallel", "parallel")),
    )(idx_flat, pt_flat, q5, tail, tail, *([pool] * (2 * n)))


def _nsa_combine_kernel(oc_ref, os_ref, ow_ref, gate_ref, ge_ref, o_ref):
    sig = jax.nn.sigmoid(gate_ref[...])
    o_ref[...] = (_dot_exact01(sig, ge_ref[0]) * oc_ref[...] + _dot_exact01(sig, ge_ref[1]) * os_ref[...]
                  + _dot_exact01(sig, ge_ref[2]) * ow_ref[...])


def _nsa_combine(oc, os_, ow, small):
    ge = _gate_expand(N_HEADS)
    return pl.pallas_call(
        _nsa_combine_kernel,
        out_shape=jax.ShapeDtypeStruct(oc.shape, F32),
        compiler_params=pltpu.CompilerParams(vmem_limit_bytes=VMEM_LIMIT),
    )(oc, os_, ow, small, ge)


def _pack_in_proj(w_in, b_in):
    sizes = (N_HEADS * HEAD_DIM, KV_WIDTH, KV_WIDTH, KV_WIDTH, N_GATE, 2 * M_INNER, M_INNER, M_INNER, 2 * M_HEADS,
             2 * D_MODEL)
    offs = [0]
    for sz in sizes:
        offs.append(offs[-1] + sz)
    o_attg, o_mqk, o_mif, o_merge, end = offs[4], offs[5], offs[8], offs[9], offs[10]
    pad = LANES - N_GATE - 2 * M_HEADS

    def pack(a):
        main = jnp.concatenate([a[..., :o_attg], a[..., o_mqk:o_mif], a[..., o_merge:end]], axis=-1)
        small = jnp.concatenate([a[..., o_attg:o_mqk], a[..., o_mif:o_merge],
                                 jnp.zeros(a.shape[:-1] + (pad,), a.dtype)], axis=-1)
        return main, small

    w_main, w_small = pack(w_in)
    b_main, b_small = pack(b_in[None, :])
    return w_main.astype(BF16), b_main, w_small, b_small


def _project(x, pos, g_pre, packed, *, tm):
    w_main, b_main, w_small, b_small = packed
    tabs = _rope_tables(pos)
    common = dict(tm=tm, tn=512, rope=tabs)
    q = _norm_proj(x, g_pre, w_main, b_main, col0=OFF_Q, ncols=N_HEADS * HEAD_DIM, out_dtype=BF16, rope_mod=1, **common)
    kv = [_norm_proj(x, g_pre, w_main, b_main, col0=off, ncols=KV_WIDTH, out_dtype=F32, rope_mod=2, **common)
          for off in (OFF_KVC, OFF_KVS, OFF_KVW)]
    rest = _norm_proj(x, g_pre, w_main, b_main, col0=OFF_REST, ncols=REST_WIDTH, tm=tm, tn=512, out_dtype=F32)
    small = _norm_proj(x, g_pre, w_small, b_small, col0=0, ncols=LANES, tm=min(tm, 512), tn=LANES, out_dtype=F32,
                       precise=True)
    return q, kv, rest, small


def _gate_layouts(small, nb, t, t_pad):
    mif = small[:, N_GATE:N_GATE + 2 * M_HEADS].reshape(nb, t, 2, M_HEADS)
    if t_pad > t:
        fill = jnp.broadcast_to(jnp.array([NEG_BIG, 1e4], F32)[None, None, :, None], (nb, t_pad - t, 2, M_HEADS))
        mif = jnp.concatenate([mif, fill], axis=1)
    return jnp.transpose(mif, (0, 3, 1, 2)), jnp.transpose(mif, (0, 3, 2, 1))


def _pad_rows(a, n):
    return jnp.concatenate([a, jnp.zeros((a.shape[0], n - a.shape[1]) + a.shape[2:], a.dtype)], axis=1)


def _nsa_sample(q_s, kvc_s, kvs_s, kvw_s, small_s, cache_kv_cmp, cache_kv_slc, cache_win_kv, page_table,
                wc, w2c, peb, *, DB, T):
    npg = page_table.shape[1]
    past = npg * PAGE_SIZE
    n_phys = cache_kv_cmp.shape[0]
    cpp = PAGE_SIZE // CMP_STRIDE
    t_blk = -(-T // SLC_BLOCK) * SLC_BLOCK
    cmp_tail = _pad_rows(kvc_s.reshape(DB, T, KV_WIDTH), t_blk).reshape(DB, t_blk // CMP_STRIDE, CMP_STRIDE, KG, HEAD_DIM)
    slots_s, slots_t = _cmp_summaries(cache_kv_cmp.reshape(n_phys, cpp, CMP_STRIDE, KG, HEAD_DIM), page_table,
                                      cmp_tail, wc, w2c, peb, P=8)
    n_blocks = (past + t_blk) // CMP_STRIDE - 1
    nb_past = past // SLC_BLOCK
    n_sb = nb_past + t_blk // SLC_BLOCK
    q6 = q_s.astype(F32).reshape(DB, T, KV_GROUPS, HEADS_PER_GROUP, HEAD_DIM)
    q4 = _pad_rows(jnp.transpose(q6, (0, 2, 3, 1, 4)).reshape(DB * KV_GROUPS * HEADS_PER_GROUP, T, HEAD_DIM), T_PAD)
    q4 = q4.reshape(DB, KV_GROUPS, HEADS_PER_GROUP * T_PAD, HEAD_DIM)
    wb = cache_win_kv.shape[1]
    oc, ow, idx = _attn_sample_a(q4, slots_s.reshape(DB, npg * cpp, KV_WIDTH), slots_t.reshape(DB, 8, KV_WIDTH),
                                 cache_win_kv.reshape(DB, wb, KV_WIDTH), _pad_rows(kvw_s.reshape(DB, T, KV_WIDTH), T_PAD),
                                 past=past, t_new=T, n_blocks=n_blocks, n_sb=n_sb)
    idx_flat = jnp.transpose(idx[:, :, :T, :N_SELECT], (0, 2, 1, 3)).reshape(-1)
    q5 = jnp.concatenate([q6, jnp.zeros((DB, T, KV_GROUPS, 8 - HEADS_PER_GROUP, HEAD_DIM), F32)], axis=3)
    os5 = _attn_sample_b(idx_flat, page_table.reshape(-1), q5, _pad_rows(kvs_s.reshape(DB, T, KV_WIDTH), SLC_BLOCK),
                         cache_kv_slc.reshape(n_phys * (PAGE_SIZE // SLC_BLOCK), SLC_BLOCK, KV_WIDTH),
                         past=past, npg=npg)
    tok_major = lambda o: jnp.transpose(o.reshape(DB, KV_GROUPS, HEADS_PER_GROUP, T_PAD, HEAD_DIM)[:, :, :, :T],
                                        (0, 3, 1, 2, 4)).reshape(DB * T, N_HEADS * HEAD_DIM)
    return _nsa_combine(tok_major(oc), os5[:, :, :, :HEADS_PER_GROUP].reshape(DB * T, N_HEADS * HEAD_DIM),
                        tok_major(ow), small_s)


def kernel(x_prompt, x_sample, cache_kv_cmp, cache_kv_slc, cache_win_kv, state_mlstm_c, state_mlstm_n, state_mlstm_m, state_mlstm_conv, page_table, g_mix_pre, w_in, b_in, cmp_pe, cmp_w1, cmp_w2, conv_w, conv_b, mlstm_norm_w, w_out, g_mix_post, g_mlp_pre, w_up, w_down, g_mlp_post):
    B, S, D = x_prompt.shape
    DB, T, _ = x_sample.shape
    npg = page_table.shape[1]
    past = npg * PAGE_SIZE
    n_phys = cache_kv_cmp.shape[0]
    cpp = PAGE_SIZE // CMP_STRIDE
    kv5 = (2, KV_GROUPS, HEAD_DIM)

    packed = _pack_in_proj(w_in, b_in)
    w_out_b, w_up_b, w_down_b = w_out.astype(BF16), w_up.astype(BF16), w_down.astype(BF16)
    row = lambda v: v[None, :]
    peb = _pe_bias(cmp_pe, cmp_w1)
    wc, w2c = _pack_cmp_weights(cmp_w1, cmp_w2)

    def tail(rest, att, mh, x, tm, tm_mlp):
        x1 = _merge_out(rest, att, mh, x, w_out_b, row(g_mix_post), tm=tm)
        return _mlp(x1, row(g_mlp_pre), w_up_b, w_down_b, row(g_mlp_post), tm=tm_mlp, tf=512)

    xp = x_prompt.reshape(B * S, D)
    q_p, (kvc_p, kvs_p, kvw_p), rest_p, small_p = _project(xp, jnp.arange(S), row(g_mix_pre), packed, tm=1024)
    pt_p = jnp.arange(B * S // PAGE_SIZE, dtype=jnp.int32).reshape(B, S // PAGE_SIZE)
    slots_p, = _cmp_summaries(kvc_p.reshape(B * S // PAGE_SIZE, cpp, CMP_STRIDE, KG, HEAD_DIM), pt_p, None,
                              wc, w2c, peb, P=8)
    att_p = _attn_prompt(q_p, slots_p.reshape(B, S // CMP_STRIDE, KV_WIDTH), kvs_p.reshape(B, S, KV_WIDTH),
                         kvw_p.reshape(B, S, KV_WIDTH), small_p, nb=B, s_len=S)
    gcol, grow = _gate_layouts(small_p, B, S, S)
    mh_p, c_p, n_p, m_p = _mlstm(rest_p, gcol, grow, conv_w, row(conv_b), jnp.zeros((B, 8, 2 * M_INNER), F32),
                                 row(mlstm_norm_w), None, nb=B, t=S, L=256)
    y_p = tail(rest_p, att_p, mh_p, xp, 256, 512)
    wlen = min(WINDOW, S)
    conv_p = rest_p.reshape(B, S, REST_WIDTH)[:, S - (CONV_W - 1):, R_MQK:R_MQK + 2 * M_INNER]

    xs = x_sample.reshape(DB * T, D)
    pos_s = jnp.tile(past + jnp.arange(T), DB)
    q_s, (kvc_s, kvs_s, kvw_s), rest_s, small_s = _project(xs, pos_s, row(g_mix_pre), packed, tm=DB * T)
    att_s = _nsa_sample(q_s, kvc_s, kvs_s, kvw_s, small_s, cache_kv_cmp, cache_kv_slc, cache_win_kv, page_table,
                        wc, w2c, peb, DB=DB, T=T)


    LS = 16
    gcol_s, grow_s = _gate_layouts(small_s, DB, T, LS)
    rest_pad = _pad_rows(rest_s.reshape(DB, T, REST_WIDTH), LS).reshape(DB * LS, REST_WIDTH)
    conv0 = jnp.concatenate([jnp.zeros((DB, 8 - (CONV_W - 1), 2 * M_INNER), F32), state_mlstm_conv], axis=1)
    state = (state_mlstm_c, state_mlstm_n[:, :, None, :], state_mlstm_m[:, :, None, None])
    mh_pad, c_s, n_s, m_s = _mlstm(rest_pad, gcol_s, grow_s, conv_w, row(conv_b), conv0, row(mlstm_norm_w), state,
                                   nb=DB, t=LS, L=LS)
    mh_s = mh_pad.reshape(DB, LS, M_INNER)[:, :T].reshape(DB * T, M_INNER)
    y_s = tail(rest_s, att_s, mh_s, xs, DB * T, DB * T)
    win_s = jnp.concatenate([cache_win_kv, kvw_s.reshape((DB, T) + kv5).astype(cache_win_kv.dtype)], axis=1)[:, T:]
    mqk_s = rest_s.reshape(DB, T, REST_WIDTH)[:, :, R_MQK:R_MQK + 2 * M_INNER]
    conv_s = jnp.concatenate([state_mlstm_conv, mqk_s], axis=1)[:, T:]

    return (y_p.reshape(B, S, D), y_s.reshape(DB, T, D),
            kvc_p.reshape((B, S) + kv5), kvc_s.reshape((DB, T) + kv5),
            kvs_p.reshape((B, S) + kv5), kvs_s.reshape((DB, T) + kv5),
            kvw_p.reshape((B, S) + kv5)[:, S - wlen:], win_s,
            c_p, c_s, n_p[:, :, 0], n_s[:, :, 0], m_p[:, :, 0, 0], m_s[:, :, 0, 0], conv_p, conv_s)
```

```python
import functools

import jax
import jax.numpy as jnp
from jax import lax
from jax.experimental import pallas as pl
from jax.experimental.pallas import tpu as pltpu

F32 = jnp.float32
BF16 = jnp.bfloat16

D_MODEL = 2048
N_HEADS = 16
HEAD_DIM = 128
KV_GROUPS = 4
HEADS_PER_GROUP = N_HEADS // KV_GROUPS
ROPE_DIM = HEAD_DIM // 4
ROPE_HALF = ROPE_DIM // 2
ROPE_THETA = 500000.0
CMP_STRIDE = 16
CMP_BLOCK = 2 * CMP_STRIDE
SLC_BLOCK = 64
N_SELECT = 16
WINDOW = 512
ATT_SCALE = HEAD_DIM ** -0.5
FORCE_SCORE = 1e9
M_HEADS = 4
M_INNER = D_MODEL
M_HEAD_DIM = M_INNER // M_HEADS
CONV_W = 4
NEG_BIG = -1e30
D_FF = 4 * D_MODEL
EPS = 1e-6
PAGE_SIZE = 128
KV_WIDTH = 2 * KV_GROUPS * HEAD_DIM
KG = 2 * KV_GROUPS

LANES = 128
VMEM_LIMIT = 56 * 1024 * 1024

OFF_Q = 0
OFF_KVC = OFF_Q + N_HEADS * HEAD_DIM
OFF_KVS = OFF_KVC + KV_WIDTH
OFF_KVW = OFF_KVS + KV_WIDTH
ATT_WIDTH = OFF_KVW + KV_WIDTH
REST_WIDTH = 2 * M_INNER + M_INNER + M_INNER
N_GATE = 3 * N_HEADS
R_MQK, R_MV, R_MO = 0, 2 * M_INNER, 3 * M_INNER


def _cparams(sem):
    return pltpu.CompilerParams(dimension_semantics=sem, vmem_limit_bytes=VMEM_LIMIT)


def _split3(x):
    hi = x.astype(BF16)
    r1 = x - hi.astype(F32)
    mid = r1.astype(BF16)
    lo = (r1 - mid.astype(F32)).astype(BF16)
    return hi, mid, lo


def _dot_exact01(x, e):
    hi, mid, lo = _split3(x)
    d = lambda a: jnp.dot(a, e, preferred_element_type=F32)
    return d(hi) + d(mid) + d(lo)


def _kv_proj_kernel(x_ref, g_ref, w_ref, b_ref, c_ref, s1_ref, s2_ref, *refs, head_major):
    if head_major:
        o_ref, hm_ref, h_scr = refs
    else:
        o_ref, h_scr = refs
    j = pl.program_id(1)

    @pl.when(j == 0)
    def _():
        xf = x_ref[...]
        ms = jnp.mean(xf * xf, axis=-1, keepdims=True)
        h_scr[...] = (xf * lax.rsqrt(ms + EPS) * g_ref[...]).astype(h_scr.dtype)

    acc = jnp.dot(h_scr[...], w_ref[...], preferred_element_type=F32) + b_ref[...]

    def emit(rope):
        c, s1, s2 = c_ref[...], s1_ref[...], s2_ref[...]
        for g in range(KV_GROUPS):
            a = acc[:, g * LANES:(g + 1) * LANES]
            if rope:
                a = a * c + pltpu.roll(a, ROPE_HALF, 1) * s1 + pltpu.roll(a, LANES - ROPE_HALF, 1) * s2
            o_ref[:, 0, g, :] = a
            if head_major:
                hm_ref[0, g] = a.astype(BF16)

    pl.when(j == 0)(functools.partial(emit, True))
    pl.when(j != 0)(functools.partial(emit, False))


def _kv_proj(x, g, w, b, rope, *, col0, tm, seq_len=None):
    m, d = x.shape
    tn = KV_GROUPS * HEAD_DIM
    jb = col0 // tn
    period = rope[0].shape[0] // tm
    assert m % tm == 0 and col0 % tn == 0 and rope[0].shape[0] % tm == 0
    in_specs = [
        pl.BlockSpec((tm, d), lambda i, j: (i, 0)),
        pl.BlockSpec((1, d), lambda i, j: (0, 0)),
        pl.BlockSpec((d, tn), lambda i, j: (0, j + jb)),
        pl.BlockSpec((1, tn), lambda i, j: (0, j + jb)),
    ] + [pl.BlockSpec((tm, LANES), lambda i, j: (i % period, 0))] * 3
    out_shape = [jax.ShapeDtypeStruct((m, 2, KV_GROUPS, HEAD_DIM), F32)]
    out_specs = [pl.BlockSpec((tm, 1, KV_GROUPS, HEAD_DIM), lambda i, j: (i, j, 0, 0))]
    if seq_len is not None:
        per = seq_len // tm
        assert seq_len % tm == 0
        out_shape.append(jax.ShapeDtypeStruct((m // seq_len, KG, seq_len, HEAD_DIM), BF16))
        out_specs.append(pl.BlockSpec((1, KV_GROUPS, tm, HEAD_DIM), lambda i, j: (i // per, j, i % per, 0)))
    return pl.pallas_call(
        functools.partial(_kv_proj_kernel, head_major=seq_len is not None),
        grid=(m // tm, 2),
        in_specs=in_specs,
        out_specs=out_specs,
        out_shape=out_shape,
        scratch_shapes=[pltpu.VMEM((tm, d), BF16)],
        compiler_params=_cparams(("parallel", "arbitrary")),
    )(x, g, w, b, *rope)


def _proj_kernel(*refs, tn, rope_mod, precise):
    if rope_mod:
        x_ref, g_ref, w_ref, b_ref, c_ref, s1_ref, s2_ref, o_ref, h_scr = refs
    else:
        x_ref, g_ref, w_ref, b_ref, o_ref, h_scr = refs
    j = pl.program_id(1)

    @pl.when(j == 0)
    def _():
        xf = x_ref[...]
        ms = jnp.mean(xf * xf, axis=-1, keepdims=True)
        h_scr[...] = (xf * lax.rsqrt(ms + EPS) * g_ref[...]).astype(h_scr.dtype)

    if precise:
        acc = jnp.dot(h_scr[...], w_ref[...], preferred_element_type=F32,
                      precision=lax.Precision.HIGHEST)
    else:
        acc = jnp.dot(h_scr[...], w_ref[...], preferred_element_type=F32)
    acc = acc + b_ref[...]

    if not rope_mod:
        o_ref[...] = acc.astype(o_ref.dtype)
        return

    def roped():
        c, s1, s2 = c_ref[...], s1_ref[...], s2_ref[...]
        for hh in range(tn // LANES):
            a = acc[:, hh * LANES:(hh + 1) * LANES]
            r = a * c + pltpu.roll(a, ROPE_HALF, 1) * s1 + pltpu.roll(a, LANES - ROPE_HALF, 1) * s2
            o_ref[:, hh * LANES:(hh + 1) * LANES] = r.astype(o_ref.dtype)

    if rope_mod == 1:
        roped()
    else:
        pl.when(j % rope_mod == 0)(roped)

        @pl.when(j % rope_mod != 0)
        def _():
            o_ref[...] = acc.astype(o_ref.dtype)


def _norm_proj(x, g, w, b, *, col0, ncols, tm, tn, out_dtype, rope=None, rope_mod=0, precise=False):
    m, d = x.shape
    assert m % tm == 0 and ncols % tn == 0 and col0 % tn == 0
    jb = col0 // tn
    in_specs = [
        pl.BlockSpec((tm, d), lambda i, j: (i, 0)),
        pl.BlockSpec((1, d), lambda i, j: (0, 0)),
        pl.BlockSpec((d, tn), lambda i, j: (0, j + jb)),
        pl.BlockSpec((1, tn), lambda i, j: (0, j + jb)),
    ]
    args = [x, g, w, b]
    if rope_mod:
        period = rope[0].shape[0] // tm
        assert rope[0].shape[0] % tm == 0
        for tab in rope:
            in_specs.append(pl.BlockSpec((tm, LANES), lambda i, j: (i % period, 0)))
            args.append(tab)
    return pl.pallas_call(
        functools.partial(_proj_kernel, tn=tn, rope_mod=rope_mod, precise=precise),
        grid=(m // tm, ncols // tn),
        in_specs=in_specs,
        out_specs=pl.BlockSpec((tm, tn), lambda i, j: (i, j)),
        out_shape=jax.ShapeDtypeStruct((m, ncols), out_dtype),
        scratch_shapes=[pltpu.VMEM((tm, d), F32 if precise else BF16)],
        compiler_params=_cparams(("parallel", "arbitrary")),
    )(*args)


def _rope_tables(pos):
    inv = ROPE_THETA ** (-jnp.arange(ROPE_HALF, dtype=F32) / ROPE_HALF)
    ang = pos.astype(F32)[:, None] * inv
    cos, sin = jnp.cos(ang), jnp.sin(ang)
    p = pos.shape[0]
    rest = HEAD_DIM - ROPE_DIM
    c = jnp.concatenate([cos, cos, jnp.ones((p, rest), F32)], axis=1)
    s1 = jnp.concatenate([jnp.zeros((p, ROPE_HALF), F32), sin, jnp.zeros((p, rest), F32)], axis=1)
    s2 = jnp.concatenate([-sin, jnp.zeros((p, ROPE_HALF + rest), F32)], axis=1)
    return c, s1, s2


def _log_sigmoid(x):
    return jnp.minimum(x, 0.0) - jnp.log1p(jnp.exp(-jnp.abs(x)))


def _mlstm_kernel(*refs, L, zero_init):
    if zero_init:
        (q_ref, k_ref, v_ref, o_ref, gc_ref, gr_ref, cwq_ref, cwk_ref, cbq_ref, cbk_ref, cvq_ref, cvk_ref, nw_ref,
         h_ref, c_ref, n_ref, m_ref, xq_scr, xk_scr) = refs
    else:
        (q_ref, k_ref, v_ref, o_ref, gc_ref, gr_ref, cwq_ref, cwk_ref, cbq_ref, cbk_ref, cvq_ref, cvk_ref, nw_ref,
         c0_ref, n0_ref, m0_ref, h_ref, c_ref, n_ref, m_ref, xq_scr, xk_scr) = refs
    ci = pl.program_id(2)
    PAD = 8

    @pl.when(ci == 0)
    def _():
        if zero_init:
            c_ref[...] = jnp.zeros_like(c_ref)
            n_ref[...] = jnp.zeros_like(n_ref)
            m_ref[...] = jnp.zeros_like(m_ref)
        else:
            c_ref[...] = c0_ref[...]
            n_ref[...] = n0_ref[...]
            m_ref[...] = m0_ref[...]
        xq_scr[0:PAD, :] = cvq_ref[0]
        xk_scr[0:PAD, :] = cvk_ref[0]

    xq_scr[PAD:PAD + L, :] = q_ref[...]
    xk_scr[PAD:PAD + L, :] = k_ref[...]

    def conv_silu(x_scr, w_ref, b_ref):
        acc = b_ref[...] + jnp.zeros((L, M_HEAD_DIM), F32)
        for j in range(CONV_W):
            acc = acc + x_scr[pl.ds(PAD - (CONV_W - 1) + j, L), :] * w_ref[j:j + 1, :]
        return acc * jax.nn.sigmoid(acc)

    qs = conv_silu(xq_scr, cwq_ref, cbq_ref)
    ks = conv_silu(xk_scr, cwk_ref, cbk_ref) * (M_HEAD_DIM ** -0.5)
    xq_scr[0:PAD, :] = xq_scr[L:L + PAD, :]
    xk_scr[0:PAD, :] = xk_scr[L:L + PAD, :]

    gc = gc_ref[0, 0]
    gr = gr_ref[0, 0]
    i_col, f_col = gc[:, 0:1], _log_sigmoid(gc[:, 1:2])
    i_row, f_row = gr[0:1, :], _log_sigmoid(gr[1:2, :])
    t_idx = lax.broadcasted_iota(jnp.int32, (L, L), 0)
    s_idx = lax.broadcasted_iota(jnp.int32, (L, L), 1)
    tril = s_idx <= t_idx
    b_col = jnp.sum(jnp.where(tril, f_row, 0.0), axis=1, keepdims=True)
    b_row = jnp.sum(jnp.where(t_idx <= s_idx, f_col, 0.0), axis=0, keepdims=True)
    m_prev = m_ref[0, 0]
    dmat = jnp.where(tril, b_col - b_row + i_row, NEG_BIG)
    m_t = jnp.maximum(m_prev + b_col, jnp.max(dmat, axis=1, keepdims=True))
    w = jnp.exp(dmat - m_t)
    inter = jnp.exp(m_prev + b_col - m_t)

    c_prev = c_ref[0, 0]
    n_prev = n_ref[0, 0]
    qb, kb, vb = qs.astype(BF16), ks.astype(BF16), v_ref[...].astype(BF16)
    s = lax.dot_general(qb, kb, (((1,), (1,)), ((), ())), preferred_element_type=F32) * w
    num = inter * jnp.dot(qb, c_prev.astype(BF16), preferred_element_type=F32) \
        + jnp.dot(s.astype(BF16), vb, preferred_element_type=F32)
    den = inter * jnp.sum(qs * n_prev, axis=1, keepdims=True) + jnp.sum(s, axis=1, keepdims=True)
    h = num / jnp.maximum(jnp.abs(den), jnp.exp(-m_t))

    m_new = m_t[L - 1:L, :]
    b_last = b_col[L - 1:L, :]
    decay = jnp.exp(m_prev + b_last - m_new)
    g_col = jnp.exp(i_col + b_last - b_col - m_new)
    kg = ks * g_col
    c_ref[0, 0] = decay * c_prev + lax.dot_general(kg.astype(BF16), vb, (((0,), (0,)), ((), ())),
                                                   preferred_element_type=F32)
    n_ref[0, 0] = decay * n_prev + jnp.sum(kg, axis=0, keepdims=True)
    m_ref[0, 0] = m_new

    ho = jax.nn.sigmoid(o_ref[...]) * h
    mu = jnp.mean(ho, axis=1, keepdims=True)
    var = jnp.mean(jnp.square(ho - mu), axis=1, keepdims=True)
    h_ref[...] = ((ho - mu) * lax.rsqrt(var + EPS) * nw_ref[...]).astype(h_ref.dtype)


def _mlstm(rest, gates_col, gates_row, conv_w, conv_b, conv0, norm_w, state, *, nb, t, L):
    nc = t // L
    hb = M_HEAD_DIM
    zero_init = state is None
    row = lambda b, h, c: b * nc + c
    in_specs = [
        pl.BlockSpec((L, hb), lambda b, h, c: (row(b, h, c), R_MQK // hb + h)),
        pl.BlockSpec((L, hb), lambda b, h, c: (row(b, h, c), (R_MQK + M_INNER) // hb + h)),
        pl.BlockSpec((L, hb), lambda b, h, c: (row(b, h, c), R_MV // hb + h)),
        pl.BlockSpec((L, hb), lambda b, h, c: (row(b, h, c), R_MO // hb + h)),
        pl.BlockSpec((1, 1, L, 2), lambda b, h, c: (b, h, c, 0)),
        pl.BlockSpec((1, 1, 2, L), lambda b, h, c: (b, h, 0, c)),
        pl.BlockSpec((CONV_W, hb), lambda b, h, c: (0, h)),
        pl.BlockSpec((CONV_W, hb), lambda b, h, c: (0, M_HEADS + h)),
        pl.BlockSpec((1, hb), lambda b, h, c: (0, h)),
        pl.BlockSpec((1, hb), lambda b, h, c: (0, M_HEADS + h)),
        pl.BlockSpec((1, 8, hb), lambda b, h, c: (b, 0, h)),
        pl.BlockSpec((1, 8, hb), lambda b, h, c: (b, 0, M_HEADS + h)),
        pl.BlockSpec((1, hb), lambda b, h, c: (0, h)),
    ]
    args = [rest, rest, rest, rest, gates_col, gates_row, conv_w, conv_w, conv_b, conv_b, conv0, conv0, norm_w]
    st_specs = [
        pl.BlockSpec((1, 1, hb, hb), lambda b, h, c: (b, h, 0, 0)),
        pl.BlockSpec((1, 1, 1, hb), lambda b, h, c: (b, h, 0, 0)),
        pl.BlockSpec((1, 1, 1, 1), lambda b, h, c: (b, h, 0, 0)),
    ]
    if not zero_init:
        in_specs += st_specs
        args += list(state)
    out_shape = [
        jax.ShapeDtypeStruct((nb * t, M_INNER), F32),
        jax.ShapeDtypeStruct((nb, M_HEADS, hb, hb), F32),
        jax.ShapeDtypeStruct((nb, M_HEADS, 1, hb), F32),
        jax.ShapeDtypeStruct((nb, M_HEADS, 1, 1), F32),
    ]
    out_specs = [pl.BlockSpec((L, hb), lambda b, h, c: (row(b, h, c), h))] + st_specs
    return pl.pallas_call(
        functools.partial(_mlstm_kernel, L=L, zero_init=zero_init),
        grid=(nb, M_HEADS, nc),
        in_specs=in_specs,
        out_specs=out_specs,
        out_shape=out_shape,
        scratch_shapes=[pltpu.VMEM((L + 8, hb), F32), pltpu.VMEM((L + 8, hb), F32)],
        compiler_params=_cparams(("parallel", "parallel", "arbitrary")),
    )(*args)


def _rms(y, g):
    return y * lax.rsqrt(jnp.mean(y * y, axis=-1, keepdims=True) + EPS) * g


def _merge_out_kernel(ga_ref, gm_ref, att_ref, mh_ref, x_ref, w_ref, g_ref, o_ref):
    mixed = jax.nn.sigmoid(ga_ref[...]) * att_ref[...] + jax.nn.sigmoid(gm_ref[...]) * mh_ref[...]
    y = jnp.dot(mixed.astype(BF16), w_ref[...], preferred_element_type=F32)
    o_ref[...] = x_ref[...] + _rms(y, g_ref[...])


def _merge_out(merge, att, mh, x, w_out, g_post, *, tm):
    m, d = x.shape
    row = pl.BlockSpec((tm, d), lambda i: (i, 0))
    return pl.pallas_call(
        _merge_out_kernel,
        grid=(m // tm,),
        in_specs=[
            row,
            pl.BlockSpec((tm, d), lambda i: (i, 1)),
            row, row, row,
            pl.BlockSpec((d, d), lambda i: (0, 0)),
            pl.BlockSpec((1, d), lambda i: (0, 0)),
        ],
        out_specs=row,
        out_shape=jax.ShapeDtypeStruct((m, d), F32),
        compiler_params=_cparams(("parallel",)),
    )(merge, merge, att, mh, x, w_out, g_post)


def _mlp_kernel(x_ref, gpre_ref, wu_ref, wd_ref, gpost_ref, o_ref, h_scr, acc_scr):
    f = pl.program_id(1)

    @pl.when(f == 0)
    def _():
        h_scr[...] = _rms(x_ref[...], gpre_ref[...]).astype(BF16)
        acc_scr[...] = jnp.zeros_like(acc_scr)

    u = jnp.dot(h_scr[...], wu_ref[...], preferred_element_type=F32)
    u = jnp.square(jnp.maximum(u, 0.0))
    acc_scr[...] += jnp.dot(u.astype(BF16), wd_ref[...], preferred_element_type=F32)

    @pl.when(f == pl.num_programs(1) - 1)
    def _():
        o_ref[...] = x_ref[...] + _rms(acc_scr[...], gpost_ref[...])


def _mlp(x, g_pre, w_up, w_down, g_post, *, tm, tf):
    m, d = x.shape
    ff = w_up.shape[1]
    return pl.pallas_call(
        _mlp_kernel,
        grid=(m // tm, ff // tf),
        in_specs=[
            pl.BlockSpec((tm, d), lambda i, f: (i, 0)),
            pl.BlockSpec((1, d), lambda i, f: (0, 0)),
            pl.BlockSpec((d, tf), lambda i, f: (0, f)),
            pl.BlockSpec((tf, d), lambda i, f: (f, 0)),
            pl.BlockSpec((1, d), lambda i, f: (0, 0)),
        ],
        out_specs=pl.BlockSpec((tm, d), lambda i, f: (i, 0)),
        out_shape=jax.ShapeDtypeStruct((m, d), F32),
        scratch_shapes=[pltpu.VMEM((tm, d), BF16), pltpu.VMEM((tm, d), F32)],
        compiler_params=_cparams(("parallel", "arbitrary")),
    )(x, g_pre, w_up, w_down, g_post)


CMP_TAIL_SLOTS = 16


def _pe_bias_kernel(pet_ref, w1_ref, o_ref):
    rows = []
    for k in range(2):
        acc = jnp.zeros((1, HEAD_DIM), F32)
        for l in range(CMP_BLOCK):
            acc = acc + jnp.sum(pet_ref[k][:, l:l + 1] * w1_ref[k, l], axis=0, keepdims=True)
        rows += [acc] * KV_GROUPS
    o_ref[...] = jnp.concatenate(rows, axis=0)


def _pe_bias(cmp_pe, cmp_w1):
    pet = jnp.transpose(cmp_pe, (0, 2, 1))
    return pl.pallas_call(
        _pe_bias_kernel,
        out_shape=jax.ShapeDtypeStruct((KG, HEAD_DIM), F32),
        compiler_params=pltpu.CompilerParams(vmem_limit_bytes=VMEM_LIMIT),
    )(pet, cmp_w1)


def _cmp_kernel(pt_ref, *refs, P, has_tail):
    page_refs = refs[:P]
    if has_tail:
        tail_ref, wc_ref, w2_ref, peb_ref, o_ref, ot_ref, carry, res_scr = refs[P:]
    else:
        wc_ref, w2_ref, peb_ref, o_ref, carry, res_scr = refs[P:]
    p = pl.program_id(1)

    def store_head_major(res, out_ref, n_out):
        n = res.shape[0] // KG
        res_scr[0:n * KG, :] = res
        for kg in range(KG):
            rows = res_scr[pl.ds(kg, n, stride=KG), :]
            if n_out > n:
                rows = jnp.concatenate([rows, jnp.zeros((n_out - n, HEAD_DIM), F32)], axis=0)
            out_ref[0, kg] = rows.astype(out_ref.dtype)

    @pl.when(p == 0)
    def _():
        carry[...] = jnp.zeros_like(carry)

    def is_k(rows):
        return (lax.broadcasted_iota(jnp.int32, (rows, 1), 0) % KG) < KV_GROUPS

    def half_proj(xs):
        rows = xs[0].shape[0]
        acc = jnp.zeros((rows, 4 * HEAD_DIM), F32)
        for lp in range(CMP_STRIDE // 2):
            lhs = jnp.concatenate([xs[lp], xs[lp + CMP_STRIDE // 2]], axis=1).astype(BF16)
            acc = acc + jnp.dot(lhs, wc_ref[lp], preferred_element_type=F32)
        ik = is_k(rows)
        a = jnp.where(ik, acc[:, 0:128], acc[:, 256:384])
        b = jnp.where(ik, acc[:, 128:256], acc[:, 384:512])
        return a, b

    def finish(a_prev, b):
        rows = b.shape[0]
        peb = jnp.concatenate([peb_ref[...]] * (rows // KG), axis=0)
        hid = jax.nn.gelu(a_prev + b + peb)
        o2 = jnp.dot(hid.astype(BF16), w2_ref[...], preferred_element_type=F32)
        return jnp.where(is_k(rows), o2[:, 0:128], o2[:, 128:256])

    xs = [jnp.concatenate([r[0, :, l].reshape(CMP_STRIDE // 2 * KG, HEAD_DIM) for r in page_refs], axis=0)
          for l in range(CMP_STRIDE)]
    a, b = half_proj(xs)
    a_prev = jnp.concatenate([carry[...], a[:-KG]], axis=0)
    store_head_major(finish(a_prev, b), o_ref, o_ref.shape[2])
    carry[...] = a[-KG:]

    if has_tail:
        @pl.when(p == pl.num_programs(1) - 1)
        def _():
            nct = tail_ref.shape[1]
            xt = [tail_ref[0, :, l].reshape(nct * KG, HEAD_DIM) for l in range(CMP_STRIDE)]
            at, bt = half_proj(xt)
            ap = jnp.concatenate([a[-KG:], at[:-KG]], axis=0)
            store_head_major(finish(ap, bt), ot_ref, ot_ref.shape[2])


def _cmp_summaries(pool, page_table, tail, wc, w2c, peb, *, P):
    ns, npg = page_table.shape
    cpp = PAGE_SIZE // CMP_STRIDE
    assert npg % P == 0
    has_tail = tail is not None
    blk = (1, cpp, CMP_STRIDE, KG, HEAD_DIM)
    in_specs = [pl.BlockSpec(blk, functools.partial(lambda s, p, pt, u: (pt[s * npg + p * P + u], 0, 0, 0, 0), u=u))
                for u in range(P)]
    args = [pool] * P
    if has_tail:
        nct = tail.shape[1]
        in_specs.append(pl.BlockSpec((1, nct, CMP_STRIDE, KG, HEAD_DIM), lambda s, p, pt: (s, 0, 0, 0, 0)))
        args.append(tail)
    in_specs += [
        pl.BlockSpec(wc.shape, lambda s, p, pt: (0, 0, 0)),
        pl.BlockSpec(w2c.shape, lambda s, p, pt: (0, 0)),
        pl.BlockSpec(peb.shape, lambda s, p, pt: (0, 0)),
    ]
    args += [wc, w2c, peb]
    out_shape = [jax.ShapeDtypeStruct((ns, KG, npg * cpp, HEAD_DIM), BF16)]
    out_specs = [pl.BlockSpec((1, KG, P * cpp, HEAD_DIM), lambda s, p, pt: (s, 0, p, 0))]
    if has_tail:
        out_shape.append(jax.ShapeDtypeStruct((ns, KG, CMP_TAIL_SLOTS, HEAD_DIM), BF16))
        out_specs.append(pl.BlockSpec((1, KG, CMP_TAIL_SLOTS, HEAD_DIM), lambda s, p, pt: (s, 0, 0, 0)))
    return pl.pallas_call(
        functools.partial(_cmp_kernel, P=P, has_tail=has_tail),
        grid_spec=pltpu.PrefetchScalarGridSpec(
            num_scalar_prefetch=1, grid=(ns, npg // P), in_specs=in_specs, out_specs=out_specs,
            scratch_shapes=[pltpu.VMEM((KG, HEAD_DIM), F32), pltpu.VMEM((P * cpp * KG, HEAD_DIM), F32)]),
        out_shape=out_shape,
        compiler_params=_cparams(("parallel", "arbitrary")),
    )(page_table.reshape(-1), *args)


def _pack_cmp_weights(cmp_w1, cmp_w2):
    half = CMP_STRIDE // 2
    def cols(l):
        return jnp.concatenate([cmp_w1[0, l], cmp_w1[0, CMP_STRIDE + l], cmp_w1[1, l], cmp_w1[1, CMP_STRIDE + l]], axis=1)
    wc = jnp.stack([jnp.concatenate([cols(lp), cols(lp + half)], axis=0) for lp in range(half)]).astype(BF16)
    w2c = jnp.concatenate([cmp_w2[0], cmp_w2[1]], axis=1).astype(BF16)
    return wc, w2c


def _overlap_matrix(n_slots, n_sb, width):
    i = jnp.arange(n_slots)[:, None] - 1
    j = jnp.arange(width)[None, :]
    ov = (i >= 0) & (j < n_sb) & (i * CMP_STRIDE < (j + 1) * SLC_BLOCK) & (i * CMP_STRIDE + CMP_BLOCK > j * SLC_BLOCK)
    return ov.astype(BF16)


def _masked_softmax(s, mask):
    s = jnp.where(mask, s, -jnp.inf)
    m = jnp.max(s, axis=-1, keepdims=True)
    m = jnp.where(m > -jnp.inf, m, 0.0)
    e = jnp.exp(s - m)
    return e / jnp.maximum(jnp.sum(e, axis=-1, keepdims=True), 1e-30)


def _dot_nt(a, b):
    return lax.dot_general(a, b, (((1,), (1,)), ((), ())), preferred_element_type=F32)


def _attn_prompt_kernel(q_ref, kc_ref, vc_ref, ks_ref, vs_ref, kw_ref, vw_ref, gate_ref, ov_ref, e_ref, ge_ref,
                        o_ref, os_scr, *, tq, n_sb, sel_step):
    i = pl.program_id(2)
    hpg = HEADS_PER_GROUP
    rows = hpg * tq
    s_len = ks_ref.shape[2]

    q = jnp.concatenate([q_ref[:, h * HEAD_DIM:(h + 1) * HEAD_DIM] for h in range(hpg)], axis=0)
    t0 = i * tq
    t_row = t0 + lax.broadcasted_iota(jnp.int32, (rows, 1), 0) % tq
    t_tok = t0 + lax.broadcasted_iota(jnp.int32, (tq, 1), 0)

    n_slots = kc_ref.shape[2]
    s_c = _dot_nt(q, kc_ref[0, 0]) * ATT_SCALE
    slot = lax.broadcasted_iota(jnp.int32, (1, n_slots), 1)
    mask_c = (slot >= 1) & ((slot - 1) * CMP_STRIDE + CMP_BLOCK - 1 <= t_row)
    p_c = _masked_softmax(s_c, mask_c)
    o_c = jnp.dot(p_c.astype(BF16), vc_ref[0, 0], preferred_element_type=F32)
    p_grp = p_c[0:tq]
    for h in range(1, hpg):
        p_grp = p_grp + p_c[h * tq:(h + 1) * tq]

    score = _dot_exact01(p_grp, ov_ref[...])
    jl = lax.broadcasted_iota(jnp.int32, (1, LANES), 1)
    cur = t_tok // SLC_BLOCK
    valid = jl <= cur
    forced = (jl == 0) | (jl == cur) | (jl == cur - 1)
    score = jnp.where(forced, FORCE_SCORE, score)
    score = jnp.where(valid, score, -jnp.inf)
    rank = jnp.zeros((tq, LANES), F32)
    for j2 in range(n_sb):
        col = score[:, j2:j2 + 1]
        beats = (col > score) | ((col == score) & (jl > j2))
        rank = rank + jnp.where(beats, 1.0, 0.0)
    sel = jnp.where((rank < N_SELECT) & valid, 1.0, 0.0).astype(BF16)

    def attend(k, v, keep):
        bias = jnp.where(keep, 0.0, NEG_BIG)
        s = _dot_nt(q, k)
        es, ls = [], []
        for h in range(hpg):
            sh = s[h * tq:(h + 1) * tq] * ATT_SCALE + bias
            eh = jnp.exp(sh - jnp.max(sh, axis=1, keepdims=True))
            es.append(eh.astype(BF16))
            ls.append(jnp.sum(eh, axis=1, keepdims=True))
        o = jnp.dot(jnp.concatenate(es, axis=0), v, preferred_element_type=F32)
        return o / jnp.concatenate(ls, axis=0)

    for var in range(s_len // sel_step):
        width = (var + 1) * sel_step

        @pl.when((t0 + tq - 1) // sel_step == var)
        def _(width=width):
            selx = jnp.dot(sel, e_ref[:, 0:width], preferred_element_type=F32)
            kpos = lax.broadcasted_iota(jnp.int32, (1, width), 1)
            os_scr[...] = attend(ks_ref[0, 0, 0:width, :], vs_ref[0, 0, 0:width, :], (selx > 0.5) & (kpos <= t_tok))

    band = WINDOW + tq
    start = pl.multiple_of(jnp.maximum(t0 - WINDOW, 0), tq)
    kpos = start + lax.broadcasted_iota(jnp.int32, (1, band), 1)
    o_w = attend(kw_ref[0, 0, pl.ds(start, band), :], vw_ref[0, 0, pl.ds(start, band), :],
                 (kpos <= t_tok) & (kpos > t_tok - WINDOW))
    o_s = os_scr[...]

    sig = jax.nn.sigmoid(gate_ref[...])
    g_c, g_s, g_w = (_dot_exact01(sig, ge_ref[k, 0]) for k in range(3))
    for h in range(hpg):
        hs = slice(h * HEAD_DIM, (h + 1) * HEAD_DIM)
        rs = slice(h * tq, (h + 1) * tq)
        o_ref[:, hs] = (g_c[:, hs] * o_c[rs] + g_s[:, hs] * o_s[rs] + g_w[:, hs] * o_w[rs]).astype(o_ref.dtype)


def _gate_expand(width_heads):
    c = jnp.arange(LANES)[None, :, None]
    k = jnp.arange(3)[:, None, None]
    hh = (jnp.arange(width_heads * HEAD_DIM) // HEAD_DIM)[None, None, :]
    return (c == k * N_HEADS + hh).astype(BF16)


def _attn_prompt(q, kvc_slots, kvs, kvw, small, *, nb, s_len, tq=128, sel_step=512):
    nq = s_len // tq
    n_slots = kvc_slots.shape[2]
    n_sb = s_len // SLC_BLOCK
    assert n_sb <= LANES and n_slots <= LANES and s_len % sel_step == 0 and sel_step % tq == 0
    assert s_len >= WINDOW + tq and WINDOW % tq == 0
    hd = HEAD_DIM
    gw = HEADS_PER_GROUP * hd
    ov = _overlap_matrix(n_slots, n_sb, LANES)
    e = (jnp.arange(LANES)[:, None] == (jnp.arange(s_len) // SLC_BLOCK)[None, :]).astype(BF16)
    ge = _gate_expand(N_HEADS).reshape(3, LANES, KV_GROUPS, gw).transpose(0, 2, 1, 3)
    kspec = lambda kv: pl.BlockSpec((1, 1, s_len, hd), lambda b, g, i: (b, kv * KV_GROUPS + g, 0, 0))
    cspec = lambda kv: pl.BlockSpec((1, 1, n_slots, hd), lambda b, g, i: (b, kv * KV_GROUPS + g, 0, 0))
    return pl.pallas_call(
        functools.partial(_attn_prompt_kernel, tq=tq, n_sb=n_sb, sel_step=sel_step),
        grid=(nb, KV_GROUPS, nq),
        in_specs=[
            pl.BlockSpec((tq, gw), lambda b, g, i: (b * nq + i, g)),
            cspec(0), cspec(1), kspec(0), kspec(1), kspec(0), kspec(1),
            pl.BlockSpec((tq, LANES), lambda b, g, i: (b * nq + i, 0)),
            pl.BlockSpec(ov.shape, lambda b, g, i: (0, 0)),
            pl.BlockSpec(e.shape, lambda b, g, i: (0, 0)),
            pl.BlockSpec((3, 1, LANES, gw), lambda b, g, i: (0, g, 0, 0)),
        ],
        out_specs=pl.BlockSpec((tq, gw), lambda b, g, i: (b * nq + i, g)),
        out_shape=jax.ShapeDtypeStruct((nb * s_len, N_HEADS * hd), F32),
        scratch_shapes=[pltpu.VMEM((HEADS_PER_GROUP * tq, hd), F32)],
        compiler_params=_cparams(("parallel", "parallel", "parallel")),
    )(q, kvc_slots, kvc_slots, kvs, kvs, kvw, kvw, small, ov, e, ge)


T_PAD = 8


def _attn_sample_a_kernel(q_ref, kcm_ref, kct_ref, wc_ref, wn_ref, ov_ref, oc_ref, ow_ref, idx_ref,
                          *, past, t_new, n_blocks, n_sb, wb):
    hpg = HEADS_PER_GROUP
    rows = hpg * T_PAD
    t_pos = past + lax.broadcasted_iota(jnp.int32, (rows, 1), 0) % T_PAD
    tp8 = past + lax.broadcasted_iota(jnp.int32, (T_PAD, 1), 0)
    n_slots = kcm_ref.shape[2] + kct_ref.shape[2]
    slot = lax.broadcasted_iota(jnp.int32, (1, n_slots), 1)
    mask_c = (slot >= 1) & (slot <= n_blocks) & ((slot - 1) * CMP_STRIDE + CMP_BLOCK - 1 <= t_pos)
    width = ov_ref.shape[1]
    jl = lax.broadcasted_iota(jnp.int32, (1, width), 1)
    jf = jl.astype(F32)
    cur = tp8 // SLC_BLOCK
    valid = (jl <= cur) & (jl < n_sb)
    forced = (jl == 0) | (jl == cur) | (jl == cur - 1)
    lane = lax.broadcasted_iota(jnp.int32, (1, LANES), 1)
    wj = lax.broadcasted_iota(jnp.int32, (1, wb + T_PAD), 1)
    kp = past - wb + wj
    mask_w = (kp <= t_pos) & (kp > t_pos - WINDOW) & (kp >= 0) & (wj < wb + t_new)

    for g in range(KV_GROUPS):
        q = q_ref[0, g].astype(BF16)
        kc = jnp.concatenate([kcm_ref[0, g], kct_ref[0, g]], axis=0)
        vc = jnp.concatenate([kcm_ref[0, KV_GROUPS + g], kct_ref[0, KV_GROUPS + g]], axis=0)
        p_c = _masked_softmax(_dot_nt(q, kc) * ATT_SCALE, mask_c)
        oc_ref[0, g] = jnp.dot(p_c.astype(BF16), vc, preferred_element_type=F32)
        p_grp = p_c[0:T_PAD]
        for h in range(1, hpg):
            p_grp = p_grp + p_c[h * T_PAD:(h + 1) * T_PAD]
        score = _dot_exact01(p_grp, ov_ref[...])
        score = jnp.where(forced, FORCE_SCORE, score)
        score = jnp.where(valid, score, -jnp.inf)
        idx = jnp.zeros((T_PAD, LANES), jnp.int32)
        for k in range(N_SELECT):
            mx = jnp.max(score, axis=1, keepdims=True)
            am = jnp.min(jnp.where(score == mx, jf, 1e9), axis=1, keepdims=True)
            pick = jnp.where(mx > -jnp.inf, am, -1.0).astype(jnp.int32)
            idx = jnp.where(lane == k, pick, idx)
            score = jnp.where(jf == am, -jnp.inf, score)
        idx_ref[0, g] = idx

        kw = jnp.concatenate([wc_ref[0, :, 0, g, :], wn_ref[0, :, 0, g, :]], axis=0).astype(BF16)
        vw = jnp.concatenate([wc_ref[0, :, 1, g, :], wn_ref[0, :, 1, g, :]], axis=0).astype(BF16)
        p_w = _masked_softmax(_dot_nt(q, kw) * ATT_SCALE, mask_w)
        ow_ref[0, g] = jnp.dot(p_w.astype(BF16), vw, preferred_element_type=F32)


def _attn_sample_a(q4, slots_main, slots_tail, win_cache, win_new, *, past, t_new, n_blocks, n_sb):
    ns = q4.shape[0]
    wb = win_cache.shape[1]
    n_slots = slots_main.shape[2] + slots_tail.shape[2]
    width = -(-n_sb // LANES) * LANES
    ov = _overlap_matrix(n_slots, n_sb, width)
    rows = HEADS_PER_GROUP * T_PAD
    full = lambda a: pl.BlockSpec((1,) + a.shape[1:], lambda s: (s,) + (0,) * (a.ndim - 1))
    o_sds = jax.ShapeDtypeStruct((ns, KV_GROUPS, rows, HEAD_DIM), F32)
    o_spec = pl.BlockSpec((1, KV_GROUPS, rows, HEAD_DIM), lambda s: (s, 0, 0, 0))
    return pl.pallas_call(
        functools.partial(_attn_sample_a_kernel, past=past, t_new=t_new, n_blocks=n_blocks, n_sb=n_sb, wb=wb),
        grid=(ns,),
        in_specs=[full(q4), full(slots_main), full(slots_tail), full(win_cache), full(win_new),
                  pl.BlockSpec(ov.shape, lambda s: (0, 0))],
        out_specs=[o_spec, o_spec, pl.BlockSpec((1, KV_GROUPS, T_PAD, LANES), lambda s: (s, 0, 0, 0))],
        out_shape=[o_sds, o_sds, jax.ShapeDtypeStruct((ns, KV_GROUPS, T_PAD, LANES), jnp.int32)],
        compiler_params=_cparams(("parallel",)),
    )(q4, slots_main, slots_tail, win_cache, win_new, ov)


def _attn_sample_b_kernel(idx_ref, pt_ref, q_ref, pool_ref, tail_ref, o_ref, buf, sem, *, past, nb_past, t_new, npg):
    n = N_SELECT
    G = KV_GROUPS
    bpp = PAGE_SIZE // SLC_BLOCK
    nblk = t_new * n
    s, g = pl.program_id(0), pl.program_id(1)
    step = s * G + g
    nsteps = pl.num_programs(0) * G
    slot = step % 2

    def start_copies(step_, slot_):
        s_, g_ = step_ // G, step_ % G

        def body(j, carry):
            jj = idx_ref[step_ * nblk + j]
            jp = jnp.clip(jj, 0, nb_past - 1)
            page = pt_ref[s_ * npg + jp // bpp]
            r0 = pl.multiple_of((jp % bpp) * SLC_BLOCK, SLC_BLOCK)
            for kv in range(2):
                @pl.when(jj < nb_past)
                def _(kv=kv):
                    pltpu.make_async_copy(pool_ref.at[page, pl.ds(r0, SLC_BLOCK), kv, g_], buf.at[slot_, kv, j],
                                          sem.at[slot_]).start()

                @pl.when(jj >= nb_past)
                def _(kv=kv):
                    pltpu.make_async_copy(tail_ref.at[s_, :, kv, g_], buf.at[slot_, kv, j], sem.at[slot_]).start()
            return carry

        lax.fori_loop(0, nblk, body, 0)

    def wait_copies(slot_):
        def body(j, carry):
            for kv in range(2):
                pltpu.make_async_copy(tail_ref.at[0, :, kv, 0], buf.at[slot_, kv, j], sem.at[slot_]).wait()
            return carry

        lax.fori_loop(0, nblk, body, 0)

    @pl.when(step == 0)
    def _():
        start_copies(step, slot)

    @pl.when(step + 1 < nsteps)
    def _():
        start_copies(step + 1, 1 - slot)

    wait_copies(slot)

    q = q_ref[0, 0].astype(BF16)
    rows = q.shape[0]
    t8 = lax.broadcasted_iota(jnp.int32, (rows, 1), 0) % T_PAD
    jl = lax.broadcasted_iota(jnp.int32, (1, n * SLC_BLOCK), 1)
    kslot = jl // SLC_BLOCK
    out = jnp.zeros((rows, HEAD_DIM), F32)
    for t in range(t_new):
        start = jnp.zeros((1, n * SLC_BLOCK), jnp.int32)
        ok = jnp.zeros((1, n * SLC_BLOCK), jnp.int32)
        for k in range(n):
            jj = idx_ref[step * nblk + t * n + k]
            start = jnp.where(kslot == k, jj * SLC_BLOCK, start)
            ok = jnp.where(kslot == k, (jj >= 0).astype(jnp.int32), ok)
        kpos = start + jl % SLC_BLOCK
        mask = (ok > 0) & (kpos <= past + t)
        kk = buf[slot, 0, t * n:(t + 1) * n].reshape(n * SLC_BLOCK, HEAD_DIM).astype(BF16)
        vv = buf[slot, 1, t * n:(t + 1) * n].reshape(n * SLC_BLOCK, HEAD_DIM).astype(BF16)
        p = _masked_softmax(_dot_nt(q, kk) * ATT_SCALE, mask)
        o_t = jnp.dot(p.astype(BF16), vv, preferred_element_type=F32)
        out = jnp.where(t8 == t, o_t, out)
    o_ref[0, 0] = out


def _attn_sample_b(idx_flat, pt_flat, q4, tail, pool, *, past, npg, t_new):
    ns = q4.shape[0]
    G = KV_GROUPS
    rows = q4.shape[2]
    assert tail.shape[1] == SLC_BLOCK
    qspec = pl.BlockSpec((1, 1, rows, HEAD_DIM), lambda s, g, idx, pt: (s, g, 0, 0))
    return pl.pallas_call(
        functools.partial(_attn_sample_b_kernel, past=past, nb_past=past // SLC_BLOCK, t_new=t_new, npg=npg),
        grid_spec=pltpu.PrefetchScalarGridSpec(
            num_scalar_prefetch=2, grid=(ns, G),
            in_specs=[qspec, pl.BlockSpec(memory_space=pl.ANY), pl.BlockSpec(memory_space=pl.ANY)],
            out_specs=qspec,
            scratch_shapes=[pltpu.VMEM((2, 2, t_new * N_SELECT, SLC_BLOCK, HEAD_DIM), F32),
                            pltpu.SemaphoreType.DMA((2,))]),
        out_shape=jax.ShapeDtypeStruct((ns, G, rows, HEAD_DIM), F32),
        compiler_params=_cparams(("arbitrary", "arbitrary")),
    )(idx_flat, pt_flat, q4, pool, tail)


def _nsa_combine_kernel(oc_ref, os_ref, ow_ref, gate_ref, ge_ref, o_ref):
    sig = jax.nn.sigmoid(gate_ref[...])
    o_ref[...] = (_dot_exact01(sig, ge_ref[0]) * oc_ref[...] + _dot_exact01(sig, ge_ref[1]) * os_ref[...]
                  + _dot_exact01(sig, ge_ref[2]) * ow_ref[...])


def _nsa_combine(oc, os_, ow, small):
    ge = _gate_expand(N_HEADS)
    return pl.pallas_call(
        _nsa_combine_kernel,
        out_shape=jax.ShapeDtypeStruct(oc.shape, F32),
        compiler_params=pltpu.CompilerParams(vmem_limit_bytes=VMEM_LIMIT),
    )(oc, os_, ow, small, ge)


def _pack_in_proj(w_in, b_in):
    sizes = (N_HEADS * HEAD_DIM, KV_WIDTH, KV_WIDTH, KV_WIDTH, N_GATE, 2 * M_INNER, M_INNER, M_INNER, 2 * M_HEADS,
             2 * D_MODEL)
    offs = [0]
    for sz in sizes:
        offs.append(offs[-1] + sz)
    o_attg, o_mqk, o_mif, o_merge, end = offs[4], offs[5], offs[8], offs[9], offs[10]
    pad = LANES - N_GATE - 2 * M_HEADS

    def pack(a, dt):
        slabs = [a[..., :o_attg].astype(dt), a[..., o_mqk:o_mif].astype(dt), a[..., o_merge:end].astype(dt)]
        small = jnp.concatenate([a[..., o_attg:o_mqk], a[..., o_mif:o_merge],
                                 jnp.zeros(a.shape[:-1] + (pad,), a.dtype)], axis=-1)
        return slabs, small

    w_slabs, w_small = pack(w_in, BF16)
    b_slabs, b_small = pack(b_in[None, :], F32)
    return w_slabs, b_slabs, w_small, b_small


def _project(x, pos, g_pre, packed, *, tm, seq_len=None):
    (w_att, w_rest, w_merge), (b_att, b_rest, b_merge), w_small, b_small = packed
    tabs = _rope_tables(pos)
    q = _norm_proj(x, g_pre, w_att, b_att, col0=OFF_Q, ncols=N_HEADS * HEAD_DIM, out_dtype=BF16, rope_mod=1,
                   tm=tm, tn=512, rope=tabs)
    kv = [_kv_proj(x, g_pre, w_att, b_att, tabs, col0=off, tm=tm, seq_len=None if off == OFF_KVC else seq_len)
          for off in (OFF_KVC, OFF_KVS, OFF_KVW)]
    rest = _norm_proj(x, g_pre, w_rest, b_rest, col0=0, ncols=REST_WIDTH, tm=tm, tn=512, out_dtype=F32)
    merge = _norm_proj(x, g_pre, w_merge, b_merge, col0=0, ncols=2 * D_MODEL, tm=tm, tn=512, out_dtype=F32)
    small = _norm_proj(x, g_pre, w_small, b_small, col0=0, ncols=LANES, tm=min(tm, 512), tn=LANES, out_dtype=F32,
                       precise=True)
    return q, kv, rest, merge, small


def _gate_layouts(small, nb, t, t_pad):
    mif = small[:, N_GATE:N_GATE + 2 * M_HEADS].reshape(nb, t, 2, M_HEADS)
    if t_pad > t:
        fill = jnp.broadcast_to(jnp.array([NEG_BIG, 1e4], F32)[None, None, :, None], (nb, t_pad - t, 2, M_HEADS))
        mif = jnp.concatenate([mif, fill], axis=1)
    return jnp.transpose(mif, (0, 3, 1, 2)), jnp.transpose(mif, (0, 3, 2, 1))


def _pad_rows(a, n):
    return jnp.concatenate([a, jnp.zeros((a.shape[0], n - a.shape[1]) + a.shape[2:], a.dtype)], axis=1)


def _nsa_sample(q_s, kvc_s, kvs_s, kvw_s, small_s, cache_kv_cmp, cache_kv_slc, cache_win_kv, page_table,
                wc, w2c, peb, *, DB, T):
    npg = page_table.shape[1]
    past = npg * PAGE_SIZE
    n_phys = cache_kv_cmp.shape[0]
    cpp = PAGE_SIZE // CMP_STRIDE
    t_blk = -(-T // SLC_BLOCK) * SLC_BLOCK
    assert t_blk == SLC_BLOCK
    new_rows = lambda a, n: _pad_rows(a.reshape(DB, T, 2, KV_GROUPS, HEAD_DIM), n)
    cmp_tail = new_rows(kvc_s, t_blk).reshape(DB, t_blk // CMP_STRIDE, CMP_STRIDE, KG, HEAD_DIM)
    slots_s, slots_t = _cmp_summaries(cache_kv_cmp.reshape(n_phys, cpp, CMP_STRIDE, KG, HEAD_DIM), page_table,
                                      cmp_tail, wc, w2c, peb, P=8)
    n_blocks = (past + t_blk) // CMP_STRIDE - 1
    nb_past = past // SLC_BLOCK
    n_sb = nb_past + t_blk // SLC_BLOCK
    q6 = q_s.astype(F32).reshape(DB, T, KV_GROUPS, HEADS_PER_GROUP, HEAD_DIM)
    q4 = _pad_rows(jnp.transpose(q6, (0, 2, 3, 1, 4)).reshape(DB * KV_GROUPS * HEADS_PER_GROUP, T, HEAD_DIM), T_PAD)
    q4 = q4.reshape(DB, KV_GROUPS, HEADS_PER_GROUP * T_PAD, HEAD_DIM)
    oc, ow, idx = _attn_sample_a(q4, slots_s, slots_t, cache_win_kv, new_rows(kvw_s, T_PAD),
                                 past=past, t_new=T, n_blocks=n_blocks, n_sb=n_sb)
    idx_flat = idx[:, :, :T, :N_SELECT].reshape(-1)
    os4 = _attn_sample_b(idx_flat, page_table.reshape(-1), q4, new_rows(kvs_s, SLC_BLOCK), cache_kv_slc,
                         past=past, npg=npg, t_new=T)
    tok_major = lambda o: jnp.transpose(o.reshape(DB, KV_GROUPS, HEADS_PER_GROUP, T_PAD, HEAD_DIM)[:, :, :, :T],
                                        (0, 3, 1, 2, 4)).reshape(DB * T, N_HEADS * HEAD_DIM)
    return _nsa_combine(tok_major(oc), tok_major(os4), tok_major(ow), small_s)


def kernel(x_prompt, x_sample, cache_kv_cmp, cache_kv_slc, cache_win_kv, state_mlstm_c, state_mlstm_n, state_mlstm_m, state_mlstm_conv, page_table, g_mix_pre, w_in, b_in, cmp_pe, cmp_w1, cmp_w2, conv_w, conv_b, mlstm_norm_w, w_out, g_mix_post, g_mlp_pre, w_up, w_down, g_mlp_post):
    B, S, D = x_prompt.shape
    DB, T, _ = x_sample.shape
    npg = page_table.shape[1]
    past = npg * PAGE_SIZE
    n_phys = cache_kv_cmp.shape[0]
    cpp = PAGE_SIZE // CMP_STRIDE
    kv5 = (2, KV_GROUPS, HEAD_DIM)

    packed = _pack_in_proj(w_in, b_in)
    w_out_b, w_up_b, w_down_b = w_out.astype(BF16), w_up.astype(BF16), w_down.astype(BF16)
    row = lambda v: v[None, :]
    peb = _pe_bias(cmp_pe, cmp_w1)
    wc, w2c = _pack_cmp_weights(cmp_w1, cmp_w2)

    def tail(merge, att, mh, x, tm, tm_mlp):
        x1 = _merge_out(merge, att, mh, x, w_out_b, row(g_mix_post), tm=tm)
        return _mlp(x1, row(g_mlp_pre), w_up_b, w_down_b, row(g_mlp_post), tm=tm_mlp, tf=512)

    xp = x_prompt.reshape(B * S, D)
    q_p, kv_p, rest_p, merge_p, small_p = _project(xp, jnp.arange(S), row(g_mix_pre), packed, tm=1024, seq_len=S)
    (kvc_p,), (kvs_p, kvs_hm), (kvw_p, kvw_hm) = kv_p
    pt_p = jnp.arange(B * S // PAGE_SIZE, dtype=jnp.int32).reshape(B, S // PAGE_SIZE)
    slots_p, = _cmp_summaries(kvc_p.reshape(B * S // PAGE_SIZE, cpp, CMP_STRIDE, KG, HEAD_DIM), pt_p, None,
                              wc, w2c, peb, P=8)
    att_p = _attn_prompt(q_p, slots_p, kvs_hm, kvw_hm, small_p, nb=B, s_len=S)
    gcol, grow = _gate_layouts(small_p, B, S, S)
    mh_p, c_p, n_p, m_p = _mlstm(rest_p, gcol, grow, conv_w, row(conv_b), jnp.zeros((B, 8, 2 * M_INNER), F32),
                                 row(mlstm_norm_w), None, nb=B, t=S, L=256)
    y_p = tail(merge_p, att_p, mh_p, xp, 256, 512)
    wlen = min(WINDOW, S)
    conv_p = rest_p.reshape(B, S, REST_WIDTH)[:, S - (CONV_W - 1):, R_MQK:R_MQK + 2 * M_INNER]

    xs = x_sample.reshape(DB * T, D)
    pos_s = jnp.tile(past + jnp.arange(T), DB)
    q_s, ((kvc_s,), (kvs_s,), (kvw_s,)), rest_s, merge_s, small_s = _project(xs, pos_s, row(g_mix_pre), packed, tm=DB * T)
    att_s = _nsa_sample(q_s, kvc_s, kvs_s, kvw_s, small_s, cache_kv_cmp, cache_kv_slc, cache_win_kv, page_table,
                        wc, w2c, peb, DB=DB, T=T)

    LS = 16
    gcol_s, grow_s = _gate_layouts(small_s, DB, T, LS)
    rest_pad = _pad_rows(rest_s.reshape(DB, T, REST_WIDTH), LS).reshape(DB * LS, REST_WIDTH)
    conv0 = jnp.concatenate([jnp.zeros((DB, 8 - (CONV_W - 1), 2 * M_INNER), F32), state_mlstm_conv], axis=1)
    state = (state_mlstm_c, state_mlstm_n[:, :, None, :], state_mlstm_m[:, :, None, None])
    mh_pad, c_s, n_s, m_s = _mlstm(rest_pad, gcol_s, grow_s, conv_w, row(conv_b), conv0, row(mlstm_norm_w), state,
                                   nb=DB, t=LS, L=LS)
    mh_s = mh_pad.reshape(DB, LS, M_INNER)[:, :T].reshape(DB * T, M_INNER)
    y_s = tail(merge_s, att_s, mh_s, xs, DB * T, DB * T)
    win_s = jnp.concatenate([cache_win_kv[:, T:], kvw_s.reshape((DB, T) + kv5).astype(cache_win_kv.dtype)], axis=1)
    mqk_s = rest_s.reshape(DB, T, REST_WIDTH)[:, :, R_MQK:R_MQK + 2 * M_INNER]
    conv_s = jnp.concatenate([state_mlstm_conv, mqk_s], axis=1)[:, T:]

    return (y_p.reshape(B, S, D), y_s.reshape(DB, T, D),
            kvc_p.reshape((B, S) + kv5), kvc_s.reshape((DB, T) + kv5),
            kvs_p.reshape((B, S) + kv5), kvs_s.reshape((DB, T) + kv5),
            kvw_p.reshape((B, S) + kv5)[:, S - wlen:], win_s,
            c_p, c_s, n_p[:, :, 0], n_s[:, :, 0], m_p[:, :, 0, 0], m_s[:, :, 0, 0], conv_p, conv_s)
```

```python
import functools

import jax
import jax.numpy as jnp
from jax import lax
from jax.experimental import pallas as pl
from jax.experimental.pallas import tpu as pltpu

F32 = jnp.float32
BF16 = jnp.bfloat16

D_MODEL = 2048
N_HEADS = 16
HEAD_DIM = 128
KV_GROUPS = 4
HEADS_PER_GROUP = N_HEADS // KV_GROUPS
ROPE_DIM = HEAD_DIM // 4
ROPE_HALF = ROPE_DIM // 2
ROPE_THETA = 500000.0
CMP_STRIDE = 16
CMP_BLOCK = 2 * CMP_STRIDE
SLC_BLOCK = 64
N_SELECT = 16
WINDOW = 512
ATT_SCALE = HEAD_DIM ** -0.5
FORCE_SCORE = 1e9
M_HEADS = 4
M_INNER = D_MODEL
M_HEAD_DIM = M_INNER // M_HEADS
CONV_W = 4
NEG_BIG = -1e30
D_FF = 4 * D_MODEL
EPS = 1e-6
PAGE_SIZE = 128
KV_WIDTH = 2 * KV_GROUPS * HEAD_DIM
KG = 2 * KV_GROUPS

LANES = 128
VMEM_LIMIT = 56 * 1024 * 1024

OFF_Q = 0
OFF_KVC = OFF_Q + N_HEADS * HEAD_DIM
OFF_KVS = OFF_KVC + KV_WIDTH
OFF_KVW = OFF_KVS + KV_WIDTH
ATT_WIDTH = OFF_KVW + KV_WIDTH
REST_WIDTH = 2 * M_INNER + M_INNER + M_INNER
N_GATE = 3 * N_HEADS
R_MQK, R_MV, R_MO = 0, 2 * M_INNER, 3 * M_INNER


def _cparams(sem):
    return pltpu.CompilerParams(dimension_semantics=sem, vmem_limit_bytes=VMEM_LIMIT)


def _split3(x):
    hi = x.astype(BF16)
    r1 = x - hi.astype(F32)
    mid = r1.astype(BF16)
    lo = (r1 - mid.astype(F32)).astype(BF16)
    return hi, mid, lo


def _dot_exact01(x, e):
    hi, mid, lo = _split3(x)
    d = lambda a: jnp.dot(a, e, preferred_element_type=F32)
    return d(hi) + d(mid) + d(lo)


def _rms_norm_kernel(x_ref, g_ref, o_ref):
    xf = x_ref[...]
    ms = jnp.mean(xf * xf, axis=-1, keepdims=True)
    o_ref[...] = (xf * lax.rsqrt(ms + EPS) * g_ref[...]).astype(o_ref.dtype)


def _rms_norm_bf16(x, g, *, tm):
    m, d = x.shape
    return pl.pallas_call(
        _rms_norm_kernel,
        grid=(m // tm,),
        in_specs=[pl.BlockSpec((tm, d), lambda i: (i, 0)), pl.BlockSpec((1, d), lambda i: (0, 0))],
        out_specs=pl.BlockSpec((tm, d), lambda i: (i, 0)),
        out_shape=jax.ShapeDtypeStruct((m, d), BF16),
        compiler_params=_cparams(("parallel",)),
    )(x, g)


def _kv_proj_kernel(h_ref, w_ref, b_ref, c_ref, s1_ref, s2_ref, *refs, head_major):
    if head_major:
        o_ref, hm_ref = refs
    else:
        o_ref, = refs
    j = pl.program_id(1)
    acc = jnp.dot(h_ref[...], w_ref[...], preferred_element_type=F32) + b_ref[...]

    def emit(rope):
        c, s1, s2 = c_ref[...], s1_ref[...], s2_ref[...]
        for g in range(KV_GROUPS):
            a = acc[:, g * LANES:(g + 1) * LANES]
            if rope:
                a = a * c + pltpu.roll(a, ROPE_HALF, 1) * s1 + pltpu.roll(a, LANES - ROPE_HALF, 1) * s2
            o_ref[:, 0, g, :] = a
            if head_major:
                hm_ref[0, g] = a.astype(BF16)

    pl.when(j == 0)(functools.partial(emit, True))
    pl.when(j != 0)(functools.partial(emit, False))


def _kv_proj(h, w, b, rope, *, col0, tm, seq_len=None):
    m, d = h.shape
    tn = KV_GROUPS * HEAD_DIM
    jb = col0 // tn
    period = rope[0].shape[0] // tm
    assert m % tm == 0 and col0 % tn == 0 and rope[0].shape[0] % tm == 0
    in_specs = [
        pl.BlockSpec((tm, d), lambda i, j: (i, 0)),
        pl.BlockSpec((d, tn), lambda i, j: (0, j + jb)),
        pl.BlockSpec((1, tn), lambda i, j: (0, j + jb)),
    ] + [pl.BlockSpec((tm, LANES), lambda i, j: (i % period, 0))] * 3
    out_shape = [jax.ShapeDtypeStruct((m, 2, KV_GROUPS, HEAD_DIM), F32)]
    out_specs = [pl.BlockSpec((tm, 1, KV_GROUPS, HEAD_DIM), lambda i, j: (i, j, 0, 0))]
    if seq_len is not None:
        per = seq_len // tm
        assert seq_len % tm == 0
        out_shape.append(jax.ShapeDtypeStruct((m // seq_len, KG, seq_len, HEAD_DIM), BF16))
        out_specs.append(pl.BlockSpec((1, KV_GROUPS, tm, HEAD_DIM), lambda i, j: (i // per, j, i % per, 0)))
    return pl.pallas_call(
        functools.partial(_kv_proj_kernel, head_major=seq_len is not None),
        grid=(m // tm, 2),
        in_specs=in_specs,
        out_specs=out_specs,
        out_shape=out_shape,
        compiler_params=_cparams(("parallel", "arbitrary")),
    )(h, w, b, *rope)


def _proj_kernel(h_ref, w_ref, b_ref, *refs, tn, rope, out_scale):
    if rope:
        c_ref, s1_ref, s2_ref, o_ref = refs
    else:
        o_ref, = refs
    acc = jnp.dot(h_ref[...], w_ref[...], preferred_element_type=F32) + b_ref[...]
    if not rope:
        o_ref[...] = (acc * out_scale if out_scale != 1.0 else acc).astype(o_ref.dtype)
        return
    c, s1, s2 = c_ref[...], s1_ref[...], s2_ref[...]
    for hh in range(tn // LANES):
        a = acc[:, hh * LANES:(hh + 1) * LANES]
        r = a * c + pltpu.roll(a, ROPE_HALF, 1) * s1 + pltpu.roll(a, LANES - ROPE_HALF, 1) * s2
        o_ref[:, hh * LANES:(hh + 1) * LANES] = (r * out_scale if out_scale != 1.0 else r).astype(o_ref.dtype)


def _proj(h, w, b, *, col0, ncols, tm, tn, out_dtype, rope=None, out_scale=1.0):
    m, d = h.shape
    assert m % tm == 0 and ncols % tn == 0 and col0 % tn == 0
    jb = col0 // tn
    in_specs = [
        pl.BlockSpec((tm, d), lambda i, j: (i, 0)),
        pl.BlockSpec((d, tn), lambda i, j: (0, j + jb)),
        pl.BlockSpec((1, tn), lambda i, j: (0, j + jb)),
    ]
    args = [h, w, b]
    if rope is not None:
        period = rope[0].shape[0] // tm
        assert rope[0].shape[0] % tm == 0
        for tab in rope:
            in_specs.append(pl.BlockSpec((tm, LANES), lambda i, j: (i % period, 0)))
            args.append(tab)
    return pl.pallas_call(
        functools.partial(_proj_kernel, tn=tn, rope=rope is not None, out_scale=out_scale),
        grid=(m // tm, ncols // tn),
        in_specs=in_specs,
        out_specs=pl.BlockSpec((tm, tn), lambda i, j: (i, j)),
        out_shape=jax.ShapeDtypeStruct((m, ncols), out_dtype),
        compiler_params=_cparams(("parallel", "parallel")),
    )(*args)


def _small_proj_kernel(x_ref, g_ref, w_ref, b_ref, o_ref):
    xf = x_ref[...]
    h = xf * lax.rsqrt(jnp.mean(xf * xf, axis=-1, keepdims=True) + EPS) * g_ref[...]
    o_ref[...] = jnp.dot(h, w_ref[...], preferred_element_type=F32, precision=lax.Precision.HIGHEST) + b_ref[...]


def _small_proj(x, g, w, b, *, tm):
    m, d = x.shape
    n = w.shape[1]
    return pl.pallas_call(
        _small_proj_kernel,
        grid=(m // tm,),
        in_specs=[pl.BlockSpec((tm, d), lambda i: (i, 0)), pl.BlockSpec((1, d), lambda i: (0, 0)),
                  pl.BlockSpec((d, n), lambda i: (0, 0)), pl.BlockSpec((1, n), lambda i: (0, 0))],
        out_specs=pl.BlockSpec((tm, n), lambda i: (i, 0)),
        out_shape=jax.ShapeDtypeStruct((m, n), F32),
        compiler_params=_cparams(("parallel",)),
    )(x, g, w, b)


def _rope_tables(pos):
    inv = ROPE_THETA ** (-jnp.arange(ROPE_HALF, dtype=F32) / ROPE_HALF)
    ang = pos.astype(F32)[:, None] * inv
    cos, sin = jnp.cos(ang), jnp.sin(ang)
    p = pos.shape[0]
    rest = HEAD_DIM - ROPE_DIM
    c = jnp.concatenate([cos, cos, jnp.ones((p, rest), F32)], axis=1)
    s1 = jnp.concatenate([jnp.zeros((p, ROPE_HALF), F32), sin, jnp.zeros((p, rest), F32)], axis=1)
    s2 = jnp.concatenate([-sin, jnp.zeros((p, ROPE_HALF + rest), F32)], axis=1)
    return c, s1, s2


def _log_sigmoid(x):
    return jnp.minimum(x, 0.0) - jnp.log1p(jnp.exp(-jnp.abs(x)))


def _mlstm_kernel(*refs, L, zero_init):
    if zero_init:
        (q_ref, k_ref, v_ref, o_ref, gc_ref, gr_ref, cwq_ref, cwk_ref, cbq_ref, cbk_ref, cvq_ref, cvk_ref, nw_ref,
         h_ref, c_ref, n_ref, m_ref, xq_scr, xk_scr) = refs
    else:
        (q_ref, k_ref, v_ref, o_ref, gc_ref, gr_ref, cwq_ref, cwk_ref, cbq_ref, cbk_ref, cvq_ref, cvk_ref, nw_ref,
         c0_ref, n0_ref, m0_ref, h_ref, c_ref, n_ref, m_ref, xq_scr, xk_scr) = refs
    ci = pl.program_id(2)
    PAD = 8

    @pl.when(ci == 0)
    def _():
        if zero_init:
            c_ref[...] = jnp.zeros_like(c_ref)
            n_ref[...] = jnp.zeros_like(n_ref)
            m_ref[...] = jnp.zeros_like(m_ref)
        else:
            c_ref[...] = c0_ref[...]
            n_ref[...] = n0_ref[...]
            m_ref[...] = m0_ref[...]
        xq_scr[0:PAD, :] = cvq_ref[0]
        xk_scr[0:PAD, :] = cvk_ref[0]

    xq_scr[PAD:PAD + L, :] = q_ref[...]
    xk_scr[PAD:PAD + L, :] = k_ref[...]

    def conv_silu(x_scr, w_ref, b_ref):
        acc = b_ref[...] + jnp.zeros((L, M_HEAD_DIM), F32)
        for j in range(CONV_W):
            acc = acc + x_scr[pl.ds(PAD - (CONV_W - 1) + j, L), :] * w_ref[j:j + 1, :]
        return acc * jax.nn.sigmoid(acc)

    qs = conv_silu(xq_scr, cwq_ref, cbq_ref)
    ks = conv_silu(xk_scr, cwk_ref, cbk_ref) * (M_HEAD_DIM ** -0.5)
    xq_scr[0:PAD, :] = xq_scr[L:L + PAD, :]
    xk_scr[0:PAD, :] = xk_scr[L:L + PAD, :]

    gc = gc_ref[0, 0]
    gr = gr_ref[0, 0]
    i_col, f_col = gc[:, 0:1], _log_sigmoid(gc[:, 1:2])
    i_row, f_row = gr[0:1, :], _log_sigmoid(gr[1:2, :])
    t_idx = lax.broadcasted_iota(jnp.int32, (L, L), 0)
    s_idx = lax.broadcasted_iota(jnp.int32, (L, L), 1)
    tril = s_idx <= t_idx
    b_col = jnp.sum(jnp.where(tril, f_row, 0.0), axis=1, keepdims=True)
    b_row = jnp.sum(jnp.where(t_idx <= s_idx, f_col, 0.0), axis=0, keepdims=True)
    m_prev = m_ref[0, 0]
    dmat = jnp.where(tril, b_col - b_row + i_row, NEG_BIG)
    m_t = jnp.maximum(m_prev + b_col, jnp.max(dmat, axis=1, keepdims=True))
    w = jnp.exp(dmat - m_t)
    inter = jnp.exp(m_prev + b_col - m_t)

    c_prev = c_ref[0, 0]
    n_prev = n_ref[0, 0]
    qb, kb, vb = qs.astype(BF16), ks.astype(BF16), v_ref[...].astype(BF16)
    s = lax.dot_general(qb, kb, (((1,), (1,)), ((), ())), preferred_element_type=F32) * w
    num = inter * jnp.dot(qb, c_prev.astype(BF16), preferred_element_type=F32) \
        + jnp.dot(s.astype(BF16), vb, preferred_element_type=F32)
    den = inter * jnp.sum(qs * n_prev, axis=1, keepdims=True) + jnp.sum(s, axis=1, keepdims=True)
    h = num / jnp.maximum(jnp.abs(den), jnp.exp(-m_t))

    m_new = m_t[L - 1:L, :]
    b_last = b_col[L - 1:L, :]
    decay = jnp.exp(m_prev + b_last - m_new)
    g_col = jnp.exp(i_col + b_last - b_col - m_new)
    kg = ks * g_col
    c_ref[0, 0] = decay * c_prev + lax.dot_general(kg.astype(BF16), vb, (((0,), (0,)), ((), ())),
                                                   preferred_element_type=F32)
    n_ref[0, 0] = decay * n_prev + jnp.sum(kg, axis=0, keepdims=True)
    m_ref[0, 0] = m_new

    ho = jax.nn.sigmoid(o_ref[...]) * h
    mu = jnp.mean(ho, axis=1, keepdims=True)
    var = jnp.mean(jnp.square(ho - mu), axis=1, keepdims=True)
    h_ref[...] = ((ho - mu) * lax.rsqrt(var + EPS) * nw_ref[...]).astype(h_ref.dtype)


def _mlstm(rest, gates_col, gates_row, conv_w, conv_b, conv0, norm_w, state, *, nb, t, L):
    nc = t // L
    hb = M_HEAD_DIM
    zero_init = state is None
    row = lambda b, h, c: b * nc + c
    in_specs = [
        pl.BlockSpec((L, hb), lambda b, h, c: (row(b, h, c), R_MQK // hb + h)),
        pl.BlockSpec((L, hb), lambda b, h, c: (row(b, h, c), (R_MQK + M_INNER) // hb + h)),
        pl.BlockSpec((L, hb), lambda b, h, c: (row(b, h, c), R_MV // hb + h)),
        pl.BlockSpec((L, hb), lambda b, h, c: (row(b, h, c), R_MO // hb + h)),
        pl.BlockSpec((1, 1, L, 2), lambda b, h, c: (b, h, c, 0)),
        pl.BlockSpec((1, 1, 2, L), lambda b, h, c: (b, h, 0, c)),
        pl.BlockSpec((CONV_W, hb), lambda b, h, c: (0, h)),
        pl.BlockSpec((CONV_W, hb), lambda b, h, c: (0, M_HEADS + h)),
        pl.BlockSpec((1, hb), lambda b, h, c: (0, h)),
        pl.BlockSpec((1, hb), lambda b, h, c: (0, M_HEADS + h)),
        pl.BlockSpec((1, 8, hb), lambda b, h, c: (b, 0, h)),
        pl.BlockSpec((1, 8, hb), lambda b, h, c: (b, 0, M_HEADS + h)),
        pl.BlockSpec((1, hb), lambda b, h, c: (0, h)),
    ]
    args = [rest, rest, rest, rest, gates_col, gates_row, conv_w, conv_w, conv_b, conv_b, conv0, conv0, norm_w]
    st_specs = [
        pl.BlockSpec((1, 1, hb, hb), lambda b, h, c: (b, h, 0, 0)),
        pl.BlockSpec((1, 1, 1, hb), lambda b, h, c: (b, h, 0, 0)),
        pl.BlockSpec((1, 1, 1, 1), lambda b, h, c: (b, h, 0, 0)),
    ]
    if not zero_init:
        in_specs += st_specs
        args += list(state)
    out_shape = [
        jax.ShapeDtypeStruct((nb * t, M_INNER), BF16),
        jax.ShapeDtypeStruct((nb, M_HEADS, hb, hb), F32),
        jax.ShapeDtypeStruct((nb, M_HEADS, 1, hb), F32),
        jax.ShapeDtypeStruct((nb, M_HEADS, 1, 1), F32),
    ]
    out_specs = [pl.BlockSpec((L, hb), lambda b, h, c: (row(b, h, c), h))] + st_specs
    return pl.pallas_call(
        functools.partial(_mlstm_kernel, L=L, zero_init=zero_init),
        grid=(nb, M_HEADS, nc),
        in_specs=in_specs,
        out_specs=out_specs,
        out_shape=out_shape,
        scratch_shapes=[pltpu.VMEM((L + 8, hb), F32), pltpu.VMEM((L + 8, hb), F32)],
        compiler_params=_cparams(("parallel", "parallel", "arbitrary")),
    )(*args)


def _rms(y, g):
    return y * lax.rsqrt(jnp.mean(y * y, axis=-1, keepdims=True) + EPS) * g


def _merge_out_kernel(ga_ref, gm_ref, att_ref, mh_ref, x_ref, w_ref, g_ref, o_ref):
    f = lambda r: r[...].astype(F32)
    mixed = jax.nn.sigmoid(f(ga_ref)) * f(att_ref) + jax.nn.sigmoid(f(gm_ref)) * f(mh_ref)
    y = jnp.dot(mixed.astype(BF16), w_ref[...], preferred_element_type=F32)
    o_ref[...] = x_ref[...] + _rms(y, g_ref[...])


def _merge_out(merge, att, mh, x, w_out, g_post, *, tm):
    m, d = x.shape
    row = pl.BlockSpec((tm, d), lambda i: (i, 0))
    return pl.pallas_call(
        _merge_out_kernel,
        grid=(m // tm,),
        in_specs=[
            row,
            pl.BlockSpec((tm, d), lambda i: (i, 1)),
            row, row, row,
            pl.BlockSpec((d, d), lambda i: (0, 0)),
            pl.BlockSpec((1, d), lambda i: (0, 0)),
        ],
        out_specs=row,
        out_shape=jax.ShapeDtypeStruct((m, d), F32),
        compiler_params=_cparams(("parallel",)),
    )(merge, merge, att, mh, x, w_out, g_post)


def _mlp_kernel(x_ref, gpre_ref, wu_ref, wd_ref, gpost_ref, o_ref, h_scr, acc_scr):
    f = pl.program_id(1)

    @pl.when(f == 0)
    def _():
        h_scr[...] = _rms(x_ref[...], gpre_ref[...]).astype(BF16)
        acc_scr[...] = jnp.zeros_like(acc_scr)

    u = jnp.dot(h_scr[...], wu_ref[...], preferred_element_type=F32)
    u = jnp.square(jnp.maximum(u, 0.0))
    acc_scr[...] += jnp.dot(u.astype(BF16), wd_ref[...], preferred_element_type=F32)

    @pl.when(f == pl.num_programs(1) - 1)
    def _():
        o_ref[...] = x_ref[...] + _rms(acc_scr[...], gpost_ref[...])


def _mlp(x, g_pre, w_up, w_down, g_post, *, tm, tf):
    m, d = x.shape
    ff = w_up.shape[1]
    return pl.pallas_call(
        _mlp_kernel,
        grid=(m // tm, ff // tf),
        in_specs=[
            pl.BlockSpec((tm, d), lambda i, f: (i, 0)),
            pl.BlockSpec((1, d), lambda i, f: (0, 0)),
            pl.BlockSpec((d, tf), lambda i, f: (0, f)),
            pl.BlockSpec((tf, d), lambda i, f: (f, 0)),
            pl.BlockSpec((1, d), lambda i, f: (0, 0)),
        ],
        out_specs=pl.BlockSpec((tm, d), lambda i, f: (i, 0)),
        out_shape=jax.ShapeDtypeStruct((m, d), F32),
        scratch_shapes=[pltpu.VMEM((tm, d), BF16), pltpu.VMEM((tm, d), F32)],
        compiler_params=_cparams(("parallel", "arbitrary")),
    )(x, g_pre, w_up, w_down, g_post)


CMP_TAIL_SLOTS = 16


def _pe_bias_kernel(pet_ref, w1_ref, o_ref):
    rows = []
    for k in range(2):
        acc = jnp.zeros((1, HEAD_DIM), F32)
        for l in range(CMP_BLOCK):
            acc = acc + jnp.sum(pet_ref[k][:, l:l + 1] * w1_ref[k, l], axis=0, keepdims=True)
        rows += [acc] * KV_GROUPS
    o_ref[...] = jnp.concatenate(rows, axis=0)


def _pe_bias(cmp_pe, cmp_w1):
    pet = jnp.transpose(cmp_pe, (0, 2, 1))
    return pl.pallas_call(
        _pe_bias_kernel,
        out_shape=jax.ShapeDtypeStruct((KG, HEAD_DIM), F32),
        compiler_params=pltpu.CompilerParams(vmem_limit_bytes=VMEM_LIMIT),
    )(pet, cmp_w1)


def _cmp_kernel(pt_ref, *refs, P, has_tail):
    page_refs = refs[:P]
    if has_tail:
        tail_ref, wc_ref, w2_ref, peb_ref, o_ref, ot_ref, carry, res_scr = refs[P:]
    else:
        wc_ref, w2_ref, peb_ref, o_ref, carry, res_scr = refs[P:]
    p = pl.program_id(1)

    def store_head_major(res, out_ref, n_out):
        n = res.shape[0] // KG
        res_scr[0:n * KG, :] = res
        for kg in range(KG):
            rows = res_scr[pl.ds(kg, n, stride=KG), :]
            if n_out > n:
                rows = jnp.concatenate([rows, jnp.zeros((n_out - n, HEAD_DIM), F32)], axis=0)
            out_ref[0, kg] = rows.astype(out_ref.dtype)

    @pl.when(p == 0)
    def _():
        carry[...] = jnp.zeros_like(carry)

    def is_k(rows):
        return (lax.broadcasted_iota(jnp.int32, (rows, 1), 0) % KG) < KV_GROUPS

    def half_proj(xs):
        rows = xs[0].shape[0]
        lhs = jnp.concatenate([x.astype(BF16) for x in xs], axis=1)
        acc = jnp.dot(lhs, wc_ref[...], preferred_element_type=F32)
        ik = is_k(rows)
        a = jnp.where(ik, acc[:, 0:128], acc[:, 256:384])
        b = jnp.where(ik, acc[:, 128:256], acc[:, 384:512])
        return a, b

    def finish(a_prev, b):
        rows = b.shape[0]
        peb = jnp.concatenate([peb_ref[...]] * (rows // KG), axis=0)
        hid = jax.nn.gelu(a_prev + b + peb)
        o2 = jnp.dot(hid.astype(BF16), w2_ref[...], preferred_element_type=F32)
        return jnp.where(is_k(rows), o2[:, 0:128], o2[:, 128:256])

    xs = [jnp.concatenate([r[0, :, l].reshape(CMP_STRIDE // 2 * KG, HEAD_DIM) for r in page_refs], axis=0)
          for l in range(CMP_STRIDE)]
    a, b = half_proj(xs)
    a_prev = jnp.concatenate([carry[...], a[:-KG]], axis=0)
    store_head_major(finish(a_prev, b), o_ref, o_ref.shape[2])
    carry[...] = a[-KG:]

    if has_tail:
        @pl.when(p == pl.num_programs(1) - 1)
        def _():
            nct = tail_ref.shape[1]
            xt = [tail_ref[0, :, l].reshape(nct * KG, HEAD_DIM) for l in range(CMP_STRIDE)]
            at, bt = half_proj(xt)
            ap = jnp.concatenate([a[-KG:], at[:-KG]], axis=0)
            store_head_major(finish(ap, bt), ot_ref, ot_ref.shape[2])


def _cmp_summaries(pool, page_table, tail, wc, w2c, peb, *, P):
    ns, npg = page_table.shape
    cpp = PAGE_SIZE // CMP_STRIDE
    assert npg % P == 0
    has_tail = tail is not None
    blk = (1, cpp, CMP_STRIDE, KG, HEAD_DIM)
    in_specs = [pl.BlockSpec(blk, functools.partial(lambda s, p, pt, u: (pt[s * npg + p * P + u], 0, 0, 0, 0), u=u))
                for u in range(P)]
    args = [pool] * P
    if has_tail:
        nct = tail.shape[1]
        in_specs.append(pl.BlockSpec((1, nct, CMP_STRIDE, KG, HEAD_DIM), lambda s, p, pt: (s, 0, 0, 0, 0)))
        args.append(tail)
    in_specs += [
        pl.BlockSpec(wc.shape, lambda s, p, pt: (0, 0)),
        pl.BlockSpec(w2c.shape, lambda s, p, pt: (0, 0)),
        pl.BlockSpec(peb.shape, lambda s, p, pt: (0, 0)),
    ]
    args += [wc, w2c, peb]
    out_shape = [jax.ShapeDtypeStruct((ns, KG, npg * cpp, HEAD_DIM), BF16)]
    out_specs = [pl.BlockSpec((1, KG, P * cpp, HEAD_DIM), lambda s, p, pt: (s, 0, p, 0))]
    if has_tail:
        out_shape.append(jax.ShapeDtypeStruct((ns, KG, CMP_TAIL_SLOTS, HEAD_DIM), BF16))
        out_specs.append(pl.BlockSpec((1, KG, CMP_TAIL_SLOTS, HEAD_DIM), lambda s, p, pt: (s, 0, 0, 0)))
    return pl.pallas_call(
        functools.partial(_cmp_kernel, P=P, has_tail=has_tail),
        grid_spec=pltpu.PrefetchScalarGridSpec(
            num_scalar_prefetch=1, grid=(ns, npg // P), in_specs=in_specs, out_specs=out_specs,
            scratch_shapes=[pltpu.VMEM((KG, HEAD_DIM), F32), pltpu.VMEM((P * cpp * KG, HEAD_DIM), F32)]),
        out_shape=out_shape,
        compiler_params=_cparams(("parallel", "arbitrary")),
    )(page_table.reshape(-1), *args)


def _pack_cmp_weights(cmp_w1, cmp_w2):
    def cols(l):
        return jnp.concatenate([cmp_w1[0, l], cmp_w1[0, CMP_STRIDE + l], cmp_w1[1, l], cmp_w1[1, CMP_STRIDE + l]], axis=1)
    wc = jnp.concatenate([cols(l) for l in range(CMP_STRIDE)], axis=0).astype(BF16)
    w2c = jnp.concatenate([cmp_w2[0], cmp_w2[1]], axis=1).astype(BF16)
    return wc, w2c


def _overlap_matrix(n_slots, n_sb, width):
    i = jnp.arange(n_slots)[:, None] - 1
    j = jnp.arange(width)[None, :]
    ov = (i >= 0) & (j < n_sb) & (i * CMP_STRIDE < (j + 1) * SLC_BLOCK) & (i * CMP_STRIDE + CMP_BLOCK > j * SLC_BLOCK)
    return ov.astype(BF16)


def _masked_softmax(s, mask):
    s = jnp.where(mask, s, -jnp.inf)
    m = jnp.max(s, axis=-1, keepdims=True)
    m = jnp.where(m > -jnp.inf, m, 0.0)
    e = jnp.exp(s - m)
    return e / jnp.maximum(jnp.sum(e, axis=-1, keepdims=True), 1e-30)


def _dot_nt(a, b):
    return lax.dot_general(a, b, (((1,), (1,)), ((), ())), preferred_element_type=F32)


def _attn_prompt_kernel(q_ref, kc_ref, vc_ref, ks_ref, vs_ref, kw_ref, vw_ref, gate_ref, ov_ref, e_ref, ge_ref,
                        o_ref, os_scr, *, tq, n_sb, sel_step):
    i = pl.program_id(2)
    hpg = HEADS_PER_GROUP
    rows = hpg * tq
    s_len = ks_ref.shape[2]

    q = jnp.concatenate([q_ref[:, h * HEAD_DIM:(h + 1) * HEAD_DIM] for h in range(hpg)], axis=0)
    t0 = i * tq
    t_row = t0 + lax.broadcasted_iota(jnp.int32, (rows, 1), 0) % tq
    t_tok = t0 + lax.broadcasted_iota(jnp.int32, (tq, 1), 0)

    n_slots = kc_ref.shape[2]
    s_c = _dot_nt(q, kc_ref[0, 0])
    slot = lax.broadcasted_iota(jnp.int32, (1, n_slots), 1)
    mask_c = (slot >= 1) & ((slot - 1) * CMP_STRIDE + CMP_BLOCK - 1 <= t_row)
    p_c = _masked_softmax(s_c, mask_c)
    o_c = jnp.dot(p_c.astype(BF16), vc_ref[0, 0], preferred_element_type=F32)
    p_grp = p_c[0:tq]
    for h in range(1, hpg):
        p_grp = p_grp + p_c[h * tq:(h + 1) * tq]

    def select_blocks():
        score = _dot_exact01(p_grp, ov_ref[...])
        jl = lax.broadcasted_iota(jnp.int32, (1, LANES), 1)
        cur = t_tok // SLC_BLOCK
        valid = jl <= cur
        forced = (jl == 0) | (jl == cur) | (jl == cur - 1)
        score = jnp.where(forced, FORCE_SCORE, score)
        score = jnp.where(valid, score, -jnp.inf)
        rank = jnp.zeros((tq, LANES), F32)
        for j2 in range(n_sb):
            col = score[:, j2:j2 + 1]
            beats = (col > score) | ((col == score) & (jl > j2))
            rank = rank + jnp.where(beats, 1.0, 0.0)
        return jnp.where((rank < N_SELECT) & valid, 1.0, 0.0).astype(BF16)

    def attend(k, v, keep):
        bias = jnp.where(keep, 0.0, NEG_BIG)
        s = _dot_nt(q, k)
        es, ls = [], []
        for h in range(hpg):
            sh = s[h * tq:(h + 1) * tq] + bias
            eh = jnp.exp(sh - jnp.max(sh, axis=1, keepdims=True))
            es.append(eh.astype(BF16))
            ls.append(jnp.sum(eh, axis=1, keepdims=True))
        o = jnp.dot(jnp.concatenate(es, axis=0), v, preferred_element_type=F32)
        return o / jnp.concatenate(ls, axis=0)

    for var in range(s_len // sel_step):
        width = (var + 1) * sel_step

        @pl.when((t0 + tq - 1) // sel_step == var)
        def _(width=width):
            kpos = lax.broadcasted_iota(jnp.int32, (1, width), 1)
            keep = kpos <= t_tok
            if width > N_SELECT * SLC_BLOCK:
                selx = jnp.dot(select_blocks(), e_ref[:, 0:width], preferred_element_type=F32)
                keep = keep & (selx > 0.5)
            os_scr[...] = attend(ks_ref[0, 0, 0:width, :], vs_ref[0, 0, 0:width, :], keep)

    band = WINDOW + tq
    start = pl.multiple_of(jnp.maximum(t0 - WINDOW, 0), tq)
    kpos = start + lax.broadcasted_iota(jnp.int32, (1, band), 1)
    o_w = attend(kw_ref[0, 0, pl.ds(start, band), :], vw_ref[0, 0, pl.ds(start, band), :],
                 (kpos <= t_tok) & (kpos > t_tok - WINDOW))
    o_s = os_scr[...]

    sig = jax.nn.sigmoid(gate_ref[...])
    g_c, g_s, g_w = (_dot_exact01(sig, ge_ref[k, 0]) for k in range(3))
    for h in range(hpg):
        hs = slice(h * HEAD_DIM, (h + 1) * HEAD_DIM)
        rs = slice(h * tq, (h + 1) * tq)
        o_ref[:, hs] = (g_c[:, hs] * o_c[rs] + g_s[:, hs] * o_s[rs] + g_w[:, hs] * o_w[rs]).astype(o_ref.dtype)


def _gate_expand(width_heads):
    c = jnp.arange(LANES)[None, :, None]
    k = jnp.arange(3)[:, None, None]
    hh = (jnp.arange(width_heads * HEAD_DIM) // HEAD_DIM)[None, None, :]
    return (c == k * N_HEADS + hh).astype(BF16)


def _attn_prompt(q, kvc_slots, kvs, kvw, small, *, nb, s_len, tq=128, sel_step=512):
    nq = s_len // tq
    n_slots = kvc_slots.shape[2]
    n_sb = s_len // SLC_BLOCK
    assert n_sb <= LANES and n_slots <= LANES and s_len % sel_step == 0 and sel_step % tq == 0
    assert s_len >= WINDOW + tq and WINDOW % tq == 0
    hd = HEAD_DIM
    gw = HEADS_PER_GROUP * hd
    ov = _overlap_matrix(n_slots, n_sb, LANES)
    e = (jnp.arange(LANES)[:, None] == (jnp.arange(s_len) // SLC_BLOCK)[None, :]).astype(BF16)
    ge = _gate_expand(N_HEADS).reshape(3, LANES, KV_GROUPS, gw).transpose(0, 2, 1, 3)
    kspec = lambda kv: pl.BlockSpec((1, 1, s_len, hd), lambda b, g, i: (b, kv * KV_GROUPS + g, 0, 0))
    cspec = lambda kv: pl.BlockSpec((1, 1, n_slots, hd), lambda b, g, i: (b, kv * KV_GROUPS + g, 0, 0))
    return pl.pallas_call(
        functools.partial(_attn_prompt_kernel, tq=tq, n_sb=n_sb, sel_step=sel_step),
        grid=(nb, KV_GROUPS, nq),
        in_specs=[
            pl.BlockSpec((tq, gw), lambda b, g, i: (b * nq + i, g)),
            cspec(0), cspec(1), kspec(0), kspec(1), kspec(0), kspec(1),
            pl.BlockSpec((tq, LANES), lambda b, g, i: (b * nq + i, 0)),
            pl.BlockSpec(ov.shape, lambda b, g, i: (0, 0)),
            pl.BlockSpec(e.shape, lambda b, g, i: (0, 0)),
            pl.BlockSpec((3, 1, LANES, gw), lambda b, g, i: (0, g, 0, 0)),
        ],
        out_specs=pl.BlockSpec((tq, gw), lambda b, g, i: (b * nq + i, g)),
        out_shape=jax.ShapeDtypeStruct((nb * s_len, N_HEADS * hd), BF16),
        scratch_shapes=[pltpu.VMEM((HEADS_PER_GROUP * tq, hd), F32)],
        compiler_params=_cparams(("parallel", "parallel", "parallel")),
    )(q, kvc_slots, kvc_slots, kvs, kvs, kvw, kvw, small, ov, e, ge)


T_PAD = 8


def _attn_sample_a_kernel(q_ref, kcm_ref, kct_ref, wc_ref, wn_ref, ov_ref, oc_ref, ow_ref, idx_ref,
                          *, past, t_new, n_blocks, n_sb, wb):
    hpg = HEADS_PER_GROUP
    rows = hpg * T_PAD
    t_pos = past + lax.broadcasted_iota(jnp.int32, (rows, 1), 0) % T_PAD
    tp8 = past + lax.broadcasted_iota(jnp.int32, (T_PAD, 1), 0)
    n_slots = kcm_ref.shape[2] + kct_ref.shape[2]
    slot = lax.broadcasted_iota(jnp.int32, (1, n_slots), 1)
    mask_c = (slot >= 1) & (slot <= n_blocks) & ((slot - 1) * CMP_STRIDE + CMP_BLOCK - 1 <= t_pos)
    width = ov_ref.shape[1]
    jl = lax.broadcasted_iota(jnp.int32, (1, width), 1)
    jf = jl.astype(F32)
    cur = tp8 // SLC_BLOCK
    valid = (jl <= cur) & (jl < n_sb)
    forced = (jl == 0) | (jl == cur) | (jl == cur - 1)
    lane = lax.broadcasted_iota(jnp.int32, (1, LANES), 1)
    wj = lax.broadcasted_iota(jnp.int32, (1, wb + T_PAD), 1)
    kp = past - wb + wj
    mask_w = (kp <= t_pos) & (kp > t_pos - WINDOW) & (kp >= 0) & (wj < wb + t_new)

    for g in range(KV_GROUPS):
        q = q_ref[0, g].astype(BF16)
        kc = jnp.concatenate([kcm_ref[0, g], kct_ref[0, g]], axis=0)
        vc = jnp.concatenate([kcm_ref[0, KV_GROUPS + g], kct_ref[0, KV_GROUPS + g]], axis=0)
        p_c = _masked_softmax(_dot_nt(q, kc), mask_c)
        oc_ref[0, g] = jnp.dot(p_c.astype(BF16), vc, preferred_element_type=F32)
        p_grp = p_c[0:T_PAD]
        for h in range(1, hpg):
            p_grp = p_grp + p_c[h * T_PAD:(h + 1) * T_PAD]
        score = _dot_exact01(p_grp, ov_ref[...])
        score = jnp.where(forced, FORCE_SCORE, score)
        score = jnp.where(valid, score, -jnp.inf)
        idx = jnp.zeros((T_PAD, LANES), jnp.int32)
        for k in range(N_SELECT):
            mx = jnp.max(score, axis=1, keepdims=True)
            am = jnp.min(jnp.where(score == mx, jf, 1e9), axis=1, keepdims=True)
            pick = jnp.where(mx > -jnp.inf, am, -1.0).astype(jnp.int32)
            idx = jnp.where(lane == k, pick, idx)
            score = jnp.where(jf == am, -jnp.inf, score)
        idx_ref[0, g] = idx

        kw = jnp.concatenate([wc_ref[0, :, 0, g, :], wn_ref[0, :, 0, g, :]], axis=0).astype(BF16)
        vw = jnp.concatenate([wc_ref[0, :, 1, g, :], wn_ref[0, :, 1, g, :]], axis=0).astype(BF16)
        p_w = _masked_softmax(_dot_nt(q, kw), mask_w)
        ow_ref[0, g] = jnp.dot(p_w.astype(BF16), vw, preferred_element_type=F32)


def _attn_sample_a(q4, slots_main, slots_tail, win_cache, win_new, *, past, t_new, n_blocks, n_sb):
    ns = q4.shape[0]
    wb = win_cache.shape[1]
    n_slots = slots_main.shape[2] + slots_tail.shape[2]
    width = -(-n_sb // LANES) * LANES
    ov = _overlap_matrix(n_slots, n_sb, width)
    rows = HEADS_PER_GROUP * T_PAD
    full = lambda a: pl.BlockSpec((1,) + a.shape[1:], lambda s: (s,) + (0,) * (a.ndim - 1))
    o_sds = jax.ShapeDtypeStruct((ns, KV_GROUPS, rows, HEAD_DIM), F32)
    o_spec = pl.BlockSpec((1, KV_GROUPS, rows, HEAD_DIM), lambda s: (s, 0, 0, 0))
    return pl.pallas_call(
        functools.partial(_attn_sample_a_kernel, past=past, t_new=t_new, n_blocks=n_blocks, n_sb=n_sb, wb=wb),
        grid=(ns,),
        in_specs=[full(q4), full(slots_main), full(slots_tail), full(win_cache), full(win_new),
                  pl.BlockSpec(ov.shape, lambda s: (0, 0))],
        out_specs=[o_spec, o_spec, pl.BlockSpec((1, KV_GROUPS, T_PAD, LANES), lambda s: (s, 0, 0, 0))],
        out_shape=[o_sds, o_sds, jax.ShapeDtypeStruct((ns, KV_GROUPS, T_PAD, LANES), jnp.int32)],
        compiler_params=_cparams(("parallel",)),
    )(q4, slots_main, slots_tail, win_cache, win_new, ov)


def _attn_sample_b_kernel(idx_ref, pt_ref, q_ref, pool_ref, tail_ref, o_ref, buf, sem, *, past, nb_past, t_new, npg):
    n = N_SELECT
    G = KV_GROUPS
    bpp = PAGE_SIZE // SLC_BLOCK
    nblk = t_new * n
    s, g = pl.program_id(0), pl.program_id(1)
    step = s * G + g
    nsteps = pl.num_programs(0) * G
    slot = step % 2

    def start_copies(step_, slot_):
        s_, g_ = step_ // G, step_ % G

        def body(j, carry):
            jj = idx_ref[step_ * nblk + j]
            jp = jnp.clip(jj, 0, nb_past - 1)
            page = pt_ref[s_ * npg + jp // bpp]
            r0 = pl.multiple_of((jp % bpp) * SLC_BLOCK, SLC_BLOCK)
            for kv in range(2):
                @pl.when(jj < nb_past)
                def _(kv=kv):
                    pltpu.make_async_copy(pool_ref.at[page, pl.ds(r0, SLC_BLOCK), kv, g_], buf.at[slot_, kv, j],
                                          sem.at[slot_]).start()

                @pl.when(jj >= nb_past)
                def _(kv=kv):
                    pltpu.make_async_copy(tail_ref.at[s_, :, kv, g_], buf.at[slot_, kv, j], sem.at[slot_]).start()
            return carry

        lax.fori_loop(0, nblk, body, 0)

    def wait_copies(slot_):
        def body(j, carry):
            for kv in range(2):
                pltpu.make_async_copy(tail_ref.at[0, :, kv, 0], buf.at[slot_, kv, j], sem.at[slot_]).wait()
            return carry

        lax.fori_loop(0, nblk, body, 0)

    @pl.when(step == 0)
    def _():
        start_copies(step, slot)

    @pl.when(step + 1 < nsteps)
    def _():
        start_copies(step + 1, 1 - slot)

    wait_copies(slot)

    q = q_ref[0, 0].astype(BF16)
    rows = q.shape[0]
    t8 = lax.broadcasted_iota(jnp.int32, (rows, 1), 0) % T_PAD
    jl = lax.broadcasted_iota(jnp.int32, (1, n * SLC_BLOCK), 1)
    kslot = jl // SLC_BLOCK
    out = jnp.zeros((rows, HEAD_DIM), F32)
    for t in range(t_new):
        start = jnp.zeros((1, n * SLC_BLOCK), jnp.int32)
        ok = jnp.zeros((1, n * SLC_BLOCK), jnp.int32)
        for k in range(n):
            jj = idx_ref[step * nblk + t * n + k]
            start = jnp.where(kslot == k, jj * SLC_BLOCK, start)
            ok = jnp.where(kslot == k, (jj >= 0).astype(jnp.int32), ok)
        kpos = start + jl % SLC_BLOCK
        mask = (ok > 0) & (kpos <= past + t)
        kk = buf[slot, 0, t * n:(t + 1) * n].reshape(n * SLC_BLOCK, HEAD_DIM).astype(BF16)
        vv = buf[slot, 1, t * n:(t + 1) * n].reshape(n * SLC_BLOCK, HEAD_DIM).astype(BF16)
        p = _masked_softmax(_dot_nt(q, kk), mask)
        o_t = jnp.dot(p.astype(BF16), vv, preferred_element_type=F32)
        out = jnp.where(t8 == t, o_t, out)
    o_ref[0, 0] = out


def _attn_sample_b(idx_flat, pt_flat, q4, tail, pool, *, past, npg, t_new):
    ns = q4.shape[0]
    G = KV_GROUPS
    rows = q4.shape[2]
    assert tail.shape[1] == SLC_BLOCK
    qspec = pl.BlockSpec((1, 1, rows, HEAD_DIM), lambda s, g, idx, pt: (s, g, 0, 0))
    return pl.pallas_call(
        functools.partial(_attn_sample_b_kernel, past=past, nb_past=past // SLC_BLOCK, t_new=t_new, npg=npg),
        grid_spec=pltpu.PrefetchScalarGridSpec(
            num_scalar_prefetch=2, grid=(ns, G),
            in_specs=[qspec, pl.BlockSpec(memory_space=pl.ANY), pl.BlockSpec(memory_space=pl.ANY)],
            out_specs=qspec,
            scratch_shapes=[pltpu.VMEM((2, 2, t_new * N_SELECT, SLC_BLOCK, HEAD_DIM), F32),
                            pltpu.SemaphoreType.DMA((2,))]),
        out_shape=jax.ShapeDtypeStruct((ns, G, rows, HEAD_DIM), F32),
        compiler_params=_cparams(("arbitrary", "arbitrary")),
    )(idx_flat, pt_flat, q4, pool, tail)


def _nsa_combine_kernel(oc_ref, os_ref, ow_ref, gate_ref, ge_ref, o_ref):
    sig = jax.nn.sigmoid(gate_ref[...])
    o_ref[...] = (_dot_exact01(sig, ge_ref[0]) * oc_ref[...] + _dot_exact01(sig, ge_ref[1]) * os_ref[...]
                  + _dot_exact01(sig, ge_ref[2]) * ow_ref[...])


def _nsa_combine(oc, os_, ow, small):
    ge = _gate_expand(N_HEADS)
    return pl.pallas_call(
        _nsa_combine_kernel,
        out_shape=jax.ShapeDtypeStruct(oc.shape, F32),
        compiler_params=pltpu.CompilerParams(vmem_limit_bytes=VMEM_LIMIT),
    )(oc, os_, ow, small, ge)


def _pack_in_proj(w_in, b_in):
    sizes = (N_HEADS * HEAD_DIM, KV_WIDTH, KV_WIDTH, KV_WIDTH, N_GATE, 2 * M_INNER, M_INNER, M_INNER, 2 * M_HEADS,
             2 * D_MODEL)
    offs = [0]
    for sz in sizes:
        offs.append(offs[-1] + sz)
    o_attg, o_mqk, o_mif, o_merge, end = offs[4], offs[5], offs[8], offs[9], offs[10]
    pad = LANES - N_GATE - 2 * M_HEADS

    def pack(a, dt):
        slabs = [a[..., :o_attg].astype(dt), a[..., o_mqk:o_mif].astype(dt), a[..., o_merge:end].astype(dt)]
        small = jnp.concatenate([a[..., o_attg:o_mqk], a[..., o_mif:o_merge],
                                 jnp.zeros(a.shape[:-1] + (pad,), a.dtype)], axis=-1)
        return slabs, small

    w_slabs, w_small = pack(w_in, BF16)
    b_slabs, b_small = pack(b_in[None, :], F32)
    return w_slabs, b_slabs, w_small, b_small


def _project(x, pos, g_pre, packed, *, tm, seq_len=None):
    (w_att, w_rest, w_merge), (b_att, b_rest, b_merge), w_small, b_small = packed
    tabs = _rope_tables(pos)
    h = _rms_norm_bf16(x, g_pre, tm=min(tm, 512))
    q = _proj(h, w_att, b_att, col0=OFF_Q, ncols=N_HEADS * HEAD_DIM, out_dtype=BF16, tm=tm, tn=512, rope=tabs,
              out_scale=ATT_SCALE)
    kv = [_kv_proj(h, w_att, b_att, tabs, col0=off, tm=tm, seq_len=None if off == OFF_KVC else seq_len)
          for off in (OFF_KVC, OFF_KVS, OFF_KVW)]
    rest = _proj(h, w_rest, b_rest, col0=0, ncols=REST_WIDTH, tm=tm, tn=1024, out_dtype=F32)
    merge = _proj(h, w_merge, b_merge, col0=0, ncols=2 * D_MODEL, tm=tm, tn=1024, out_dtype=BF16)
    small = _small_proj(x, g_pre, w_small, b_small, tm=min(tm, 512))
    return q, kv, rest, merge, small


def _gate_layouts(small, nb, t, t_pad):
    mif = small[:, N_GATE:N_GATE + 2 * M_HEADS].reshape(nb, t, 2, M_HEADS)
    if t_pad > t:
        fill = jnp.broadcast_to(jnp.array([NEG_BIG, 1e4], F32)[None, None, :, None], (nb, t_pad - t, 2, M_HEADS))
        mif = jnp.concatenate([mif, fill], axis=1)
    return jnp.transpose(mif, (0, 3, 1, 2)), jnp.transpose(mif, (0, 3, 2, 1))


def _pad_rows(a, n):
    return jnp.concatenate([a, jnp.zeros((a.shape[0], n - a.shape[1]) + a.shape[2:], a.dtype)], axis=1)


def _nsa_sample(q_s, kvc_s, kvs_s, kvw_s, small_s, cache_kv_cmp, cache_kv_slc, cache_win_kv, page_table,
                wc, w2c, peb, *, DB, T):
    npg = page_table.shape[1]
    past = npg * PAGE_SIZE
    n_phys = cache_kv_cmp.shape[0]
    cpp = PAGE_SIZE // CMP_STRIDE
    t_blk = -(-T // SLC_BLOCK) * SLC_BLOCK
    assert t_blk == SLC_BLOCK
    new_rows = lambda a, n: _pad_rows(a.reshape(DB, T, 2, KV_GROUPS, HEAD_DIM), n)
    cmp_tail = new_rows(kvc_s, t_blk).reshape(DB, t_blk // CMP_STRIDE, CMP_STRIDE, KG, HEAD_DIM)
    slots_s, slots_t = _cmp_summaries(cache_kv_cmp.reshape(n_phys, cpp, CMP_STRIDE, KG, HEAD_DIM), page_table,
                                      cmp_tail, wc, w2c, peb, P=16)
    n_blocks = (past + t_blk) // CMP_STRIDE - 1
    nb_past = past // SLC_BLOCK
    n_sb = nb_past + t_blk // SLC_BLOCK
    q6 = q_s.astype(F32).reshape(DB, T, KV_GROUPS, HEADS_PER_GROUP, HEAD_DIM)
    q4 = _pad_rows(jnp.transpose(q6, (0, 2, 3, 1, 4)).reshape(DB * KV_GROUPS * HEADS_PER_GROUP, T, HEAD_DIM), T_PAD)
    q4 = q4.reshape(DB, KV_GROUPS, HEADS_PER_GROUP * T_PAD, HEAD_DIM)
    oc, ow, idx = _attn_sample_a(q4, slots_s, slots_t, cache_win_kv, new_rows(kvw_s, T_PAD),
                                 past=past, t_new=T, n_blocks=n_blocks, n_sb=n_sb)
    idx_flat = idx[:, :, :T, :N_SELECT].reshape(-1)
    os4 = _attn_sample_b(idx_flat, page_table.reshape(-1), q4, new_rows(kvs_s, SLC_BLOCK), cache_kv_slc,
                         past=past, npg=npg, t_new=T)
    tok_major = lambda o: jnp.transpose(o.reshape(DB, KV_GROUPS, HEADS_PER_GROUP, T_PAD, HEAD_DIM)[:, :, :, :T],
                                        (0, 3, 1, 2, 4)).reshape(DB * T, N_HEADS * HEAD_DIM)
    return _nsa_combine(tok_major(oc), tok_major(os4), tok_major(ow), small_s)


def kernel(x_prompt, x_sample, cache_kv_cmp, cache_kv_slc, cache_win_kv, state_mlstm_c, state_mlstm_n, state_mlstm_m, state_mlstm_conv, page_table, g_mix_pre, w_in, b_in, cmp_pe, cmp_w1, cmp_w2, conv_w, conv_b, mlstm_norm_w, w_out, g_mix_post, g_mlp_pre, w_up, w_down, g_mlp_post):
    B, S, D = x_prompt.shape
    DB, T, _ = x_sample.shape
    npg = page_table.shape[1]
    past = npg * PAGE_SIZE
    n_phys = cache_kv_cmp.shape[0]
    cpp = PAGE_SIZE // CMP_STRIDE
    kv5 = (2, KV_GROUPS, HEAD_DIM)

    packed = _pack_in_proj(w_in, b_in)
    w_out_b, w_up_b, w_down_b = w_out.astype(BF16), w_up.astype(BF16), w_down.astype(BF16)
    row = lambda v: v[None, :]
    peb = _pe_bias(cmp_pe, cmp_w1)
    wc, w2c = _pack_cmp_weights(cmp_w1, cmp_w2)

    def tail(merge, att, mh, x, tm, tm_mlp):
        x1 = _merge_out(merge, att, mh, x, w_out_b, row(g_mix_post), tm=tm)
        return _mlp(x1, row(g_mlp_pre), w_up_b, w_down_b, row(g_mlp_post), tm=tm_mlp, tf=1024)

    xp = x_prompt.reshape(B * S, D)
    q_p, kv_p, rest_p, merge_p, small_p = _project(xp, jnp.arange(S), row(g_mix_pre), packed, tm=1024, seq_len=S)
    (kvc_p,), (kvs_p, kvs_hm), (kvw_p, kvw_hm) = kv_p
    pt_p = jnp.arange(B * S // PAGE_SIZE, dtype=jnp.int32).reshape(B, S // PAGE_SIZE)
    slots_p, = _cmp_summaries(kvc_p.reshape(B * S // PAGE_SIZE, cpp, CMP_STRIDE, KG, HEAD_DIM), pt_p, None,
                              wc, w2c, peb, P=16)
    att_p = _attn_prompt(q_p, slots_p, kvs_hm, kvw_hm, small_p, nb=B, s_len=S)
    gcol, grow = _gate_layouts(small_p, B, S, S)
    mh_p, c_p, n_p, m_p = _mlstm(rest_p, gcol, grow, conv_w, row(conv_b), jnp.zeros((B, 8, 2 * M_INNER), F32),
                                 row(mlstm_norm_w), None, nb=B, t=S, L=256)
    y_p = tail(merge_p, att_p, mh_p, xp, 256, 512)
    wlen = min(WINDOW, S)
    conv_p = rest_p.reshape(B, S, REST_WIDTH)[:, S - (CONV_W - 1):, R_MQK:R_MQK + 2 * M_INNER]

    xs = x_sample.reshape(DB * T, D)
    pos_s = jnp.tile(past + jnp.arange(T), DB)
    q_s, ((kvc_s,), (kvs_s,), (kvw_s,)), rest_s, merge_s, small_s = _project(xs, pos_s, row(g_mix_pre), packed, tm=DB * T)
    att_s = _nsa_sample(q_s, kvc_s, kvs_s, kvw_s, small_s, cache_kv_cmp, cache_kv_slc, cache_win_kv, page_table,
                        wc, w2c, peb, DB=DB, T=T)

    LS = 16
    gcol_s, grow_s = _gate_layouts(small_s, DB, T, LS)
    rest_pad = _pad_rows(rest_s.reshape(DB, T, REST_WIDTH), LS).reshape(DB * LS, REST_WIDTH)
    conv0 = jnp.concatenate([jnp.zeros((DB, 8 - (CONV_W - 1), 2 * M_INNER), F32), state_mlstm_conv], axis=1)
    state = (state_mlstm_c, state_mlstm_n[:, :, None, :], state_mlstm_m[:, :, None, None])
    mh_pad, c_s, n_s, m_s = _mlstm(rest_pad, gcol_s, grow_s, conv_w, row(conv_b), conv0, row(mlstm_norm_w), state,
                                   nb=DB, t=LS, L=LS)
    mh_s = mh_pad.reshape(DB, LS, M_INNER)[:, :T].reshape(DB * T, M_INNER)
    y_s = tail(merge_s, att_s, mh_s, xs, DB * T, DB * T)
    win_s = jnp.concatenate([cache_win_kv[:, T:], kvw_s.reshape((DB, T) + kv5).astype(cache_win_kv.dtype)], axis=1)
    mqk_s = rest_s.reshape(DB, T, REST_WIDTH)[:, :, R_MQK:R_MQK + 2 * M_INNER]
    conv_s = jnp.concatenate([state_mlstm_conv, mqk_s], axis=1)[:, T:]

    return (y_p.reshape(B, S, D), y_s.reshape(DB, T, D),
            kvc_p.reshape((B, S) + kv5), kvc_s.reshape((DB, T) + kv5),
            kvs_p.reshape((B, S) + kv5), kvs_s.reshape((DB, T) + kv5),
            kvw_p.reshape((B, S) + kv5)[:, S - wlen:], win_s,
            c_p, c_s, n_p[:, :, 0], n_s[:, :, 0], m_p[:, :, 0, 0], m_s[:, :, 0, 0], conv_p, conv_s)
```

```python
import functools

import jax
import jax.numpy as jnp
from jax import lax
from jax.experimental import pallas as pl
from jax.experimental.pallas import tpu as pltpu

F32 = jnp.float32
BF16 = jnp.bfloat16

D_MODEL = 2048
N_HEADS = 16
HEAD_DIM = 128
KV_GROUPS = 4
HEADS_PER_GROUP = N_HEADS // KV_GROUPS
ROPE_DIM = HEAD_DIM // 4
ROPE_HALF = ROPE_DIM // 2
ROPE_THETA = 500000.0
CMP_STRIDE = 16
CMP_BLOCK = 2 * CMP_STRIDE
SLC_BLOCK = 64
N_SELECT = 16
WINDOW = 512
ATT_SCALE = HEAD_DIM ** -0.5
LOG2E = 1.4426950408889634
FORCE_SCORE = 1e9
M_HEADS = 4
M_INNER = D_MODEL
M_HEAD_DIM = M_INNER // M_HEADS
CONV_W = 4
NEG_BIG = -1e30
D_FF = 4 * D_MODEL
EPS = 1e-6
PAGE_SIZE = 128
KV_WIDTH = 2 * KV_GROUPS * HEAD_DIM
KG = 2 * KV_GROUPS

LANES = 128
VMEM_LIMIT = 56 * 1024 * 1024

OFF_Q = 0
OFF_KVC = OFF_Q + N_HEADS * HEAD_DIM
OFF_KVS = OFF_KVC + KV_WIDTH
OFF_KVW = OFF_KVS + KV_WIDTH
ATT_WIDTH = OFF_KVW + KV_WIDTH
REST_WIDTH = 2 * M_INNER + M_INNER + M_INNER
N_GATE = 3 * N_HEADS
R_MQK, R_MV, R_MO = 0, 2 * M_INNER, 3 * M_INNER


def _cparams(sem):
    return pltpu.CompilerParams(dimension_semantics=sem, vmem_limit_bytes=VMEM_LIMIT)


def _split3(x):
    hi = x.astype(BF16)
    r1 = x - hi.astype(F32)
    mid = r1.astype(BF16)
    lo = (r1 - mid.astype(F32)).astype(BF16)
    return hi, mid, lo


def _dot_exact01(x, e):
    hi, mid, lo = _split3(x)
    d = lambda a: jnp.dot(a, e, preferred_element_type=F32)
    return d(hi) + d(mid) + d(lo)


def _rms_norm_kernel(x_ref, g_ref, o_ref):
    xf = x_ref[...]
    ms = jnp.mean(xf * xf, axis=-1, keepdims=True)
    o_ref[...] = (xf * lax.rsqrt(ms + EPS) * g_ref[...]).astype(o_ref.dtype)


def _rms_norm_bf16(x, g, *, tm):
    m, d = x.shape
    return pl.pallas_call(
        _rms_norm_kernel,
        grid=(m // tm,),
        in_specs=[pl.BlockSpec((tm, d), lambda i: (i, 0)), pl.BlockSpec((1, d), lambda i: (0, 0))],
        out_specs=pl.BlockSpec((tm, d), lambda i: (i, 0)),
        out_shape=jax.ShapeDtypeStruct((m, d), BF16),
        compiler_params=_cparams(("parallel",)),
    )(x, g)


def _kv_proj_kernel(h_ref, w_ref, b_ref, c_ref, s1_ref, s2_ref, *refs, head_major):
    if head_major:
        o_ref, hm_ref = refs
    else:
        o_ref, = refs
    j = pl.program_id(1)
    acc = jnp.dot(h_ref[...], w_ref[...], preferred_element_type=F32) + b_ref[...]

    def emit(rope):
        c, s1, s2 = c_ref[...], s1_ref[...], s2_ref[...]
        for g in range(KV_GROUPS):
            a = acc[:, g * LANES:(g + 1) * LANES]
            if rope:
                a = a * c + pltpu.roll(a, ROPE_HALF, 1) * s1 + pltpu.roll(a, LANES - ROPE_HALF, 1) * s2
            o_ref[:, 0, g, :] = a
            if head_major:
                hm_ref[0, g] = a.astype(BF16)

    pl.when(j == 0)(functools.partial(emit, True))
    pl.when(j != 0)(functools.partial(emit, False))


def _kv_proj(h, w, b, rope, *, col0, tm, seq_len=None):
    m, d = h.shape
    tn = KV_GROUPS * HEAD_DIM
    jb = col0 // tn
    period = rope[0].shape[0] // tm
    assert m % tm == 0 and col0 % tn == 0 and rope[0].shape[0] % tm == 0
    in_specs = [
        pl.BlockSpec((tm, d), lambda i, j: (i, 0)),
        pl.BlockSpec((d, tn), lambda i, j: (0, j + jb)),
        pl.BlockSpec((1, tn), lambda i, j: (0, j + jb)),
    ] + [pl.BlockSpec((tm, LANES), lambda i, j: (i % period, 0))] * 3
    out_shape = [jax.ShapeDtypeStruct((m, 2, KV_GROUPS, HEAD_DIM), F32)]
    out_specs = [pl.BlockSpec((tm, 1, KV_GROUPS, HEAD_DIM), lambda i, j: (i, j, 0, 0))]
    if seq_len is not None:
        per = seq_len // tm
        assert seq_len % tm == 0
        out_shape.append(jax.ShapeDtypeStruct((m // seq_len, KG, seq_len, HEAD_DIM), BF16))
        out_specs.append(pl.BlockSpec((1, KV_GROUPS, tm, HEAD_DIM), lambda i, j: (i // per, j, i % per, 0)))
    return pl.pallas_call(
        functools.partial(_kv_proj_kernel, head_major=seq_len is not None),
        grid=(m // tm, 2),
        in_specs=in_specs,
        out_specs=out_specs,
        out_shape=out_shape,
        compiler_params=_cparams(("parallel", "arbitrary")),
    )(h, w, b, *rope)


def _proj_kernel(h_ref, w_ref, b_ref, *refs, tn, rope, out_scale):
    if rope:
        c_ref, s1_ref, s2_ref, o_ref = refs
    else:
        o_ref, = refs
    acc = jnp.dot(h_ref[...], w_ref[...], preferred_element_type=F32) + b_ref[...]
    if not rope:
        o_ref[...] = (acc * out_scale if out_scale != 1.0 else acc).astype(o_ref.dtype)
        return
    c, s1, s2 = c_ref[...], s1_ref[...], s2_ref[...]
    for hh in range(tn // LANES):
        a = acc[:, hh * LANES:(hh + 1) * LANES]
        r = a * c + pltpu.roll(a, ROPE_HALF, 1) * s1 + pltpu.roll(a, LANES - ROPE_HALF, 1) * s2
        o_ref[:, hh * LANES:(hh + 1) * LANES] = (r * out_scale if out_scale != 1.0 else r).astype(o_ref.dtype)


def _proj(h, w, b, *, col0, ncols, tm, tn, out_dtype, rope=None, out_scale=1.0):
    m, d = h.shape
    assert m % tm == 0 and ncols % tn == 0 and col0 % tn == 0
    jb = col0 // tn
    in_specs = [
        pl.BlockSpec((tm, d), lambda i, j: (i, 0)),
        pl.BlockSpec((d, tn), lambda i, j: (0, j + jb)),
        pl.BlockSpec((1, tn), lambda i, j: (0, j + jb)),
    ]
    args = [h, w, b]
    if rope is not None:
        period = rope[0].shape[0] // tm
        assert rope[0].shape[0] % tm == 0
        for tab in rope:
            in_specs.append(pl.BlockSpec((tm, LANES), lambda i, j: (i % period, 0)))
            args.append(tab)
    return pl.pallas_call(
        functools.partial(_proj_kernel, tn=tn, rope=rope is not None, out_scale=out_scale),
        grid=(m // tm, ncols // tn),
        in_specs=in_specs,
        out_specs=pl.BlockSpec((tm, tn), lambda i, j: (i, j)),
        out_shape=jax.ShapeDtypeStruct((m, ncols), out_dtype),
        compiler_params=_cparams(("parallel", "parallel")),
    )(*args)


def _small_proj_kernel(x_ref, g_ref, w_ref, b_ref, o_ref):
    xf = x_ref[...]
    h = xf * lax.rsqrt(jnp.mean(xf * xf, axis=-1, keepdims=True) + EPS) * g_ref[...]
    o_ref[...] = jnp.dot(h, w_ref[...], preferred_element_type=F32, precision=lax.Precision.HIGHEST) + b_ref[...]


def _small_proj(x, g, w, b, *, tm):
    m, d = x.shape
    n = w.shape[1]
    return pl.pallas_call(
        _small_proj_kernel,
        grid=(m // tm,),
        in_specs=[pl.BlockSpec((tm, d), lambda i: (i, 0)), pl.BlockSpec((1, d), lambda i: (0, 0)),
                  pl.BlockSpec((d, n), lambda i: (0, 0)), pl.BlockSpec((1, n), lambda i: (0, 0))],
        out_specs=pl.BlockSpec((tm, n), lambda i: (i, 0)),
        out_shape=jax.ShapeDtypeStruct((m, n), F32),
        compiler_params=_cparams(("parallel",)),
    )(x, g, w, b)


def _rope_tables(pos):
    inv = ROPE_THETA ** (-jnp.arange(ROPE_HALF, dtype=F32) / ROPE_HALF)
    ang = pos.astype(F32)[:, None] * inv
    cos, sin = jnp.cos(ang), jnp.sin(ang)
    p = pos.shape[0]
    rest = HEAD_DIM - ROPE_DIM
    c = jnp.concatenate([cos, cos, jnp.ones((p, rest), F32)], axis=1)
    s1 = jnp.concatenate([jnp.zeros((p, ROPE_HALF), F32), sin, jnp.zeros((p, rest), F32)], axis=1)
    s2 = jnp.concatenate([-sin, jnp.zeros((p, ROPE_HALF + rest), F32)], axis=1)
    return c, s1, s2


def _log_sigmoid(x):
    return jnp.minimum(x, 0.0) - jnp.log1p(jnp.exp(-jnp.abs(x)))


def _mlstm_kernel(*refs, L, zero_init):
    if zero_init:
        (q_ref, k_ref, v_ref, o_ref, gc_ref, gr_ref, cwq_ref, cwk_ref, cbq_ref, cbk_ref, cvq_ref, cvk_ref, nw_ref,
         h_ref, c_ref, n_ref, m_ref, xq_scr, xk_scr) = refs
    else:
        (q_ref, k_ref, v_ref, o_ref, gc_ref, gr_ref, cwq_ref, cwk_ref, cbq_ref, cbk_ref, cvq_ref, cvk_ref, nw_ref,
         c0_ref, n0_ref, m0_ref, h_ref, c_ref, n_ref, m_ref, xq_scr, xk_scr) = refs
    ci = pl.program_id(2)
    PAD = 8

    @pl.when(ci == 0)
    def _():
        if zero_init:
            c_ref[...] = jnp.zeros_like(c_ref)
            n_ref[...] = jnp.zeros_like(n_ref)
            m_ref[...] = jnp.zeros_like(m_ref)
        else:
            c_ref[...] = c0_ref[...]
            n_ref[...] = n0_ref[...]
            m_ref[...] = m0_ref[...]
        xq_scr[0:PAD, :] = cvq_ref[0]
        xk_scr[0:PAD, :] = cvk_ref[0]

    xq_scr[PAD:PAD + L, :] = q_ref[...]
    xk_scr[PAD:PAD + L, :] = k_ref[...]

    def conv_silu(x_scr, w_ref, b_ref):
        acc = b_ref[...] + jnp.zeros((L, M_HEAD_DIM), F32)
        for j in range(CONV_W):
            acc = acc + x_scr[pl.ds(PAD - (CONV_W - 1) + j, L), :] * w_ref[j:j + 1, :]
        return acc * jax.nn.sigmoid(acc)

    qs = conv_silu(xq_scr, cwq_ref, cbq_ref)
    ks = conv_silu(xk_scr, cwk_ref, cbk_ref) * (M_HEAD_DIM ** -0.5)
    xq_scr[0:PAD, :] = xq_scr[L:L + PAD, :]
    xk_scr[0:PAD, :] = xk_scr[L:L + PAD, :]

    gc = gc_ref[0, 0]
    gr = gr_ref[0, 0]
    i_col, f_col = gc[:, 0:1], _log_sigmoid(gc[:, 1:2])
    i_row, f_row = gr[0:1, :], _log_sigmoid(gr[1:2, :])
    t_idx = lax.broadcasted_iota(jnp.int32, (L, L), 0)
    s_idx = lax.broadcasted_iota(jnp.int32, (L, L), 1)
    tril = s_idx <= t_idx
    b_col = jnp.sum(jnp.where(tril, f_row, 0.0), axis=1, keepdims=True)
    b_row = jnp.sum(jnp.where(t_idx <= s_idx, f_col, 0.0), axis=0, keepdims=True)
    m_prev = m_ref[0, 0]
    dmat = jnp.where(tril, b_col - b_row + i_row, NEG_BIG)
    m_t = jnp.maximum(m_prev + b_col, jnp.max(dmat, axis=1, keepdims=True))
    w = jnp.exp(dmat - m_t)
    inter = jnp.exp(m_prev + b_col - m_t)

    c_prev = c_ref[0, 0]
    n_prev = n_ref[0, 0]
    qb, kb, vb = qs.astype(BF16), ks.astype(BF16), v_ref[...].astype(BF16)
    s = lax.dot_general(qb, kb, (((1,), (1,)), ((), ())), preferred_element_type=F32) * w
    num = inter * jnp.dot(qb, c_prev.astype(BF16), preferred_element_type=F32) \
        + jnp.dot(s.astype(BF16), vb, preferred_element_type=F32)
    den = inter * jnp.sum(qs * n_prev, axis=1, keepdims=True) + jnp.sum(s, axis=1, keepdims=True)
    h = num / jnp.maximum(jnp.abs(den), jnp.exp(-m_t))

    m_new = m_t[L - 1:L, :]
    b_last = b_col[L - 1:L, :]
    decay = jnp.exp(m_prev + b_last - m_new)
    g_col = jnp.exp(i_col + b_last - b_col - m_new)
    kg = ks * g_col
    c_ref[0, 0] = decay * c_prev + lax.dot_general(kg.astype(BF16), vb, (((0,), (0,)), ((), ())),
                                                   preferred_element_type=F32)
    n_ref[0, 0] = decay * n_prev + jnp.sum(kg, axis=0, keepdims=True)
    m_ref[0, 0] = m_new

    ho = jax.nn.sigmoid(o_ref[...]) * h
    mu = jnp.mean(ho, axis=1, keepdims=True)
    var = jnp.mean(jnp.square(ho - mu), axis=1, keepdims=True)
    h_ref[...] = ((ho - mu) * lax.rsqrt(var + EPS) * nw_ref[...]).astype(h_ref.dtype)


def _mlstm(rest, gates_col, gates_row, conv_w, conv_b, conv0, norm_w, state, *, nb, t, L):
    nc = t // L
    hb = M_HEAD_DIM
    zero_init = state is None
    row = lambda b, h, c: b * nc + c
    in_specs = [
        pl.BlockSpec((L, hb), lambda b, h, c: (row(b, h, c), R_MQK // hb + h)),
        pl.BlockSpec((L, hb), lambda b, h, c: (row(b, h, c), (R_MQK + M_INNER) // hb + h)),
        pl.BlockSpec((L, hb), lambda b, h, c: (row(b, h, c), R_MV // hb + h)),
        pl.BlockSpec((L, hb), lambda b, h, c: (row(b, h, c), R_MO // hb + h)),
        pl.BlockSpec((1, 1, L, 2), lambda b, h, c: (b, h, c, 0)),
        pl.BlockSpec((1, 1, 2, L), lambda b, h, c: (b, h, 0, c)),
        pl.BlockSpec((CONV_W, hb), lambda b, h, c: (0, h)),
        pl.BlockSpec((CONV_W, hb), lambda b, h, c: (0, M_HEADS + h)),
        pl.BlockSpec((1, hb), lambda b, h, c: (0, h)),
        pl.BlockSpec((1, hb), lambda b, h, c: (0, M_HEADS + h)),
        pl.BlockSpec((1, 8, hb), lambda b, h, c: (b, 0, h)),
        pl.BlockSpec((1, 8, hb), lambda b, h, c: (b, 0, M_HEADS + h)),
        pl.BlockSpec((1, hb), lambda b, h, c: (0, h)),
    ]
    args = [rest, rest, rest, rest, gates_col, gates_row, conv_w, conv_w, conv_b, conv_b, conv0, conv0, norm_w]
    st_specs = [
        pl.BlockSpec((1, 1, hb, hb), lambda b, h, c: (b, h, 0, 0)),
        pl.BlockSpec((1, 1, 1, hb), lambda b, h, c: (b, h, 0, 0)),
        pl.BlockSpec((1, 1, 1, 1), lambda b, h, c: (b, h, 0, 0)),
    ]
    if not zero_init:
        in_specs += st_specs
        args += list(state)
    out_shape = [
        jax.ShapeDtypeStruct((nb * t, M_INNER), BF16),
        jax.ShapeDtypeStruct((nb, M_HEADS, hb, hb), F32),
        jax.ShapeDtypeStruct((nb, M_HEADS, 1, hb), F32),
        jax.ShapeDtypeStruct((nb, M_HEADS, 1, 1), F32),
    ]
    out_specs = [pl.BlockSpec((L, hb), lambda b, h, c: (row(b, h, c), h))] + st_specs
    return pl.pallas_call(
        functools.partial(_mlstm_kernel, L=L, zero_init=zero_init),
        grid=(nb, M_HEADS, nc),
        in_specs=in_specs,
        out_specs=out_specs,
        out_shape=out_shape,
        scratch_shapes=[pltpu.VMEM((L + 8, hb), F32), pltpu.VMEM((L + 8, hb), F32)],
        compiler_params=_cparams(("parallel", "parallel", "arbitrary")),
    )(*args)


def _rms(y, g):
    return y * lax.rsqrt(jnp.mean(y * y, axis=-1, keepdims=True) + EPS) * g


def _merge_out_kernel(ga_ref, gm_ref, att_ref, mh_ref, x_ref, w_ref, g_ref, o_ref):
    f = lambda r: r[...].astype(F32)
    mixed = jax.nn.sigmoid(f(ga_ref)) * f(att_ref) + jax.nn.sigmoid(f(gm_ref)) * f(mh_ref)
    y = jnp.dot(mixed.astype(BF16), w_ref[...], preferred_element_type=F32)
    o_ref[...] = x_ref[...] + _rms(y, g_ref[...])


def _merge_out(merge, att, mh, x, w_out, g_post, *, tm):
    m, d = x.shape
    row = pl.BlockSpec((tm, d), lambda i: (i, 0))
    return pl.pallas_call(
        _merge_out_kernel,
        grid=(m // tm,),
        in_specs=[
            row,
            pl.BlockSpec((tm, d), lambda i: (i, 1)),
            row, row, row,
            pl.BlockSpec((d, d), lambda i: (0, 0)),
            pl.BlockSpec((1, d), lambda i: (0, 0)),
        ],
        out_specs=row,
        out_shape=jax.ShapeDtypeStruct((m, d), F32),
        compiler_params=_cparams(("parallel",)),
    )(merge, merge, att, mh, x, w_out, g_post)


def _mlp_kernel(x_ref, gpre_ref, wu_ref, wd_ref, gpost_ref, o_ref, h_scr, acc_scr):
    f = pl.program_id(1)

    @pl.when(f == 0)
    def _():
        h_scr[...] = _rms(x_ref[...], gpre_ref[...]).astype(BF16)
        acc_scr[...] = jnp.zeros_like(acc_scr)

    u = jnp.dot(h_scr[...], wu_ref[...], preferred_element_type=F32)
    u = jnp.square(jnp.maximum(u, 0.0))
    acc_scr[...] += jnp.dot(u.astype(BF16), wd_ref[...], preferred_element_type=F32)

    @pl.when(f == pl.num_programs(1) - 1)
    def _():
        o_ref[...] = x_ref[...] + _rms(acc_scr[...], gpost_ref[...])


def _mlp(x, g_pre, w_up, w_down, g_post, *, tm, tf):
    m, d = x.shape
    ff = w_up.shape[1]
    return pl.pallas_call(
        _mlp_kernel,
        grid=(m // tm, ff // tf),
        in_specs=[
            pl.BlockSpec((tm, d), lambda i, f: (i, 0)),
            pl.BlockSpec((1, d), lambda i, f: (0, 0)),
            pl.BlockSpec((d, tf), lambda i, f: (0, f)),
            pl.BlockSpec((tf, d), lambda i, f: (f, 0)),
            pl.BlockSpec((1, d), lambda i, f: (0, 0)),
        ],
        out_specs=pl.BlockSpec((tm, d), lambda i, f: (i, 0)),
        out_shape=jax.ShapeDtypeStruct((m, d), F32),
        scratch_shapes=[pltpu.VMEM((tm, d), BF16), pltpu.VMEM((tm, d), F32)],
        compiler_params=_cparams(("parallel", "arbitrary")),
    )(x, g_pre, w_up, w_down, g_post)


CMP_TAIL_SLOTS = 16


def _pe_bias_kernel(pet_ref, w1_ref, o_ref):
    rows = []
    for k in range(2):
        acc = jnp.zeros((1, HEAD_DIM), F32)
        for l in range(CMP_BLOCK):
            acc = acc + jnp.sum(pet_ref[k][:, l:l + 1] * w1_ref[k, l], axis=0, keepdims=True)
        rows += [acc] * KV_GROUPS
    o_ref[...] = jnp.concatenate(rows, axis=0)


def _pe_bias(cmp_pe, cmp_w1):
    pet = jnp.transpose(cmp_pe, (0, 2, 1))
    return pl.pallas_call(
        _pe_bias_kernel,
        out_shape=jax.ShapeDtypeStruct((KG, HEAD_DIM), F32),
        compiler_params=pltpu.CompilerParams(vmem_limit_bytes=VMEM_LIMIT),
    )(pet, cmp_w1)


def _cmp_kernel(pt_ref, *refs, P, has_tail):
    page_refs = refs[:P]
    if has_tail:
        tail_ref, wc_ref, w2_ref, peb_ref, o_ref, ot_ref, carry, res_scr = refs[P:]
    else:
        wc_ref, w2_ref, peb_ref, o_ref, carry, res_scr = refs[P:]
    p = pl.program_id(1)

    def store_head_major(res, out_ref, n_out):
        n = res.shape[0] // KG
        res_scr[0:n * KG, :] = res
        for kg in range(KG):
            rows = res_scr[pl.ds(kg, n, stride=KG), :]
            if n_out > n:
                rows = jnp.concatenate([rows, jnp.zeros((n_out - n, HEAD_DIM), F32)], axis=0)
            out_ref[0, kg] = rows.astype(out_ref.dtype)

    @pl.when(p == 0)
    def _():
        carry[...] = jnp.zeros_like(carry)

    def is_k(rows):
        return (lax.broadcasted_iota(jnp.int32, (rows, 1), 0) % KG) < KV_GROUPS

    def half_proj(xs):
        rows = xs[0].shape[0]
        lhs = jnp.concatenate([x.astype(BF16) for x in xs], axis=1)
        acc = jnp.dot(lhs, wc_ref[...], preferred_element_type=F32)
        ik = is_k(rows)
        a = jnp.where(ik, acc[:, 0:128], acc[:, 256:384])
        b = jnp.where(ik, acc[:, 128:256], acc[:, 384:512])
        return a, b

    def finish(a_prev, b):
        rows = b.shape[0]
        peb = jnp.concatenate([peb_ref[...]] * (rows // KG), axis=0)
        hid = jax.nn.gelu(a_prev + b + peb)
        o2 = jnp.dot(hid.astype(BF16), w2_ref[...], preferred_element_type=F32)
        return jnp.where(is_k(rows), o2[:, 0:128], o2[:, 128:256])

    xs = [jnp.concatenate([r[0, :, l].reshape(CMP_STRIDE // 2 * KG, HEAD_DIM) for r in page_refs], axis=0)
          for l in range(CMP_STRIDE)]
    a, b = half_proj(xs)
    a_prev = jnp.concatenate([carry[...], a[:-KG]], axis=0)
    store_head_major(finish(a_prev, b), o_ref, o_ref.shape[2])
    carry[...] = a[-KG:]

    if has_tail:
        @pl.when(p == pl.num_programs(1) - 1)
        def _():
            nct = tail_ref.shape[1]
            xt = [tail_ref[0, :, l].reshape(nct * KG, HEAD_DIM) for l in range(CMP_STRIDE)]
            at, bt = half_proj(xt)
            ap = jnp.concatenate([a[-KG:], at[:-KG]], axis=0)
            store_head_major(finish(ap, bt), ot_ref, ot_ref.shape[2])


def _cmp_summaries(pool, page_table, tail, wc, w2c, peb, *, P):
    ns, npg = page_table.shape
    cpp = PAGE_SIZE // CMP_STRIDE
    assert npg % P == 0
    has_tail = tail is not None
    blk = (1, cpp, CMP_STRIDE, KG, HEAD_DIM)
    in_specs = [pl.BlockSpec(blk, functools.partial(lambda s, p, pt, u: (pt[s * npg + p * P + u], 0, 0, 0, 0), u=u))
                for u in range(P)]
    args = [pool] * P
    if has_tail:
        nct = tail.shape[1]
        in_specs.append(pl.BlockSpec((1, nct, CMP_STRIDE, KG, HEAD_DIM), lambda s, p, pt: (s, 0, 0, 0, 0)))
        args.append(tail)
    in_specs += [
        pl.BlockSpec(wc.shape, lambda s, p, pt: (0, 0)),
        pl.BlockSpec(w2c.shape, lambda s, p, pt: (0, 0)),
        pl.BlockSpec(peb.shape, lambda s, p, pt: (0, 0)),
    ]
    args += [wc, w2c, peb]
    out_shape = [jax.ShapeDtypeStruct((ns, KG, npg * cpp, HEAD_DIM), BF16)]
    out_specs = [pl.BlockSpec((1, KG, P * cpp, HEAD_DIM), lambda s, p, pt: (s, 0, p, 0))]
    if has_tail:
        out_shape.append(jax.ShapeDtypeStruct((ns, KG, CMP_TAIL_SLOTS, HEAD_DIM), BF16))
        out_specs.append(pl.BlockSpec((1, KG, CMP_TAIL_SLOTS, HEAD_DIM), lambda s, p, pt: (s, 0, 0, 0)))
    return pl.pallas_call(
        functools.partial(_cmp_kernel, P=P, has_tail=has_tail),
        grid_spec=pltpu.PrefetchScalarGridSpec(
            num_scalar_prefetch=1, grid=(ns, npg // P), in_specs=in_specs, out_specs=out_specs,
            scratch_shapes=[pltpu.VMEM((KG, HEAD_DIM), F32), pltpu.VMEM((P * cpp * KG, HEAD_DIM), F32)]),
        out_shape=out_shape,
        compiler_params=_cparams(("parallel", "arbitrary")),
    )(page_table.reshape(-1), *args)


def _pack_cmp_weights(cmp_w1, cmp_w2):
    def cols(l):
        return jnp.concatenate([cmp_w1[0, l], cmp_w1[0, CMP_STRIDE + l], cmp_w1[1, l], cmp_w1[1, CMP_STRIDE + l]], axis=1)
    wc = jnp.concatenate([cols(l) for l in range(CMP_STRIDE)], axis=0).astype(BF16)
    w2c = jnp.concatenate([cmp_w2[0], cmp_w2[1]], axis=1).astype(BF16)
    return wc, w2c


def _overlap_matrix(n_slots, n_sb, width):
    i = jnp.arange(n_slots)[:, None] - 1
    j = jnp.arange(width)[None, :]
    ov = (i >= 0) & (j < n_sb) & (i * CMP_STRIDE < (j + 1) * SLC_BLOCK) & (i * CMP_STRIDE + CMP_BLOCK > j * SLC_BLOCK)
    return ov.astype(BF16)


def _masked_softmax(s, mask):
    s = jnp.where(mask, s, -jnp.inf)
    m = jnp.max(s, axis=-1, keepdims=True)
    m = jnp.where(m > -jnp.inf, m, 0.0)
    e = jnp.exp2(s - m)
    return e / jnp.maximum(jnp.sum(e, axis=-1, keepdims=True), 1e-30)


def _dot_nt(a, b):
    return lax.dot_general(a, b, (((1,), (1,)), ((), ())), preferred_element_type=F32)


def _attn_prompt_kernel(q_ref, kc_ref, vc_ref, ks_ref, vs_ref, kw_ref, vw_ref, gate_ref, ov_ref, e_ref, ge_ref,
                        o_ref, *, tq, n_sb, sel_step):
    i = pl.program_id(2)
    hpg = HEADS_PER_GROUP
    rows = hpg * tq
    s_len = ks_ref.shape[2]

    q = jnp.concatenate([q_ref[:, h * HEAD_DIM:(h + 1) * HEAD_DIM] for h in range(hpg)], axis=0)
    t0 = i * tq
    t_row = t0 + lax.broadcasted_iota(jnp.int32, (rows, 1), 0) % tq
    t_tok = t0 + lax.broadcasted_iota(jnp.int32, (tq, 1), 0)

    def compressed_branch():
        n_slots = kc_ref.shape[2]
        s_c = _dot_nt(q, kc_ref[0, 0])
        slot = lax.broadcasted_iota(jnp.int32, (1, n_slots), 1)
        mask_c = (slot >= 1) & ((slot - 1) * CMP_STRIDE + CMP_BLOCK - 1 <= t_row)
        p_c = _masked_softmax(s_c, mask_c)
        o_c = jnp.dot(p_c.astype(BF16), vc_ref[0, 0], preferred_element_type=F32)
        p_grp = p_c[0:tq]
        for h in range(1, hpg):
            p_grp = p_grp + p_c[h * tq:(h + 1) * tq]
        return o_c, p_grp

    def select_blocks(p_grp):
        score = _dot_exact01(p_grp, ov_ref[...])
        jl = lax.broadcasted_iota(jnp.int32, (1, LANES), 1)
        cur = t_tok // SLC_BLOCK
        valid = jl <= cur
        forced = (jl == 0) | (jl == cur) | (jl == cur - 1)
        score = jnp.where(forced, FORCE_SCORE, score)
        score = jnp.where(valid, score, -jnp.inf)
        rank = jnp.zeros((tq, LANES), F32)
        for j2 in range(n_sb):
            col = score[:, j2:j2 + 1]
            later = jnp.where(jl > j2, 1.0, 0.0)
            rank = rank + jnp.where(col > score, 1.0, 0.0) + jnp.where(col == score, later, 0.0)
        return jnp.where((rank < N_SELECT) & valid, 1.0, 0.0).astype(BF16)

    def attend(k, v, keep):
        bias = jnp.where(keep, 0.0, NEG_BIG)
        s = _dot_nt(q, k)
        es = []
        for h in range(hpg):
            sh = s[h * tq:(h + 1) * tq] + bias
            es.append(jnp.exp2(sh - jnp.max(sh, axis=1, keepdims=True)).astype(BF16))
        ones = (lax.broadcasted_iota(jnp.int32, (v.shape[0], LANES), 1) == 0).astype(BF16)
        oa = jnp.dot(jnp.concatenate(es, axis=0), jnp.concatenate([v, ones], axis=1), preferred_element_type=F32)
        return oa[:, :HEAD_DIM] / oa[:, HEAD_DIM:HEAD_DIM + 1]

    def window_branch():
        band = WINDOW + tq
        start = pl.multiple_of(jnp.maximum(t0 - WINDOW, 0), tq)
        kpos = start + lax.broadcasted_iota(jnp.int32, (1, band), 1)
        return attend(kw_ref[0, 0, pl.ds(start, band), :], vw_ref[0, 0, pl.ds(start, band), :],
                      (kpos <= t_tok) & (kpos > t_tok - WINDOW))

    def combine(o_c, o_s, o_w):
        sig = jax.nn.sigmoid(gate_ref[...])
        g_c, g_s, g_w = (_dot_exact01(sig, ge_ref[k, 0]) for k in range(3))
        for h in range(hpg):
            hs = slice(h * HEAD_DIM, (h + 1) * HEAD_DIM)
            rs = slice(h * tq, (h + 1) * tq)
            o_ref[:, hs] = (g_c[:, hs] * o_c[rs] + g_s[:, hs] * o_s[rs] + g_w[:, hs] * o_w[rs]).astype(o_ref.dtype)

    for var in range(s_len // sel_step):
        width = (var + 1) * sel_step

        @pl.when((t0 + tq - 1) // sel_step == var)
        def _(width=width):
            o_c, p_grp = compressed_branch()
            kpos = lax.broadcasted_iota(jnp.int32, (1, width), 1)
            keep = kpos <= t_tok
            if width > N_SELECT * SLC_BLOCK:
                selx = jnp.dot(select_blocks(p_grp), e_ref[:, 0:width], preferred_element_type=F32)
                keep = keep & (selx > 0.5)
            o_s = attend(ks_ref[0, 0, 0:width, :], vs_ref[0, 0, 0:width, :], keep)
            combine(o_c, o_s, window_branch())


def _gate_expand(width_heads):
    c = jnp.arange(LANES)[None, :, None]
    k = jnp.arange(3)[:, None, None]
    hh = (jnp.arange(width_heads * HEAD_DIM) // HEAD_DIM)[None, None, :]
    return (c == k * N_HEADS + hh).astype(BF16)


def _attn_prompt(q, kvc_slots, kvs, kvw, small, *, nb, s_len, tq=128, sel_step=256):
    nq = s_len // tq
    n_slots = kvc_slots.shape[2]
    n_sb = s_len // SLC_BLOCK
    assert n_sb <= LANES and n_slots <= LANES and s_len % sel_step == 0 and sel_step % tq == 0
    assert s_len >= WINDOW + tq and WINDOW % tq == 0
    hd = HEAD_DIM
    gw = HEADS_PER_GROUP * hd
    ov = _overlap_matrix(n_slots, n_sb, LANES)
    e = (jnp.arange(LANES)[:, None] == (jnp.arange(s_len) // SLC_BLOCK)[None, :]).astype(BF16)
    ge = _gate_expand(N_HEADS).reshape(3, LANES, KV_GROUPS, gw).transpose(0, 2, 1, 3)
    kspec = lambda kv: pl.BlockSpec((1, 1, s_len, hd), lambda b, g, i: (b, kv * KV_GROUPS + g, 0, 0))
    cspec = lambda kv: pl.BlockSpec((1, 1, n_slots, hd), lambda b, g, i: (b, kv * KV_GROUPS + g, 0, 0))
    return pl.pallas_call(
        functools.partial(_attn_prompt_kernel, tq=tq, n_sb=n_sb, sel_step=sel_step),
        grid=(nb, KV_GROUPS, nq),
        in_specs=[
            pl.BlockSpec((tq, gw), lambda b, g, i: (b * nq + i, g)),
            cspec(0), cspec(1), kspec(0), kspec(1), kspec(0), kspec(1),
            pl.BlockSpec((tq, LANES), lambda b, g, i: (b * nq + i, 0)),
            pl.BlockSpec(ov.shape, lambda b, g, i: (0, 0)),
            pl.BlockSpec(e.shape, lambda b, g, i: (0, 0)),
            pl.BlockSpec((3, 1, LANES, gw), lambda b, g, i: (0, g, 0, 0)),
        ],
        out_specs=pl.BlockSpec((tq, gw), lambda b, g, i: (b * nq + i, g)),
        out_shape=jax.ShapeDtypeStruct((nb * s_len, N_HEADS * hd), BF16),
        compiler_params=_cparams(("parallel", "parallel", "parallel")),
    )(q, kvc_slots, kvc_slots, kvs, kvs, kvw, kvw, small, ov, e, ge)


T_PAD = 8


def _attn_sample_a_kernel(q_ref, kcm_ref, kct_ref, wc_ref, wn_ref, ov_ref, oc_ref, ow_ref, idx_ref,
                          *, past, t_new, n_blocks, n_sb, wb):
    hpg = HEADS_PER_GROUP
    rows = hpg * T_PAD
    t_pos = past + lax.broadcasted_iota(jnp.int32, (rows, 1), 0) % T_PAD
    tp8 = past + lax.broadcasted_iota(jnp.int32, (T_PAD, 1), 0)
    n_slots = kcm_ref.shape[2] + kct_ref.shape[2]
    slot = lax.broadcasted_iota(jnp.int32, (1, n_slots), 1)
    mask_c = (slot >= 1) & (slot <= n_blocks) & ((slot - 1) * CMP_STRIDE + CMP_BLOCK - 1 <= t_pos)
    width = ov_ref.shape[1]
    jl = lax.broadcasted_iota(jnp.int32, (1, width), 1)
    jf = jl.astype(F32)
    cur = tp8 // SLC_BLOCK
    valid = (jl <= cur) & (jl < n_sb)
    forced = (jl == 0) | (jl == cur) | (jl == cur - 1)
    lane = lax.broadcasted_iota(jnp.int32, (1, LANES), 1)
    wj = lax.broadcasted_iota(jnp.int32, (1, wb + T_PAD), 1)
    kp = past - wb + wj
    mask_w = (kp <= t_pos) & (kp > t_pos - WINDOW) & (kp >= 0) & (wj < wb + t_new)

    for g in range(KV_GROUPS):
        q = q_ref[0, g].astype(BF16)
        kc = jnp.concatenate([kcm_ref[0, g], kct_ref[0, g]], axis=0)
        vc = jnp.concatenate([kcm_ref[0, KV_GROUPS + g], kct_ref[0, KV_GROUPS + g]], axis=0)
        p_c = _masked_softmax(_dot_nt(q, kc), mask_c)
        oc_ref[0, g] = jnp.dot(p_c.astype(BF16), vc, preferred_element_type=F32)
        p_grp = p_c[0:T_PAD]
        for h in range(1, hpg):
            p_grp = p_grp + p_c[h * T_PAD:(h + 1) * T_PAD]
        score = _dot_exact01(p_grp, ov_ref[...])
        score = jnp.where(forced, FORCE_SCORE, score)
        score = jnp.where(valid, score, -jnp.inf)
        idx = jnp.zeros((T_PAD, LANES), jnp.int32)
        for k in range(N_SELECT):
            mx = jnp.max(score, axis=1, keepdims=True)
            am = jnp.min(jnp.where(score == mx, jf, 1e9), axis=1, keepdims=True)
            pick = jnp.where(mx > -jnp.inf, am, -1.0).astype(jnp.int32)
            idx = jnp.where(lane == k, pick, idx)
            score = jnp.where(jf == am, -jnp.inf, score)
        idx_ref[0, g] = idx

        kw = jnp.concatenate([wc_ref[0, :, 0, g, :], wn_ref[0, :, 0, g, :]], axis=0).astype(BF16)
        vw = jnp.concatenate([wc_ref[0, :, 1, g, :], wn_ref[0, :, 1, g, :]], axis=0).astype(BF16)
        p_w = _masked_softmax(_dot_nt(q, kw), mask_w)
        ow_ref[0, g] = jnp.dot(p_w.astype(BF16), vw, preferred_element_type=F32)


def _attn_sample_a(q4, slots_main, slots_tail, win_cache, win_new, *, past, t_new, n_blocks, n_sb):
    ns = q4.shape[0]
    wb = win_cache.shape[1]
    n_slots = slots_main.shape[2] + slots_tail.shape[2]
    width = -(-n_sb // LANES) * LANES
    ov = _overlap_matrix(n_slots, n_sb, width)
    rows = HEADS_PER_GROUP * T_PAD
    full = lambda a: pl.BlockSpec((1,) + a.shape[1:], lambda s: (s,) + (0,) * (a.ndim - 1))
    o_sds = jax.ShapeDtypeStruct((ns, KV_GROUPS, rows, HEAD_DIM), F32)
    o_spec = pl.BlockSpec((1, KV_GROUPS, rows, HEAD_DIM), lambda s: (s, 0, 0, 0))
    return pl.pallas_call(
        functools.partial(_attn_sample_a_kernel, past=past, t_new=t_new, n_blocks=n_blocks, n_sb=n_sb, wb=wb),
        grid=(ns,),
        in_specs=[full(q4), full(slots_main), full(slots_tail), full(win_cache), full(win_new),
                  pl.BlockSpec(ov.shape, lambda s: (0, 0))],
        out_specs=[o_spec, o_spec, pl.BlockSpec((1, KV_GROUPS, T_PAD, LANES), lambda s: (s, 0, 0, 0))],
        out_shape=[o_sds, o_sds, jax.ShapeDtypeStruct((ns, KV_GROUPS, T_PAD, LANES), jnp.int32)],
        compiler_params=_cparams(("parallel",)),
    )(q4, slots_main, slots_tail, win_cache, win_new, ov)


def _attn_sample_b_kernel(idx_ref, pt_ref, q_ref, pool_ref, tail_ref, o_ref, buf, sem, *, past, nb_past, t_new, npg):
    n = N_SELECT
    G = KV_GROUPS
    bpp = PAGE_SIZE // SLC_BLOCK
    nblk = t_new * n
    s, g = pl.program_id(0), pl.program_id(1)
    step = s * G + g
    nsteps = pl.num_programs(0) * G
    slot = step % 2

    def start_copies(step_, slot_):
        s_, g_ = step_ // G, step_ % G

        def body(j, carry):
            jj = idx_ref[step_ * nblk + j]
            jp = jnp.clip(jj, 0, nb_past - 1)
            page = pt_ref[s_ * npg + jp // bpp]
            r0 = pl.multiple_of((jp % bpp) * SLC_BLOCK, SLC_BLOCK)
            for kv in range(2):
                @pl.when(jj < nb_past)
                def _(kv=kv):
                    pltpu.make_async_copy(pool_ref.at[page, pl.ds(r0, SLC_BLOCK), kv, g_], buf.at[slot_, kv, j],
                                          sem.at[slot_]).start()

                @pl.when(jj >= nb_past)
                def _(kv=kv):
                    pltpu.make_async_copy(tail_ref.at[s_, :, kv, g_], buf.at[slot_, kv, j], sem.at[slot_]).start()
            return carry

        lax.fori_loop(0, nblk, body, 0)

    def wait_copies(slot_):
        def body(j, carry):
            for kv in range(2):
                pltpu.make_async_copy(tail_ref.at[0, :, kv, 0], buf.at[slot_, kv, j], sem.at[slot_]).wait()
            return carry

        lax.fori_loop(0, nblk, body, 0)

    @pl.when(step == 0)
    def _():
        start_copies(step, slot)

    @pl.when(step + 1 < nsteps)
    def _():
        start_copies(step + 1, 1 - slot)

    wait_copies(slot)

    q = q_ref[0, 0].astype(BF16)
    rows = q.shape[0]
    t8 = lax.broadcasted_iota(jnp.int32, (rows, 1), 0) % T_PAD
    jl = lax.broadcasted_iota(jnp.int32, (1, n * SLC_BLOCK), 1)
    kslot = jl // SLC_BLOCK
    out = jnp.zeros((rows, HEAD_DIM), F32)
    for t in range(t_new):
        start = jnp.zeros((1, n * SLC_BLOCK), jnp.int32)
        ok = jnp.zeros((1, n * SLC_BLOCK), jnp.int32)
        for k in range(n):
            jj = idx_ref[step * nblk + t * n + k]
            start = jnp.where(kslot == k, jj * SLC_BLOCK, start)
            ok = jnp.where(kslot == k, (jj >= 0).astype(jnp.int32), ok)
        kpos = start + jl % SLC_BLOCK
        mask = (ok > 0) & (kpos <= past + t)
        kk = buf[slot, 0, t * n:(t + 1) * n].reshape(n * SLC_BLOCK, HEAD_DIM).astype(BF16)
        vv = buf[slot, 1, t * n:(t + 1) * n].reshape(n * SLC_BLOCK, HEAD_DIM).astype(BF16)
        p = _masked_softmax(_dot_nt(q, kk), mask)
        o_t = jnp.dot(p.astype(BF16), vv, preferred_element_type=F32)
        out = jnp.where(t8 == t, o_t, out)
    o_ref[0, 0] = out


def _attn_sample_b(idx_flat, pt_flat, q4, tail, pool, *, past, npg, t_new):
    ns = q4.shape[0]
    G = KV_GROUPS
    rows = q4.shape[2]
    assert tail.shape[1] == SLC_BLOCK
    qspec = pl.BlockSpec((1, 1, rows, HEAD_DIM), lambda s, g, idx, pt: (s, g, 0, 0))
    return pl.pallas_call(
        functools.partial(_attn_sample_b_kernel, past=past, nb_past=past // SLC_BLOCK, t_new=t_new, npg=npg),
        grid_spec=pltpu.PrefetchScalarGridSpec(
            num_scalar_prefetch=2, grid=(ns, G),
            in_specs=[qspec, pl.BlockSpec(memory_space=pl.ANY), pl.BlockSpec(memory_space=pl.ANY)],
            out_specs=qspec,
            scratch_shapes=[pltpu.VMEM((2, 2, t_new * N_SELECT, SLC_BLOCK, HEAD_DIM), F32),
                            pltpu.SemaphoreType.DMA((2,))]),
        out_shape=jax.ShapeDtypeStruct((ns, G, rows, HEAD_DIM), F32),
        compiler_params=_cparams(("arbitrary", "arbitrary")),
    )(idx_flat, pt_flat, q4, pool, tail)


def _nsa_combine_kernel(oc_ref, os_ref, ow_ref, gate_ref, ge_ref, o_ref):
    sig = jax.nn.sigmoid(gate_ref[...])
    o_ref[...] = (_dot_exact01(sig, ge_ref[0]) * oc_ref[...] + _dot_exact01(sig, ge_ref[1]) * os_ref[...]
                  + _dot_exact01(sig, ge_ref[2]) * ow_ref[...])


def _nsa_combine(oc, os_, ow, small):
    ge = _gate_expand(N_HEADS)
    return pl.pallas_call(
        _nsa_combine_kernel,
        out_shape=jax.ShapeDtypeStruct(oc.shape, F32),
        compiler_params=pltpu.CompilerParams(vmem_limit_bytes=VMEM_LIMIT),
    )(oc, os_, ow, small, ge)


def _repack_kernel(a_ref, b_ref, o_ref, *, shift):
    if shift == 0:
        o_ref[...] = a_ref[...].astype(o_ref.dtype)
        return
    tn = a_ref.shape[1]
    cat = jnp.concatenate([a_ref[...], b_ref[...]], axis=1)
    o_ref[...] = pltpu.roll(cat, 2 * tn - shift, 1)[:, :tn].astype(o_ref.dtype)


def _repack(w, *, col_start, ncols, tk=512, tn=512):
    k, n = w.shape
    base, shift = divmod(col_start, tn)
    last = pl.cdiv(n, tn) - 1
    assert k % tk == 0 and ncols % tn == 0 and base + ncols // tn - 1 + (shift > 0) <= last
    return pl.pallas_call(
        functools.partial(_repack_kernel, shift=shift),
        grid=(k // tk, ncols // tn),
        in_specs=[pl.BlockSpec((tk, tn), lambda i, j: (i, base + j)),
                  pl.BlockSpec((tk, tn), lambda i, j: (i, jnp.minimum(base + j + 1, last)))],
        out_specs=pl.BlockSpec((tk, tn), lambda i, j: (i, j)),
        out_shape=jax.ShapeDtypeStruct((k, ncols), BF16),
        compiler_params=_cparams(("parallel", "parallel")),
    )(w, w)


def _small_cols_kernel(a_ref, b_ref, o_ref, *, n_a, n_b):
    lane = lax.broadcasted_iota(jnp.int32, (1, LANES), 1)
    o_ref[...] = jnp.where(lane < n_a, a_ref[...], jnp.where(lane < n_a + n_b, b_ref[...], 0.0))


def _small_cols(w, *, col_a, n_a, col_b, n_b):
    k = w.shape[0]
    assert col_a % LANES == 0 and col_b % LANES == n_a and n_a + n_b <= LANES
    return pl.pallas_call(
        functools.partial(_small_cols_kernel, n_a=n_a, n_b=n_b),
        grid=(1,),
        in_specs=[pl.BlockSpec((k, LANES), lambda i: (0, col_a // LANES)),
                  pl.BlockSpec((k, LANES), lambda i: (0, col_b // LANES))],
        out_specs=pl.BlockSpec((k, LANES), lambda i: (0, 0)),
        out_shape=jax.ShapeDtypeStruct((k, LANES), F32),
        compiler_params=_cparams(("arbitrary",)),
    )(w, w)


def _pack_in_proj(w_in, b_in):
    sizes = (N_HEADS * HEAD_DIM, KV_WIDTH, KV_WIDTH, KV_WIDTH, N_GATE, 2 * M_INNER, M_INNER, M_INNER, 2 * M_HEADS,
             2 * D_MODEL)
    offs = [0]
    for sz in sizes:
        offs.append(offs[-1] + sz)
    o_attg, o_mqk, o_mif, o_merge, end = offs[4], offs[5], offs[8], offs[9], offs[10]
    pad = LANES - N_GATE - 2 * M_HEADS

    b = b_in[None, :]
    b_slabs = [b[:, :o_attg], b[:, o_mqk:o_mif], b[:, o_merge:end]]
    b_small = jnp.concatenate([b[:, o_attg:o_mqk], b[:, o_mif:o_merge], jnp.zeros((1, pad), F32)], axis=-1)
    w_slabs = [_repack(w_in, col_start=0, ncols=o_attg), _repack(w_in, col_start=o_mqk, ncols=o_mif - o_mqk),
               _repack(w_in, col_start=o_merge, ncols=end - o_merge)]
    w_small = _small_cols(w_in, col_a=o_attg, n_a=o_mqk - o_attg, col_b=o_mif, n_b=o_merge - o_mif)
    return w_slabs, b_slabs, w_small, b_small


def _project(x, pos, g_pre, packed, *, tm, seq_len=None):
    (w_att, w_rest, w_merge), (b_att, b_rest, b_merge), w_small, b_small = packed
    tabs = _rope_tables(pos)
    h = _rms_norm_bf16(x, g_pre, tm=min(tm, 512))
    q = _proj(h, w_att, b_att, col0=OFF_Q, ncols=N_HEADS * HEAD_DIM, out_dtype=BF16, tm=tm, tn=512, rope=tabs,
              out_scale=ATT_SCALE * LOG2E)
    kv = [_kv_proj(h, w_att, b_att, tabs, col0=off, tm=tm, seq_len=None if off == OFF_KVC else seq_len)
          for off in (OFF_KVC, OFF_KVS, OFF_KVW)]
    rest = _proj(h, w_rest, b_rest, col0=0, ncols=REST_WIDTH, tm=tm, tn=1024, out_dtype=F32)
    merge = _proj(h, w_merge, b_merge, col0=0, ncols=2 * D_MODEL, tm=tm, tn=1024, out_dtype=BF16)
    small = _small_proj(x, g_pre, w_small, b_small, tm=min(tm, 512))
    return q, kv, rest, merge, small


def _gate_layouts(small, nb, t, t_pad):
    mif = small[:, N_GATE:N_GATE + 2 * M_HEADS].reshape(nb, t, 2, M_HEADS)
    if t_pad > t:
        fill = jnp.broadcast_to(jnp.array([NEG_BIG, 1e4], F32)[None, None, :, None], (nb, t_pad - t, 2, M_HEADS))
        mif = jnp.concatenate([mif, fill], axis=1)
    return jnp.transpose(mif, (0, 3, 1, 2)), jnp.transpose(mif, (0, 3, 2, 1))


def _pad_rows(a, n):
    return jnp.concatenate([a, jnp.zeros((a.shape[0], n - a.shape[1]) + a.shape[2:], a.dtype)], axis=1)


def _nsa_sample(q_s, kvc_s, kvs_s, kvw_s, small_s, cache_kv_cmp, cache_kv_slc, cache_win_kv, page_table,
                wc, w2c, peb, *, DB, T):
    npg = page_table.shape[1]
    past = npg * PAGE_SIZE
    n_phys = cache_kv_cmp.shape[0]
    cpp = PAGE_SIZE // CMP_STRIDE
    t_blk = -(-T // SLC_BLOCK) * SLC_BLOCK
    assert t_blk == SLC_BLOCK
    new_rows = lambda a, n: _pad_rows(a.reshape(DB, T, 2, KV_GROUPS, HEAD_DIM), n)
    cmp_tail = new_rows(kvc_s, t_blk).reshape(DB, t_blk // CMP_STRIDE, CMP_STRIDE, KG, HEAD_DIM)
    slots_s, slots_t = _cmp_summaries(cache_kv_cmp.reshape(n_phys, cpp, CMP_STRIDE, KG, HEAD_DIM), page_table,
                                      cmp_tail, wc, w2c, peb, P=16)
    n_blocks = (past + t_blk) // CMP_STRIDE - 1
    nb_past = past // SLC_BLOCK
    n_sb = nb_past + t_blk // SLC_BLOCK
    q6 = q_s.astype(F32).reshape(DB, T, KV_GROUPS, HEADS_PER_GROUP, HEAD_DIM)
    q4 = _pad_rows(jnp.transpose(q6, (0, 2, 3, 1, 4)).reshape(DB * KV_GROUPS * HEADS_PER_GROUP, T, HEAD_DIM), T_PAD)
    q4 = q4.reshape(DB, KV_GROUPS, HEADS_PER_GROUP * T_PAD, HEAD_DIM)
    oc, ow, idx = _attn_sample_a(q4, slots_s, slots_t, cache_win_kv, new_rows(kvw_s, T_PAD),
                                 past=past, t_new=T, n_blocks=n_blocks, n_sb=n_sb)
    idx_flat = idx[:, :, :T, :N_SELECT].reshape(-1)
    os4 = _attn_sample_b(idx_flat, page_table.reshape(-1), q4, new_rows(kvs_s, SLC_BLOCK), cache_kv_slc,
                         past=past, npg=npg, t_new=T)
    tok_major = lambda o: jnp.transpose(o.reshape(DB, KV_GROUPS, HEADS_PER_GROUP, T_PAD, HEAD_DIM)[:, :, :, :T],
                                        (0, 3, 1, 2, 4)).reshape(DB * T, N_HEADS * HEAD_DIM)
    return _nsa_combine(tok_major(oc), tok_major(os4), tok_major(ow), small_s)


def kernel(x_prompt, x_sample, cache_kv_cmp, cache_kv_slc, cache_win_kv, state_mlstm_c, state_mlstm_n, state_mlstm_m, state_mlstm_conv, page_table, g_mix_pre, w_in, b_in, cmp_pe, cmp_w1, cmp_w2, conv_w, conv_b, mlstm_norm_w, w_out, g_mix_post, g_mlp_pre, w_up, w_down, g_mlp_post):
    B, S, D = x_prompt.shape
    DB, T, _ = x_sample.shape
    npg = page_table.shape[1]
    past = npg * PAGE_SIZE
    n_phys = cache_kv_cmp.shape[0]
    cpp = PAGE_SIZE // CMP_STRIDE
    kv5 = (2, KV_GROUPS, HEAD_DIM)

    packed = _pack_in_proj(w_in, b_in)
    w_out_b, w_up_b, w_down_b = w_out.astype(BF16), w_up.astype(BF16), w_down.astype(BF16)
    row = lambda v: v[None, :]
    peb = _pe_bias(cmp_pe, cmp_w1)
    wc, w2c = _pack_cmp_weights(cmp_w1, cmp_w2)

    def tail(merge, att, mh, x, tm, tm_mlp):
        x1 = _merge_out(merge, att, mh, x, w_out_b, row(g_mix_post), tm=tm)
        return _mlp(x1, row(g_mlp_pre), w_up_b, w_down_b, row(g_mlp_post), tm=tm_mlp, tf=1024)

    xp = x_prompt.reshape(B * S, D)
    q_p, kv_p, rest_p, merge_p, small_p = _project(xp, jnp.arange(S), row(g_mix_pre), packed, tm=1024, seq_len=S)
    (kvc_p,), (kvs_p, kvs_hm), (kvw_p, kvw_hm) = kv_p
    pt_p = jnp.arange(B * S // PAGE_SIZE, dtype=jnp.int32).reshape(B, S // PAGE_SIZE)
    slots_p, = _cmp_summaries(kvc_p.reshape(B * S // PAGE_SIZE, cpp, CMP_STRIDE, KG, HEAD_DIM), pt_p, None,
                              wc, w2c, peb, P=16)
    att_p = _attn_prompt(q_p, slots_p, kvs_hm, kvw_hm, small_p, nb=B, s_len=S)
    gcol, grow = _gate_layouts(small_p, B, S, S)
    mh_p, c_p, n_p, m_p = _mlstm(rest_p, gcol, grow, conv_w, row(conv_b), jnp.zeros((B, 8, 2 * M_INNER), F32),
                                 row(mlstm_norm_w), None, nb=B, t=S, L=256)
    y_p = tail(merge_p, att_p, mh_p, xp, 256, 512)
    wlen = min(WINDOW, S)
    conv_p = rest_p.reshape(B, S, REST_WIDTH)[:, S - (CONV_W - 1):, R_MQK:R_MQK + 2 * M_INNER]

    xs = x_sample.reshape(DB * T, D)
    pos_s = jnp.tile(past + jnp.arange(T), DB)
    q_s, ((kvc_s,), (kvs_s,), (kvw_s,)), rest_s, merge_s, small_s = _project(xs, pos_s, row(g_mix_pre), packed, tm=DB * T)
    att_s = _nsa_sample(q_s, kvc_s, kvs_s, kvw_s, small_s, cache_kv_cmp, cache_kv_slc, cache_win_kv, page_table,
                        wc, w2c, peb, DB=DB, T=T)

    LS = 16
    gcol_s, grow_s = _gate_layouts(small_s, DB, T, LS)
    rest_pad = _pad_rows(rest_s.reshape(DB, T, REST_WIDTH), LS).reshape(DB * LS, REST_WIDTH)
    conv0 = jnp.concatenate([jnp.zeros((DB, 8 - (CONV_W - 1), 2 * M_INNER), F32), state_mlstm_conv], axis=1)
    state = (state_mlstm_c, state_mlstm_n[:, :, None, :], state_mlstm_m[:, :, None, None])
    mh_pad, c_s, n_s, m_s = _mlstm(rest_pad, gcol_s, grow_s, conv_w, row(conv_b), conv0, row(mlstm_norm_w), state,
                                   nb=DB, t=LS, L=LS)
    mh_s = mh_pad.reshape(DB, LS, M_INNER)[:, :T].reshape(DB * T, M_INNER)
    y_s = tail(merge_s, att_s, mh_s, xs, DB * T, DB * T)
    win_s = jnp.concatenate([cache_win_kv[:, T:], kvw_s.reshape((DB, T) + kv5).astype(cache_win_kv.dtype)], axis=1)
    mqk_s = rest_s.reshape(DB, T, REST_WIDTH)[:, :, R_MQK:R_MQK + 2 * M_INNER]
    conv_s = jnp.concatenate([state_mlstm_conv, mqk_s], axis=1)[:, T:]

    return (y_p.reshape(B, S, D), y_s.reshape(DB, T, D),
            kvc_p.reshape((B, S) + kv5), kvc_s.reshape((DB, T) + kv5),
            kvs_p.reshape((B, S) + kv5), kvs_s.reshape((DB, T) + kv5),
            kvw_p.reshape((B, S) + kv5)[:, S - wlen:], win_s,
            c_p, c_s, n_p[:, :, 0], n_s[:, :, 0], m_p[:, :, 0, 0], m_s[:, :, 0, 0], conv_p, conv_s)
```

```python
import functools

import jax
import jax.numpy as jnp
from jax import lax
from jax.experimental import pallas as pl
from jax.experimental.pallas import tpu as pltpu

F32 = jnp.float32
BF16 = jnp.bfloat16

D_MODEL = 2048
N_HEADS = 16
HEAD_DIM = 128
KV_GROUPS = 4
HEADS_PER_GROUP = N_HEADS // KV_GROUPS
ROPE_DIM = HEAD_DIM // 4
ROPE_HALF = ROPE_DIM // 2
ROPE_THETA = 500000.0
CMP_STRIDE = 16
CMP_BLOCK = 2 * CMP_STRIDE
SLC_BLOCK = 64
N_SELECT = 16
WINDOW = 512
ATT_SCALE = HEAD_DIM ** -0.5
LOG2E = 1.4426950408889634
FORCE_SCORE = 1e9
M_HEADS = 4
M_INNER = D_MODEL
M_HEAD_DIM = M_INNER // M_HEADS
CONV_W = 4
NEG_BIG = -1e30
D_FF = 4 * D_MODEL
EPS = 1e-6
PAGE_SIZE = 128
KV_WIDTH = 2 * KV_GROUPS * HEAD_DIM
KG = 2 * KV_GROUPS

LANES = 128
VMEM_LIMIT = 56 * 1024 * 1024

OFF_Q = 0
OFF_KVC = OFF_Q + N_HEADS * HEAD_DIM
OFF_KVS = OFF_KVC + KV_WIDTH
OFF_KVW = OFF_KVS + KV_WIDTH
ATT_WIDTH = OFF_KVW + KV_WIDTH
REST_WIDTH = 2 * M_INNER + M_INNER + M_INNER
N_GATE = 3 * N_HEADS
R_MQK, R_MV, R_MO = 0, 2 * M_INNER, 3 * M_INNER


def _cparams(sem):
    return pltpu.CompilerParams(dimension_semantics=sem, vmem_limit_bytes=VMEM_LIMIT)


def _split3(x):
    hi = x.astype(BF16)
    r1 = x - hi.astype(F32)
    mid = r1.astype(BF16)
    lo = (r1 - mid.astype(F32)).astype(BF16)
    return hi, mid, lo


def _dot_exact01(x, e):
    hi, mid, lo = _split3(x)
    d = lambda a: jnp.dot(a, e, preferred_element_type=F32)
    return d(hi) + d(mid) + d(lo)


def _rms_norm_kernel(x_ref, g_ref, o_ref):
    xf = x_ref[...]
    ms = jnp.mean(xf * xf, axis=-1, keepdims=True)
    o_ref[...] = (xf * lax.rsqrt(ms + EPS) * g_ref[...]).astype(o_ref.dtype)


def _rms_norm_bf16(x, g, *, tm):
    m, d = x.shape
    return pl.pallas_call(
        _rms_norm_kernel,
        grid=(m // tm,),
        in_specs=[pl.BlockSpec((tm, d), lambda i: (i, 0)), pl.BlockSpec((1, d), lambda i: (0, 0))],
        out_specs=pl.BlockSpec((tm, d), lambda i: (i, 0)),
        out_shape=jax.ShapeDtypeStruct((m, d), BF16),
        compiler_params=_cparams(("parallel",)),
    )(x, g)


def _kv_proj_kernel(h_ref, w_ref, b_ref, c_ref, s1_ref, s2_ref, *refs, head_major):
    if head_major:
        o_ref, hm_ref = refs
    else:
        o_ref, = refs
    j = pl.program_id(1)
    acc = jnp.dot(h_ref[...], w_ref[...], preferred_element_type=F32) + b_ref[...]

    def emit(rope):
        c, s1, s2 = c_ref[...], s1_ref[...], s2_ref[...]
        for g in range(KV_GROUPS):
            a = acc[:, g * LANES:(g + 1) * LANES]
            if rope:
                a = a * c + pltpu.roll(a, ROPE_HALF, 1) * s1 + pltpu.roll(a, LANES - ROPE_HALF, 1) * s2
            o_ref[:, 0, g, :] = a
            if head_major:
                hm_ref[0, g] = a.astype(BF16)

    pl.when(j == 0)(functools.partial(emit, True))
    pl.when(j != 0)(functools.partial(emit, False))


def _kv_proj(h, w, b, rope, *, col0, tm, seq_len=None):
    m, d = h.shape
    tn = KV_GROUPS * HEAD_DIM
    jb = col0 // tn
    period = rope[0].shape[0] // tm
    assert m % tm == 0 and col0 % tn == 0 and rope[0].shape[0] % tm == 0
    in_specs = [
        pl.BlockSpec((tm, d), lambda i, j: (i, 0)),
        pl.BlockSpec((d, tn), lambda i, j: (0, j + jb)),
        pl.BlockSpec((1, tn), lambda i, j: (0, j + jb)),
    ] + [pl.BlockSpec((tm, LANES), lambda i, j: (i % period, 0))] * 3
    out_shape = [jax.ShapeDtypeStruct((m, 2, KV_GROUPS, HEAD_DIM), F32)]
    out_specs = [pl.BlockSpec((tm, 1, KV_GROUPS, HEAD_DIM), lambda i, j: (i, j, 0, 0))]
    if seq_len is not None:
        per = seq_len // tm
        assert seq_len % tm == 0
        out_shape.append(jax.ShapeDtypeStruct((m // seq_len, KG, seq_len, HEAD_DIM), BF16))
        out_specs.append(pl.BlockSpec((1, KV_GROUPS, tm, HEAD_DIM), lambda i, j: (i // per, j, i % per, 0)))
    return pl.pallas_call(
        functools.partial(_kv_proj_kernel, head_major=seq_len is not None),
        grid=(m // tm, 2),
        in_specs=in_specs,
        out_specs=out_specs,
        out_shape=out_shape,
        compiler_params=_cparams(("parallel", "arbitrary")),
    )(h, w, b, *rope)


def _proj_kernel(h_ref, w_ref, b_ref, *refs, tn, rope, out_scale):
    if rope:
        c_ref, s1_ref, s2_ref, o_ref = refs
    else:
        o_ref, = refs
    acc = jnp.dot(h_ref[...], w_ref[...], preferred_element_type=F32) + b_ref[...]
    if not rope:
        o_ref[...] = (acc * out_scale if out_scale != 1.0 else acc).astype(o_ref.dtype)
        return
    c, s1, s2 = c_ref[...], s1_ref[...], s2_ref[...]
    for hh in range(tn // LANES):
        a = acc[:, hh * LANES:(hh + 1) * LANES]
        r = a * c + pltpu.roll(a, ROPE_HALF, 1) * s1 + pltpu.roll(a, LANES - ROPE_HALF, 1) * s2
        o_ref[:, hh * LANES:(hh + 1) * LANES] = (r * out_scale if out_scale != 1.0 else r).astype(o_ref.dtype)


def _proj(h, w, b, *, col0, ncols, tm, tn, out_dtype, rope=None, out_scale=1.0):
    m, d = h.shape
    assert m % tm == 0 and ncols % tn == 0 and col0 % tn == 0
    jb = col0 // tn
    in_specs = [
        pl.BlockSpec((tm, d), lambda i, j: (i, 0)),
        pl.BlockSpec((d, tn), lambda i, j: (0, j + jb)),
        pl.BlockSpec((1, tn), lambda i, j: (0, j + jb)),
    ]
    args = [h, w, b]
    if rope is not None:
        period = rope[0].shape[0] // tm
        assert rope[0].shape[0] % tm == 0
        for tab in rope:
            in_specs.append(pl.BlockSpec((tm, LANES), lambda i, j: (i % period, 0)))
            args.append(tab)
    return pl.pallas_call(
        functools.partial(_proj_kernel, tn=tn, rope=rope is not None, out_scale=out_scale),
        grid=(m // tm, ncols // tn),
        in_specs=in_specs,
        out_specs=pl.BlockSpec((tm, tn), lambda i, j: (i, j)),
        out_shape=jax.ShapeDtypeStruct((m, ncols), out_dtype),
        compiler_params=_cparams(("parallel", "parallel")),
    )(*args)


def _small_proj_kernel(x_ref, g_ref, w_ref, b_ref, o_ref):
    xf = x_ref[...]
    h = xf * lax.rsqrt(jnp.mean(xf * xf, axis=-1, keepdims=True) + EPS) * g_ref[...]
    h_hi, h_mid, _ = _split3(h)
    w_hi, w_mid, _ = _split3(w_ref[...])
    d = lambda a, b: jnp.dot(a, b, preferred_element_type=F32)
    o_ref[...] = d(h_hi, w_hi) + (d(h_hi, w_mid) + d(h_mid, w_hi)) + b_ref[...]


def _small_proj(x, g, w, b, *, tm):
    m, d = x.shape
    n = w.shape[1]
    return pl.pallas_call(
        _small_proj_kernel,
        grid=(m // tm,),
        in_specs=[pl.BlockSpec((tm, d), lambda i: (i, 0)), pl.BlockSpec((1, d), lambda i: (0, 0)),
                  pl.BlockSpec((d, n), lambda i: (0, 0)), pl.BlockSpec((1, n), lambda i: (0, 0))],
        out_specs=pl.BlockSpec((tm, n), lambda i: (i, 0)),
        out_shape=jax.ShapeDtypeStruct((m, n), F32),
        compiler_params=_cparams(("parallel",)),
    )(x, g, w, b)


def _rope_tables(pos):
    inv = ROPE_THETA ** (-jnp.arange(ROPE_HALF, dtype=F32) / ROPE_HALF)
    ang = pos.astype(F32)[:, None] * inv
    cos, sin = jnp.cos(ang), jnp.sin(ang)
    p = pos.shape[0]
    rest = HEAD_DIM - ROPE_DIM
    c = jnp.concatenate([cos, cos, jnp.ones((p, rest), F32)], axis=1)
    s1 = jnp.concatenate([jnp.zeros((p, ROPE_HALF), F32), sin, jnp.zeros((p, rest), F32)], axis=1)
    s2 = jnp.concatenate([-sin, jnp.zeros((p, ROPE_HALF + rest), F32)], axis=1)
    return c, s1, s2


def _log_sigmoid(x):
    return jnp.minimum(x, 0.0) - jnp.log1p(jnp.exp(-jnp.abs(x)))


def _mlstm_kernel(*refs, L, zero_init):
    if zero_init:
        (q_ref, k_ref, v_ref, o_ref, gc_ref, gr_ref, cwq_ref, cwk_ref, cbq_ref, cbk_ref, cvq_ref, cvk_ref, nw_ref,
         h_ref, c_ref, n_ref, m_ref, xq_scr, xk_scr) = refs
    else:
        (q_ref, k_ref, v_ref, o_ref, gc_ref, gr_ref, cwq_ref, cwk_ref, cbq_ref, cbk_ref, cvq_ref, cvk_ref, nw_ref,
         c0_ref, n0_ref, m0_ref, h_ref, c_ref, n_ref, m_ref, xq_scr, xk_scr) = refs
    ci = pl.program_id(2)
    PAD = 8

    @pl.when(ci == 0)
    def _():
        if zero_init:
            c_ref[...] = jnp.zeros_like(c_ref)
            n_ref[...] = jnp.zeros_like(n_ref)
            m_ref[...] = jnp.zeros_like(m_ref)
        else:
            c_ref[...] = c0_ref[...]
            n_ref[...] = n0_ref[...]
            m_ref[...] = m0_ref[...]
        xq_scr[0:PAD, :] = cvq_ref[0]
        xk_scr[0:PAD, :] = cvk_ref[0]

    xq_scr[PAD:PAD + L, :] = q_ref[...]
    xk_scr[PAD:PAD + L, :] = k_ref[...]

    def conv_silu(x_scr, w_ref, b_ref):
        acc = b_ref[...] + jnp.zeros((L, M_HEAD_DIM), F32)
        for j in range(CONV_W):
            acc = acc + x_scr[pl.ds(PAD - (CONV_W - 1) + j, L), :] * w_ref[j:j + 1, :]
        return acc * jax.nn.sigmoid(acc)

    qs = conv_silu(xq_scr, cwq_ref, cbq_ref)
    ks = conv_silu(xk_scr, cwk_ref, cbk_ref) * (M_HEAD_DIM ** -0.5)
    xq_scr[0:PAD, :] = xq_scr[L:L + PAD, :]
    xk_scr[0:PAD, :] = xk_scr[L:L + PAD, :]

    gc = gc_ref[0, 0]
    gr = gr_ref[0, 0]
    i_col, f_col = gc[:, 0:1], _log_sigmoid(gc[:, 1:2])
    i_row, f_row = gr[0:1, :], _log_sigmoid(gr[1:2, :])
    t_idx = lax.broadcasted_iota(jnp.int32, (L, L), 0)
    s_idx = lax.broadcasted_iota(jnp.int32, (L, L), 1)
    tril = s_idx <= t_idx
    b_col = jnp.sum(jnp.where(tril, f_row, 0.0), axis=1, keepdims=True)
    b_row = jnp.sum(jnp.where(t_idx <= s_idx, f_col, 0.0), axis=0, keepdims=True)
    m_prev = m_ref[0, 0]
    dmat = jnp.where(tril, b_col - b_row + i_row, NEG_BIG)
    m_t = jnp.maximum(m_prev + b_col, jnp.max(dmat, axis=1, keepdims=True))
    w = jnp.exp(dmat - m_t)
    inter = jnp.exp(m_prev + b_col - m_t)

    c_prev = c_ref[0, 0]
    n_prev = n_ref[0, 0]
    qb, kb, vb = qs.astype(BF16), ks.astype(BF16), v_ref[...].astype(BF16)
    s = lax.dot_general(qb, kb, (((1,), (1,)), ((), ())), preferred_element_type=F32) * w
    num = inter * jnp.dot(qb, c_prev.astype(BF16), preferred_element_type=F32) \
        + jnp.dot(s.astype(BF16), vb, preferred_element_type=F32)
    den = inter * jnp.sum(qs * n_prev, axis=1, keepdims=True) + jnp.sum(s, axis=1, keepdims=True)
    h = num / jnp.maximum(jnp.abs(den), jnp.exp(-m_t))

    m_new = m_t[L - 1:L, :]
    b_last = b_col[L - 1:L, :]
    decay = jnp.exp(m_prev + b_last - m_new)
    g_col = jnp.exp(i_col + b_last - b_col - m_new)
    kg = ks * g_col
    c_ref[0, 0] = decay * c_prev + lax.dot_general(kg.astype(BF16), vb, (((0,), (0,)), ((), ())),
                                                   preferred_element_type=F32)
    n_ref[0, 0] = decay * n_prev + jnp.sum(kg, axis=0, keepdims=True)
    m_ref[0, 0] = m_new

    ho = jax.nn.sigmoid(o_ref[...]) * h
    mu = jnp.mean(ho, axis=1, keepdims=True)
    var = jnp.mean(jnp.square(ho - mu), axis=1, keepdims=True)
    h_ref[...] = ((ho - mu) * lax.rsqrt(var + EPS) * nw_ref[...]).astype(h_ref.dtype)


def _mlstm(rest, gates_col, gates_row, conv_w, conv_b, conv0, norm_w, state, *, nb, t, L):
    nc = t // L
    hb = M_HEAD_DIM
    zero_init = state is None
    row = lambda b, h, c: b * nc + c
    in_specs = [
        pl.BlockSpec((L, hb), lambda b, h, c: (row(b, h, c), R_MQK // hb + h)),
        pl.BlockSpec((L, hb), lambda b, h, c: (row(b, h, c), (R_MQK + M_INNER) // hb + h)),
        pl.BlockSpec((L, hb), lambda b, h, c: (row(b, h, c), R_MV // hb + h)),
        pl.BlockSpec((L, hb), lambda b, h, c: (row(b, h, c), R_MO // hb + h)),
        pl.BlockSpec((1, 1, L, 2), lambda b, h, c: (b, h, c, 0)),
        pl.BlockSpec((1, 1, 2, L), lambda b, h, c: (b, h, 0, c)),
        pl.BlockSpec((CONV_W, hb), lambda b, h, c: (0, h)),
        pl.BlockSpec((CONV_W, hb), lambda b, h, c: (0, M_HEADS + h)),
        pl.BlockSpec((1, hb), lambda b, h, c: (0, h)),
        pl.BlockSpec((1, hb), lambda b, h, c: (0, M_HEADS + h)),
        pl.BlockSpec((1, 8, hb), lambda b, h, c: (b, 0, h)),
        pl.BlockSpec((1, 8, hb), lambda b, h, c: (b, 0, M_HEADS + h)),
        pl.BlockSpec((1, hb), lambda b, h, c: (0, h)),
    ]
    args = [rest, rest, rest, rest, gates_col, gates_row, conv_w, conv_w, conv_b, conv_b, conv0, conv0, norm_w]
    st_specs = [
        pl.BlockSpec((1, 1, hb, hb), lambda b, h, c: (b, h, 0, 0)),
        pl.BlockSpec((1, 1, 1, hb), lambda b, h, c: (b, h, 0, 0)),
        pl.BlockSpec((1, 1, 1, 1), lambda b, h, c: (b, h, 0, 0)),
    ]
    if not zero_init:
        in_specs += st_specs
        args += list(state)
    out_shape = [
        jax.ShapeDtypeStruct((nb * t, M_INNER), BF16),
        jax.ShapeDtypeStruct((nb, M_HEADS, hb, hb), F32),
        jax.ShapeDtypeStruct((nb, M_HEADS, 1, hb), F32),
        jax.ShapeDtypeStruct((nb, M_HEADS, 1, 1), F32),
    ]
    out_specs = [pl.BlockSpec((L, hb), lambda b, h, c: (row(b, h, c), h))] + st_specs
    return pl.pallas_call(
        functools.partial(_mlstm_kernel, L=L, zero_init=zero_init),
        grid=(nb, M_HEADS, nc),
        in_specs=in_specs,
        out_specs=out_specs,
        out_shape=out_shape,
        scratch_shapes=[pltpu.VMEM((L + 8, hb), F32), pltpu.VMEM((L + 8, hb), F32)],
        compiler_params=_cparams(("parallel", "parallel", "arbitrary")),
    )(*args)


def _rms(y, g):
    return y * lax.rsqrt(jnp.mean(y * y, axis=-1, keepdims=True) + EPS) * g


def _merge_out_kernel(ga_ref, gm_ref, att_ref, mh_ref, x_ref, w_ref, g_ref, o_ref):
    f = lambda r: r[...].astype(F32)
    mixed = jax.nn.sigmoid(f(ga_ref)) * f(att_ref) + jax.nn.sigmoid(f(gm_ref)) * f(mh_ref)
    y = jnp.dot(mixed.astype(BF16), w_ref[...], preferred_element_type=F32)
    o_ref[...] = x_ref[...] + _rms(y, g_ref[...])


def _merge_out(merge, att, mh, x, w_out, g_post, *, tm):
    m, d = x.shape
    row = pl.BlockSpec((tm, d), lambda i: (i, 0))
    return pl.pallas_call(
        _merge_out_kernel,
        grid=(m // tm,),
        in_specs=[
            row,
            pl.BlockSpec((tm, d), lambda i: (i, 1)),
            row, row, row,
            pl.BlockSpec((d, d), lambda i: (0, 0)),
            pl.BlockSpec((1, d), lambda i: (0, 0)),
        ],
        out_specs=row,
        out_shape=jax.ShapeDtypeStruct((m, d), F32),
        compiler_params=_cparams(("parallel",)),
    )(merge, merge, att, mh, x, w_out, g_post)


def _mlp_kernel(x_ref, gpre_ref, wu_ref, wd_ref, gpost_ref, o_ref, h_scr, acc_scr):
    f = pl.program_id(1)

    @pl.when(f == 0)
    def _():
        h_scr[...] = _rms(x_ref[...], gpre_ref[...]).astype(BF16)
        acc_scr[...] = jnp.zeros_like(acc_scr)

    u = jnp.dot(h_scr[...], wu_ref[...], preferred_element_type=F32)
    u = jnp.square(jnp.maximum(u, 0.0))
    acc_scr[...] += jnp.dot(u.astype(BF16), wd_ref[...], preferred_element_type=F32)

    @pl.when(f == pl.num_programs(1) - 1)
    def _():
        o_ref[...] = x_ref[...] + _rms(acc_scr[...], gpost_ref[...])


def _mlp(x, g_pre, w_up, w_down, g_post, *, tm, tf):
    m, d = x.shape
    ff = w_up.shape[1]
    return pl.pallas_call(
        _mlp_kernel,
        grid=(m // tm, ff // tf),
        in_specs=[
            pl.BlockSpec((tm, d), lambda i, f: (i, 0)),
            pl.BlockSpec((1, d), lambda i, f: (0, 0)),
            pl.BlockSpec((d, tf), lambda i, f: (0, f)),
            pl.BlockSpec((tf, d), lambda i, f: (f, 0)),
            pl.BlockSpec((1, d), lambda i, f: (0, 0)),
        ],
        out_specs=pl.BlockSpec((tm, d), lambda i, f: (i, 0)),
        out_shape=jax.ShapeDtypeStruct((m, d), F32),
        scratch_shapes=[pltpu.VMEM((tm, d), BF16), pltpu.VMEM((tm, d), F32)],
        compiler_params=_cparams(("parallel", "arbitrary")),
    )(x, g_pre, w_up, w_down, g_post)


CMP_TAIL_SLOTS = 16


def _pe_bias_kernel(pet_ref, w1_ref, o_ref):
    rows = []
    for k in range(2):
        acc = jnp.zeros((1, HEAD_DIM), F32)
        for l in range(CMP_BLOCK):
            acc = acc + jnp.sum(pet_ref[k][:, l:l + 1] * w1_ref[k, l], axis=0, keepdims=True)
        rows += [acc] * KV_GROUPS
    o_ref[...] = jnp.concatenate(rows, axis=0)


def _pe_bias(cmp_pe, cmp_w1):
    pet = jnp.transpose(cmp_pe, (0, 2, 1))
    return pl.pallas_call(
        _pe_bias_kernel,
        out_shape=jax.ShapeDtypeStruct((KG, HEAD_DIM), F32),
        compiler_params=pltpu.CompilerParams(vmem_limit_bytes=VMEM_LIMIT),
    )(pet, cmp_w1)


def _cmp_kernel(pt_ref, *refs, P, has_tail):
    page_refs = refs[:P]
    if has_tail:
        tail_ref, wc_ref, w2_ref, peb_ref, o_ref, ot_ref, carry, res_scr = refs[P:]
    else:
        wc_ref, w2_ref, peb_ref, o_ref, carry, res_scr = refs[P:]
    p = pl.program_id(1)

    def store_head_major(res, out_ref, n_out):
        n = res.shape[0] // KG
        res_scr[0:n * KG, :] = res
        for kg in range(KG):
            rows = res_scr[pl.ds(kg, n, stride=KG), :]
            if n_out > n:
                rows = jnp.concatenate([rows, jnp.zeros((n_out - n, HEAD_DIM), F32)], axis=0)
            out_ref[0, kg] = rows.astype(out_ref.dtype)

    @pl.when(p == 0)
    def _():
        carry[...] = jnp.zeros_like(carry)

    def is_k(rows):
        return (lax.broadcasted_iota(jnp.int32, (rows, 1), 0) % KG) < KV_GROUPS

    def half_proj(xs):
        rows = xs[0].shape[0]
        lhs = jnp.concatenate([x.astype(BF16) for x in xs], axis=1)
        acc = jnp.dot(lhs, wc_ref[...], preferred_element_type=F32)
        ik = is_k(rows)
        a = jnp.where(ik, acc[:, 0:128], acc[:, 256:384])
        b = jnp.where(ik, acc[:, 128:256], acc[:, 384:512])
        return a, b

    def finish(a_prev, b):
        rows = b.shape[0]
        peb = jnp.concatenate([peb_ref[...]] * (rows // KG), axis=0)
        hid = jax.nn.gelu(a_prev + b + peb)
        o2 = jnp.dot(hid.astype(BF16), w2_ref[...], preferred_element_type=F32)
        return jnp.where(is_k(rows), o2[:, 0:128], o2[:, 128:256])

    xs = [jnp.concatenate([r[0, :, l].reshape(CMP_STRIDE // 2 * KG, HEAD_DIM) for r in page_refs], axis=0)
          for l in range(CMP_STRIDE)]
    a, b = half_proj(xs)
    a_prev = jnp.concatenate([carry[...], a[:-KG]], axis=0)
    store_head_major(finish(a_prev, b), o_ref, o_ref.shape[2])
    carry[...] = a[-KG:]

    if has_tail:
        @pl.when(p == pl.num_programs(1) - 1)
        def _():
            nct = tail_ref.shape[1]
            xt = [tail_ref[0, :, l].reshape(nct * KG, HEAD_DIM) for l in range(CMP_STRIDE)]
            at, bt = half_proj(xt)
            ap = jnp.concatenate([a[-KG:], at[:-KG]], axis=0)
            store_head_major(finish(ap, bt), ot_ref, ot_ref.shape[2])


def _cmp_summaries(pool, page_table, tail, wc, w2c, peb, *, P):
    ns, npg = page_table.shape
    cpp = PAGE_SIZE // CMP_STRIDE
    assert npg % P == 0
    has_tail = tail is not None
    blk = (1, cpp, CMP_STRIDE, KG, HEAD_DIM)
    in_specs = [pl.BlockSpec(blk, functools.partial(lambda s, p, pt, u: (pt[s * npg + p * P + u], 0, 0, 0, 0), u=u))
                for u in range(P)]
    args = [pool] * P
    if has_tail:
        nct = tail.shape[1]
        in_specs.append(pl.BlockSpec((1, nct, CMP_STRIDE, KG, HEAD_DIM), lambda s, p, pt: (s, 0, 0, 0, 0)))
        args.append(tail)
    in_specs += [
        pl.BlockSpec(wc.shape, lambda s, p, pt: (0, 0)),
        pl.BlockSpec(w2c.shape, lambda s, p, pt: (0, 0)),
        pl.BlockSpec(peb.shape, lambda s, p, pt: (0, 0)),
    ]
    args += [wc, w2c, peb]
    out_shape = [jax.ShapeDtypeStruct((ns, KG, npg * cpp, HEAD_DIM), BF16)]
    out_specs = [pl.BlockSpec((1, KG, P * cpp, HEAD_DIM), lambda s, p, pt: (s, 0, p, 0))]
    if has_tail:
        out_shape.append(jax.ShapeDtypeStruct((ns, KG, CMP_TAIL_SLOTS, HEAD_DIM), BF16))
        out_specs.append(pl.BlockSpec((1, KG, CMP_TAIL_SLOTS, HEAD_DIM), lambda s, p, pt: (s, 0, 0, 0)))
    return pl.pallas_call(
        functools.partial(_cmp_kernel, P=P, has_tail=has_tail),
        grid_spec=pltpu.PrefetchScalarGridSpec(
            num_scalar_prefetch=1, grid=(ns, npg // P), in_specs=in_specs, out_specs=out_specs,
            scratch_shapes=[pltpu.VMEM((KG, HEAD_DIM), F32), pltpu.VMEM((P * cpp * KG, HEAD_DIM), F32)]),
        out_shape=out_shape,
        compiler_params=_cparams(("parallel", "arbitrary")),
    )(page_table.reshape(-1), *args)


def _pack_cmp_weights(cmp_w1, cmp_w2):
    def cols(l):
        return jnp.concatenate([cmp_w1[0, l], cmp_w1[0, CMP_STRIDE + l], cmp_w1[1, l], cmp_w1[1, CMP_STRIDE + l]], axis=1)
    wc = jnp.concatenate([cols(l) for l in range(CMP_STRIDE)], axis=0).astype(BF16)
    w2c = jnp.concatenate([cmp_w2[0], cmp_w2[1]], axis=1).astype(BF16)
    return wc, w2c


def _overlap_matrix(n_slots, n_sb, width):
    i = jnp.arange(n_slots)[:, None] - 1
    j = jnp.arange(width)[None, :]
    ov = (i >= 0) & (j < n_sb) & (i * CMP_STRIDE < (j + 1) * SLC_BLOCK) & (i * CMP_STRIDE + CMP_BLOCK > j * SLC_BLOCK)
    return ov.astype(BF16)


def _masked_softmax(s, mask):
    s = jnp.where(mask, s, -jnp.inf)
    m = jnp.max(s, axis=-1, keepdims=True)
    m = jnp.where(m > -jnp.inf, m, 0.0)
    e = jnp.exp2(s - m)
    return e / jnp.maximum(jnp.sum(e, axis=-1, keepdims=True), 1e-30)


def _dot_nt(a, b):
    return lax.dot_general(a, b, (((1,), (1,)), ((), ())), preferred_element_type=F32)


def _attn_prompt_kernel(q_ref, kc_ref, vc_ref, ks_ref, vs_ref, kw_ref, vw_ref, gate_ref, ov_ref, e_ref, ge_ref,
                        o_ref, *, tq, n_sb, sel_step):
    i = pl.program_id(2)
    hpg = HEADS_PER_GROUP
    rows = hpg * tq
    s_len = ks_ref.shape[2]

    q = jnp.concatenate([q_ref[:, h * HEAD_DIM:(h + 1) * HEAD_DIM] for h in range(hpg)], axis=0)
    t0 = i * tq
    t_row = t0 + lax.broadcasted_iota(jnp.int32, (rows, 1), 0) % tq
    t_tok = t0 + lax.broadcasted_iota(jnp.int32, (tq, 1), 0)

    def compressed_branch():
        n_slots = kc_ref.shape[2]
        s_c = _dot_nt(q, kc_ref[0, 0])
        slot = lax.broadcasted_iota(jnp.int32, (1, n_slots), 1)
        mask_c = (slot >= 1) & ((slot - 1) * CMP_STRIDE + CMP_BLOCK - 1 <= t_row)
        p_c = _masked_softmax(s_c, mask_c)
        o_c = jnp.dot(p_c.astype(BF16), vc_ref[0, 0], preferred_element_type=F32)
        p_grp = p_c[0:tq]
        for h in range(1, hpg):
            p_grp = p_grp + p_c[h * tq:(h + 1) * tq]
        return o_c, p_grp

    def select_blocks(p_grp):
        score = _dot_exact01(p_grp, ov_ref[...])
        jl = lax.broadcasted_iota(jnp.int32, (1, LANES), 1)
        cur = t_tok // SLC_BLOCK
        valid = jl <= cur
        forced = (jl == 0) | (jl == cur) | (jl == cur - 1)
        score = jnp.where(forced, FORCE_SCORE, score)
        score = jnp.where(valid, score, -jnp.inf)
        rank = jnp.zeros((tq, LANES), F32)
        for j2 in range(n_sb):
            col = score[:, j2:j2 + 1]
            later = jnp.where(jl > j2, 1.0, 0.0)
            rank = rank + jnp.where(col > score, 1.0, 0.0) + jnp.where(col == score, later, 0.0)
        return jnp.where((rank < N_SELECT) & valid, 1.0, 0.0).astype(BF16)

    def attend(k, v, keep):
        bias = jnp.where(keep, 0.0, NEG_BIG)
        s = _dot_nt(q, k)
        es = []
        for h in range(hpg):
            sh = s[h * tq:(h + 1) * tq] + bias
            es.append(jnp.exp2(sh - jnp.max(sh, axis=1, keepdims=True)).astype(BF16))
        ones = (lax.broadcasted_iota(jnp.int32, (v.shape[0], LANES), 1) == 0).astype(BF16)
        oa = jnp.dot(jnp.concatenate(es, axis=0), jnp.concatenate([v, ones], axis=1), preferred_element_type=F32)
        return oa[:, :HEAD_DIM] / oa[:, HEAD_DIM:HEAD_DIM + 1]

    def window_branch():
        band = WINDOW + tq
        start = pl.multiple_of(jnp.maximum(t0 - WINDOW, 0), tq)
        kpos = start + lax.broadcasted_iota(jnp.int32, (1, band), 1)
        return attend(kw_ref[0, 0, pl.ds(start, band), :], vw_ref[0, 0, pl.ds(start, band), :],
                      (kpos <= t_tok) & (kpos > t_tok - WINDOW))

    def combine(o_c, o_s, o_w):
        sig = jax.nn.sigmoid(gate_ref[...])
        g_c, g_s, g_w = (_dot_exact01(sig, ge_ref[k, 0]) for k in range(3))
        for h in range(hpg):
            hs = slice(h * HEAD_DIM, (h + 1) * HEAD_DIM)
            rs = slice(h * tq, (h + 1) * tq)
            o_ref[:, hs] = (g_c[:, hs] * o_c[rs] + g_s[:, hs] * o_s[rs] + g_w[:, hs] * o_w[rs]).astype(o_ref.dtype)

    for var in range(s_len // sel_step):
        width = (var + 1) * sel_step

        @pl.when((t0 + tq - 1) // sel_step == var)
        def _(width=width):
            o_c, p_grp = compressed_branch()
            kpos = lax.broadcasted_iota(jnp.int32, (1, width), 1)
            keep = kpos <= t_tok
            if width > N_SELECT * SLC_BLOCK:
                selx = jnp.dot(select_blocks(p_grp), e_ref[:, 0:width], preferred_element_type=F32)
                keep = keep & (selx > 0.5)
            o_s = attend(ks_ref[0, 0, 0:width, :], vs_ref[0, 0, 0:width, :], keep)
            combine(o_c, o_s, window_branch())


def _gate_expand(width_heads):
    c = jnp.arange(LANES)[None, :, None]
    k = jnp.arange(3)[:, None, None]
    hh = (jnp.arange(width_heads * HEAD_DIM) // HEAD_DIM)[None, None, :]
    return (c == k * N_HEADS + hh).astype(BF16)


def _attn_prompt(q, kvc_slots, kvs, kvw, small, *, nb, s_len, tq=256, sel_step=256):
    nq = s_len // tq
    n_slots = kvc_slots.shape[2]
    n_sb = s_len // SLC_BLOCK
    assert n_sb <= LANES and n_slots <= LANES and s_len % sel_step == 0 and sel_step % tq == 0
    assert s_len >= WINDOW + tq and WINDOW % tq == 0
    hd = HEAD_DIM
    gw = HEADS_PER_GROUP * hd
    ov = _overlap_matrix(n_slots, n_sb, LANES)
    e = (jnp.arange(LANES)[:, None] == (jnp.arange(s_len) // SLC_BLOCK)[None, :]).astype(BF16)
    ge = _gate_expand(N_HEADS).reshape(3, LANES, KV_GROUPS, gw).transpose(0, 2, 1, 3)
    kspec = lambda kv: pl.BlockSpec((1, 1, s_len, hd), lambda b, g, i: (b, kv * KV_GROUPS + g, 0, 0))
    cspec = lambda kv: pl.BlockSpec((1, 1, n_slots, hd), lambda b, g, i: (b, kv * KV_GROUPS + g, 0, 0))
    return pl.pallas_call(
        functools.partial(_attn_prompt_kernel, tq=tq, n_sb=n_sb, sel_step=sel_step),
        grid=(nb, KV_GROUPS, nq),
        in_specs=[
            pl.BlockSpec((tq, gw), lambda b, g, i: (b * nq + i, g)),
            cspec(0), cspec(1), kspec(0), kspec(1), kspec(0), kspec(1),
            pl.BlockSpec((tq, LANES), lambda b, g, i: (b * nq + i, 0)),
            pl.BlockSpec(ov.shape, lambda b, g, i: (0, 0)),
            pl.BlockSpec(e.shape, lambda b, g, i: (0, 0)),
            pl.BlockSpec((3, 1, LANES, gw), lambda b, g, i: (0, g, 0, 0)),
        ],
        out_specs=pl.BlockSpec((tq, gw), lambda b, g, i: (b * nq + i, g)),
        out_shape=jax.ShapeDtypeStruct((nb * s_len, N_HEADS * hd), BF16),
        compiler_params=_cparams(("parallel", "parallel", "parallel")),
    )(q, kvc_slots, kvc_slots, kvs, kvs, kvw, kvw, small, ov, e, ge)


T_PAD = 8


def _attn_sample_a_kernel(q_ref, kcm_ref, kct_ref, wc_ref, wn_ref, ov_ref, oc_ref, ow_ref, idx_ref,
                          *, past, t_new, n_blocks, n_sb, wb):
    hpg = HEADS_PER_GROUP
    rows = hpg * T_PAD
    t_pos = past + lax.broadcasted_iota(jnp.int32, (rows, 1), 0) % T_PAD
    tp8 = past + lax.broadcasted_iota(jnp.int32, (T_PAD, 1), 0)
    n_slots = kcm_ref.shape[2] + kct_ref.shape[2]
    slot = lax.broadcasted_iota(jnp.int32, (1, n_slots), 1)
    mask_c = (slot >= 1) & (slot <= n_blocks) & ((slot - 1) * CMP_STRIDE + CMP_BLOCK - 1 <= t_pos)
    width = ov_ref.shape[1]
    jl = lax.broadcasted_iota(jnp.int32, (1, width), 1)
    jf = jl.astype(F32)
    cur = tp8 // SLC_BLOCK
    valid = (jl <= cur) & (jl < n_sb)
    forced = (jl == 0) | (jl == cur) | (jl == cur - 1)
    lane = lax.broadcasted_iota(jnp.int32, (1, LANES), 1)
    wj = lax.broadcasted_iota(jnp.int32, (1, wb + T_PAD), 1)
    kp = past - wb + wj
    mask_w = (kp <= t_pos) & (kp > t_pos - WINDOW) & (kp >= 0) & (wj < wb + t_new)

    for g in range(KV_GROUPS):
        q = q_ref[0, g].astype(BF16)
        kc = jnp.concatenate([kcm_ref[0, g], kct_ref[0, g]], axis=0)
        vc = jnp.concatenate([kcm_ref[0, KV_GROUPS + g], kct_ref[0, KV_GROUPS + g]], axis=0)
        p_c = _masked_softmax(_dot_nt(q, kc), mask_c)
        oc_ref[0, g] = jnp.dot(p_c.astype(BF16), vc, preferred_element_type=F32)
        p_grp = p_c[0:T_PAD]
        for h in range(1, hpg):
            p_grp = p_grp + p_c[h * T_PAD:(h + 1) * T_PAD]
        score = _dot_exact01(p_grp, ov_ref[...])
        score = jnp.where(forced, FORCE_SCORE, score)
        score = jnp.where(valid, score, -jnp.inf)
        idx = jnp.zeros((T_PAD, LANES), jnp.int32)
        for k in range(N_SELECT):
            mx = jnp.max(score, axis=1, keepdims=True)
            am = jnp.min(jnp.where(score == mx, jf, 1e9), axis=1, keepdims=True)
            pick = jnp.where(mx > -jnp.inf, am, -1.0).astype(jnp.int32)
            idx = jnp.where(lane == k, pick, idx)
            score = jnp.where(jf == am, -jnp.inf, score)
        idx_ref[0, g] = idx

        kw = jnp.concatenate([wc_ref[0, :, 0, g, :], wn_ref[0, :, 0, g, :]], axis=0).astype(BF16)
        vw = jnp.concatenate([wc_ref[0, :, 1, g, :], wn_ref[0, :, 1, g, :]], axis=0).astype(BF16)
        p_w = _masked_softmax(_dot_nt(q, kw), mask_w)
        ow_ref[0, g] = jnp.dot(p_w.astype(BF16), vw, preferred_element_type=F32)


def _attn_sample_a(q4, slots_main, slots_tail, win_cache, win_new, *, past, t_new, n_blocks, n_sb):
    ns = q4.shape[0]
    wb = win_cache.shape[1]
    n_slots = slots_main.shape[2] + slots_tail.shape[2]
    width = -(-n_sb // LANES) * LANES
    ov = _overlap_matrix(n_slots, n_sb, width)
    rows = HEADS_PER_GROUP * T_PAD
    full = lambda a: pl.BlockSpec((1,) + a.shape[1:], lambda s: (s,) + (0,) * (a.ndim - 1))
    o_sds = jax.ShapeDtypeStruct((ns, KV_GROUPS, rows, HEAD_DIM), F32)
    o_spec = pl.BlockSpec((1, KV_GROUPS, rows, HEAD_DIM), lambda s: (s, 0, 0, 0))
    return pl.pallas_call(
        functools.partial(_attn_sample_a_kernel, past=past, t_new=t_new, n_blocks=n_blocks, n_sb=n_sb, wb=wb),
        grid=(ns,),
        in_specs=[full(q4), full(slots_main), full(slots_tail), full(win_cache), full(win_new),
                  pl.BlockSpec(ov.shape, lambda s: (0, 0))],
        out_specs=[o_spec, o_spec, pl.BlockSpec((1, KV_GROUPS, T_PAD, LANES), lambda s: (s, 0, 0, 0))],
        out_shape=[o_sds, o_sds, jax.ShapeDtypeStruct((ns, KV_GROUPS, T_PAD, LANES), jnp.int32)],
        compiler_params=_cparams(("parallel",)),
    )(q4, slots_main, slots_tail, win_cache, win_new, ov)


def _attn_sample_b_kernel(idx_ref, pt_ref, q_ref, pool_ref, tail_ref, o_ref, buf, sem, *, past, nb_past, t_new, npg):
    n = N_SELECT
    G = KV_GROUPS
    bpp = PAGE_SIZE // SLC_BLOCK
    nblk = t_new * n
    s, g = pl.program_id(0), pl.program_id(1)
    step = s * G + g
    nsteps = pl.num_programs(0) * G
    slot = step % 2

    def start_copies(step_, slot_):
        s_, g_ = step_ // G, step_ % G

        def body(j, carry):
            jj = idx_ref[step_ * nblk + j]
            jp = jnp.clip(jj, 0, nb_past - 1)
            page = pt_ref[s_ * npg + jp // bpp]
            r0 = pl.multiple_of((jp % bpp) * SLC_BLOCK, SLC_BLOCK)
            for kv in range(2):
                @pl.when(jj < nb_past)
                def _(kv=kv):
                    pltpu.make_async_copy(pool_ref.at[page, pl.ds(r0, SLC_BLOCK), kv, g_], buf.at[slot_, kv, j],
                                          sem.at[slot_]).start()

                @pl.when(jj >= nb_past)
                def _(kv=kv):
                    pltpu.make_async_copy(tail_ref.at[s_, :, kv, g_], buf.at[slot_, kv, j], sem.at[slot_]).start()
            return carry

        lax.fori_loop(0, nblk, body, 0)

    def wait_copies(slot_):
        def body(j, carry):
            for kv in range(2):
                pltpu.make_async_copy(tail_ref.at[0, :, kv, 0], buf.at[slot_, kv, j], sem.at[slot_]).wait()
            return carry

        lax.fori_loop(0, nblk, body, 0)

    @pl.when(step == 0)
    def _():
        start_copies(step, slot)

    @pl.when(step + 1 < nsteps)
    def _():
        start_copies(step + 1, 1 - slot)

    wait_copies(slot)

    q = q_ref[0, 0].astype(BF16)
    rows = q.shape[0]
    t8 = lax.broadcasted_iota(jnp.int32, (rows, 1), 0) % T_PAD
    jl = lax.broadcasted_iota(jnp.int32, (1, n * SLC_BLOCK), 1)
    kslot = jl // SLC_BLOCK
    out = jnp.zeros((rows, HEAD_DIM), F32)
    for t in range(t_new):
        start = jnp.zeros((1, n * SLC_BLOCK), jnp.int32)
        ok = jnp.zeros((1, n * SLC_BLOCK), jnp.int32)
        for k in range(n):
            jj = idx_ref[step * nblk + t * n + k]
            start = jnp.where(kslot == k, jj * SLC_BLOCK, start)
            ok = jnp.where(kslot == k, (jj >= 0).astype(jnp.int32), ok)
        kpos = start + jl % SLC_BLOCK
        mask = (ok > 0) & (kpos <= past + t)
        kk = buf[slot, 0, t * n:(t + 1) * n].reshape(n * SLC_BLOCK, HEAD_DIM).astype(BF16)
        vv = buf[slot, 1, t * n:(t + 1) * n].reshape(n * SLC_BLOCK, HEAD_DIM).astype(BF16)
        p = _masked_softmax(_dot_nt(q, kk), mask)
        o_t = jnp.dot(p.astype(BF16), vv, preferred_element_type=F32)
        out = jnp.where(t8 == t, o_t, out)
    o_ref[0, 0] = out


def _attn_sample_b(idx_flat, pt_flat, q4, tail, pool, *, past, npg, t_new):
    ns = q4.shape[0]
    G = KV_GROUPS
    rows = q4.shape[2]
    assert tail.shape[1] == SLC_BLOCK
    qspec = pl.BlockSpec((1, 1, rows, HEAD_DIM), lambda s, g, idx, pt: (s, g, 0, 0))
    return pl.pallas_call(
        functools.partial(_attn_sample_b_kernel, past=past, nb_past=past // SLC_BLOCK, t_new=t_new, npg=npg),
        grid_spec=pltpu.PrefetchScalarGridSpec(
            num_scalar_prefetch=2, grid=(ns, G),
            in_specs=[qspec, pl.BlockSpec(memory_space=pl.ANY), pl.BlockSpec(memory_space=pl.ANY)],
            out_specs=qspec,
            scratch_shapes=[pltpu.VMEM((2, 2, t_new * N_SELECT, SLC_BLOCK, HEAD_DIM), F32),
                            pltpu.SemaphoreType.DMA((2,))]),
        out_shape=jax.ShapeDtypeStruct((ns, G, rows, HEAD_DIM), F32),
        compiler_params=_cparams(("arbitrary", "arbitrary")),
    )(idx_flat, pt_flat, q4, pool, tail)


def _nsa_combine_kernel(oc_ref, os_ref, ow_ref, gate_ref, ge_ref, o_ref):
    sig = jax.nn.sigmoid(gate_ref[...])
    o_ref[...] = (_dot_exact01(sig, ge_ref[0]) * oc_ref[...] + _dot_exact01(sig, ge_ref[1]) * os_ref[...]
                  + _dot_exact01(sig, ge_ref[2]) * ow_ref[...])


def _nsa_combine(oc, os_, ow, small):
    ge = _gate_expand(N_HEADS)
    return pl.pallas_call(
        _nsa_combine_kernel,
        out_shape=jax.ShapeDtypeStruct(oc.shape, F32),
        compiler_params=pltpu.CompilerParams(vmem_limit_bytes=VMEM_LIMIT),
    )(oc, os_, ow, small, ge)


def _shift_window_kernel(cache_ref, new_ref, o_ref, sem, *, t_new):
    keep = cache_ref.shape[1] - t_new
    old = pltpu.make_async_copy(cache_ref.at[:, pl.ds(t_new, keep)], o_ref.at[:, pl.ds(0, keep)], sem.at[0])
    new = pltpu.make_async_copy(new_ref, o_ref.at[:, pl.ds(keep, t_new)], sem.at[1])
    old.start()
    new.start()
    old.wait()
    new.wait()


def _shift_window(cache, new):
    t_new = new.shape[1]
    assert 0 < t_new <= cache.shape[1]
    return pl.pallas_call(
        functools.partial(_shift_window_kernel, t_new=t_new),
        in_specs=[pl.BlockSpec(memory_space=pl.ANY), pl.BlockSpec(memory_space=pl.ANY)],
        out_specs=pl.BlockSpec(memory_space=pl.ANY),
        out_shape=jax.ShapeDtypeStruct(cache.shape, cache.dtype),
        scratch_shapes=[pltpu.SemaphoreType.DMA((2,))],
    )(cache, new)


def _repack_kernel(a_ref, b_ref, o_ref, *, shift):
    if shift == 0:
        o_ref[...] = a_ref[...].astype(o_ref.dtype)
        return
    tn = a_ref.shape[1]
    cat = jnp.concatenate([a_ref[...], b_ref[...]], axis=1)
    o_ref[...] = pltpu.roll(cat, 2 * tn - shift, 1)[:, :tn].astype(o_ref.dtype)


def _repack(w, *, col_start, ncols, tk=512, tn=512):
    k, n = w.shape
    base, shift = divmod(col_start, tn)
    last = pl.cdiv(n, tn) - 1
    assert k % tk == 0 and ncols % tn == 0 and base + ncols // tn - 1 + (shift > 0) <= last
    return pl.pallas_call(
        functools.partial(_repack_kernel, shift=shift),
        grid=(k // tk, ncols // tn),
        in_specs=[pl.BlockSpec((tk, tn), lambda i, j: (i, base + j)),
                  pl.BlockSpec((tk, tn), lambda i, j: (i, jnp.minimum(base + j + 1, last)))],
        out_specs=pl.BlockSpec((tk, tn), lambda i, j: (i, j)),
        out_shape=jax.ShapeDtypeStruct((k, ncols), BF16),
        compiler_params=_cparams(("parallel", "parallel")),
    )(w, w)


def _small_cols_kernel(a_ref, b_ref, o_ref, *, n_a, n_b):
    lane = lax.broadcasted_iota(jnp.int32, (1, LANES), 1)
    o_ref[...] = jnp.where(lane < n_a, a_ref[...], jnp.where(lane < n_a + n_b, b_ref[...], 0.0))


def _small_cols(w, *, col_a, n_a, col_b, n_b):
    k = w.shape[0]
    assert col_a % LANES == 0 and col_b % LANES == n_a and n_a + n_b <= LANES
    return pl.pallas_call(
        functools.partial(_small_cols_kernel, n_a=n_a, n_b=n_b),
        grid=(1,),
        in_specs=[pl.BlockSpec((k, LANES), lambda i: (0, col_a // LANES)),
                  pl.BlockSpec((k, LANES), lambda i: (0, col_b // LANES))],
        out_specs=pl.BlockSpec((k, LANES), lambda i: (0, 0)),
        out_shape=jax.ShapeDtypeStruct((k, LANES), F32),
        compiler_params=_cparams(("arbitrary",)),
    )(w, w)


def _pack_in_proj(w_in, b_in):
    sizes = (N_HEADS * HEAD_DIM, KV_WIDTH, KV_WIDTH, KV_WIDTH, N_GATE, 2 * M_INNER, M_INNER, M_INNER, 2 * M_HEADS,
             2 * D_MODEL)
    offs = [0]
    for sz in sizes:
        offs.append(offs[-1] + sz)
    o_attg, o_mqk, o_mif, o_merge, end = offs[4], offs[5], offs[8], offs[9], offs[10]
    pad = LANES - N_GATE - 2 * M_HEADS

    b = b_in[None, :]
    b_slabs = [b[:, :o_attg], b[:, o_mqk:o_mif], b[:, o_merge:end]]
    b_small = jnp.concatenate([b[:, o_attg:o_mqk], b[:, o_mif:o_merge], jnp.zeros((1, pad), F32)], axis=-1)
    w_slabs = [_repack(w_in, col_start=0, ncols=o_attg), _repack(w_in, col_start=o_mqk, ncols=o_mif - o_mqk),
               _repack(w_in, col_start=o_merge, ncols=end - o_merge)]
    w_small = _small_cols(w_in, col_a=o_attg, n_a=o_mqk - o_attg, col_b=o_mif, n_b=o_merge - o_mif)
    return w_slabs, b_slabs, w_small, b_small


def _project(x, pos, g_pre, packed, *, tm, seq_len=None):
    (w_att, w_rest, w_merge), (b_att, b_rest, b_merge), w_small, b_small = packed
    tabs = _rope_tables(pos)
    h = _rms_norm_bf16(x, g_pre, tm=min(tm, 512))
    q = _proj(h, w_att, b_att, col0=OFF_Q, ncols=N_HEADS * HEAD_DIM, out_dtype=BF16, tm=tm, tn=512, rope=tabs,
              out_scale=ATT_SCALE * LOG2E)
    kv = [_kv_proj(h, w_att, b_att, tabs, col0=off, tm=tm, seq_len=None if off == OFF_KVC else seq_len)
          for off in (OFF_KVC, OFF_KVS, OFF_KVW)]
    rest = _proj(h, w_rest, b_rest, col0=0, ncols=REST_WIDTH, tm=tm, tn=1024, out_dtype=F32)
    merge = _proj(h, w_merge, b_merge, col0=0, ncols=2 * D_MODEL, tm=tm, tn=1024, out_dtype=BF16)
    small = _small_proj(x, g_pre, w_small, b_small, tm=min(tm, 512))
    return q, kv, rest, merge, small


def _gate_layouts(small, nb, t, t_pad):
    mif = small[:, N_GATE:N_GATE + 2 * M_HEADS].reshape(nb, t, 2, M_HEADS)
    if t_pad > t:
        fill = jnp.broadcast_to(jnp.array([NEG_BIG, 1e4], F32)[None, None, :, None], (nb, t_pad - t, 2, M_HEADS))
        mif = jnp.concatenate([mif, fill], axis=1)
    return jnp.transpose(mif, (0, 3, 1, 2)), jnp.transpose(mif, (0, 3, 2, 1))


def _pad_rows(a, n):
    return jnp.concatenate([a, jnp.zeros((a.shape[0], n - a.shape[1]) + a.shape[2:], a.dtype)], axis=1)


def _nsa_sample(q_s, kvc_s, kvs_s, kvw_s, small_s, cache_kv_cmp, cache_kv_slc, cache_win_kv, page_table,
                wc, w2c, peb, *, DB, T):
    npg = page_table.shape[1]
    past = npg * PAGE_SIZE
    n_phys = cache_kv_cmp.shape[0]
    cpp = PAGE_SIZE // CMP_STRIDE
    t_blk = -(-T // SLC_BLOCK) * SLC_BLOCK
    assert t_blk == SLC_BLOCK
    new_rows = lambda a, n: _pad_rows(a.reshape(DB, T, 2, KV_GROUPS, HEAD_DIM), n)
    cmp_tail = new_rows(kvc_s, t_blk).reshape(DB, t_blk // CMP_STRIDE, CMP_STRIDE, KG, HEAD_DIM)
    slots_s, slots_t = _cmp_summaries(cache_kv_cmp.reshape(n_phys, cpp, CMP_STRIDE, KG, HEAD_DIM), page_table,
                                      cmp_tail, wc, w2c, peb, P=16)
    n_blocks = (past + t_blk) // CMP_STRIDE - 1
    nb_past = past // SLC_BLOCK
    n_sb = nb_past + t_blk // SLC_BLOCK
    q6 = q_s.astype(F32).reshape(DB, T, KV_GROUPS, HEADS_PER_GROUP, HEAD_DIM)
    q4 = _pad_rows(jnp.transpose(q6, (0, 2, 3, 1, 4)).reshape(DB * KV_GROUPS * HEADS_PER_GROUP, T, HEAD_DIM), T_PAD)
    q4 = q4.reshape(DB, KV_GROUPS, HEADS_PER_GROUP * T_PAD, HEAD_DIM)
    oc, ow, idx = _attn_sample_a(q4, slots_s, slots_t, cache_win_kv, new_rows(kvw_s, T_PAD),
                                 past=past, t_new=T, n_blocks=n_blocks, n_sb=n_sb)
    idx_flat = idx[:, :, :T, :N_SELECT].reshape(-1)
    os4 = _attn_sample_b(idx_flat, page_table.reshape(-1), q4, new_rows(kvs_s, SLC_BLOCK), cache_kv_slc,
                         past=past, npg=npg, t_new=T)
    tok_major = lambda o: jnp.transpose(o.reshape(DB, KV_GROUPS, HEADS_PER_GROUP, T_PAD, HEAD_DIM)[:, :, :, :T],
                                        (0, 3, 1, 2, 4)).reshape(DB * T, N_HEADS * HEAD_DIM)
    return _nsa_combine(tok_major(oc), tok_major(os4), tok_major(ow), small_s)


def kernel(x_prompt, x_sample, cache_kv_cmp, cache_kv_slc, cache_win_kv, state_mlstm_c, state_mlstm_n, state_mlstm_m, state_mlstm_conv, page_table, g_mix_pre, w_in, b_in, cmp_pe, cmp_w1, cmp_w2, conv_w, conv_b, mlstm_norm_w, w_out, g_mix_post, g_mlp_pre, w_up, w_down, g_mlp_post):
    B, S, D = x_prompt.shape
    DB, T, _ = x_sample.shape
    npg = page_table.shape[1]
    past = npg * PAGE_SIZE
    n_phys = cache_kv_cmp.shape[0]
    cpp = PAGE_SIZE // CMP_STRIDE
    kv5 = (2, KV_GROUPS, HEAD_DIM)

    packed = _pack_in_proj(w_in, b_in)
    w_out_b, w_up_b, w_down_b = w_out.astype(BF16), w_up.astype(BF16), w_down.astype(BF16)
    row = lambda v: v[None, :]
    peb = _pe_bias(cmp_pe, cmp_w1)
    wc, w2c = _pack_cmp_weights(cmp_w1, cmp_w2)

    def tail(merge, att, mh, x, tm, tm_mlp):
        x1 = _merge_out(merge, att, mh, x, w_out_b, row(g_mix_post), tm=tm)
        return _mlp(x1, row(g_mlp_pre), w_up_b, w_down_b, row(g_mlp_post), tm=tm_mlp, tf=1024)

    xp = x_prompt.reshape(B * S, D)
    q_p, kv_p, rest_p, merge_p, small_p = _project(xp, jnp.arange(S), row(g_mix_pre), packed, tm=1024, seq_len=S)
    (kvc_p,), (kvs_p, kvs_hm), (kvw_p, kvw_hm) = kv_p
    pt_p = jnp.arange(B * S // PAGE_SIZE, dtype=jnp.int32).reshape(B, S // PAGE_SIZE)
    slots_p, = _cmp_summaries(kvc_p.reshape(B * S // PAGE_SIZE, cpp, CMP_STRIDE, KG, HEAD_DIM), pt_p, None,
                              wc, w2c, peb, P=16)
    att_p = _attn_prompt(q_p, slots_p, kvs_hm, kvw_hm, small_p, nb=B, s_len=S)
    gcol, grow = _gate_layouts(small_p, B, S, S)
    mh_p, c_p, n_p, m_p = _mlstm(rest_p, gcol, grow, conv_w, row(conv_b), jnp.zeros((B, 8, 2 * M_INNER), F32),
                                 row(mlstm_norm_w), None, nb=B, t=S, L=256)
    y_p = tail(merge_p, att_p, mh_p, xp, 256, 512)
    wlen = min(WINDOW, S)
    conv_p = rest_p.reshape(B, S, REST_WIDTH)[:, S - (CONV_W - 1):, R_MQK:R_MQK + 2 * M_INNER]

    xs = x_sample.reshape(DB * T, D)
    pos_s = jnp.tile(past + jnp.arange(T), DB)
    q_s, ((kvc_s,), (kvs_s,), (kvw_s,)), rest_s, merge_s, small_s = _project(xs, pos_s, row(g_mix_pre), packed, tm=DB * T)
    att_s = _nsa_sample(q_s, kvc_s, kvs_s, kvw_s, small_s, cache_kv_cmp, cache_kv_slc, cache_win_kv, page_table,
                        wc, w2c, peb, DB=DB, T=T)

    LS = 16
    gcol_s, grow_s = _gate_layouts(small_s, DB, T, LS)
    rest_pad = _pad_rows(rest_s.reshape(DB, T, REST_WIDTH), LS).reshape(DB * LS, REST_WIDTH)
    conv0 = jnp.concatenate([jnp.zeros((DB, 8 - (CONV_W - 1), 2 * M_INNER), F32), state_mlstm_conv], axis=1)
    state = (state_mlstm_c, state_mlstm_n[:, :, None, :], state_mlstm_m[:, :, None, None])
    mh_pad, c_s, n_s, m_s = _mlstm(rest_pad, gcol_s, grow_s, conv_w, row(conv_b), conv0, row(mlstm_norm_w), state,
                                   nb=DB, t=LS, L=LS)
    mh_s = mh_pad.reshape(DB, LS, M_INNER)[:, :T].reshape(DB * T, M_INNER)
    y_s = tail(merge_s, att_s, mh_s, xs, DB * T, DB * T)
    win_s = _shift_window(cache_win_kv, kvw_s.reshape((DB, T) + kv5).astype(cache_win_kv.dtype))
    mqk_s = rest_s.reshape(DB, T, REST_WIDTH)[:, :, R_MQK:R_MQK + 2 * M_INNER]
    conv_s = jnp.concatenate([state_mlstm_conv, mqk_s], axis=1)[:, T:]

    return (y_p.reshape(B, S, D), y_s.reshape(DB, T, D),
            kvc_p.reshape((B, S) + kv5), kvc_s.reshape((DB, T) + kv5),
            kvs_p.reshape((B, S) + kv5), kvs_s.reshape((DB, T) + kv5),
            kvw_p.reshape((B, S) + kv5)[:, S - wlen:], win_s,
            c_p, c_s, n_p[:, :, 0], n_s[:, :, 0], m_p[:, :, 0, 0], m_s[:, :, 0, 0], conv_p, conv_s)
```

```python
import functools

import jax
import jax.numpy as jnp
from jax import lax
from jax.experimental import pallas as pl
from jax.experimental.pallas import tpu as pltpu

F32 = jnp.float32
BF16 = jnp.bfloat16

D_MODEL = 2048
N_HEADS = 16
HEAD_DIM = 128
KV_GROUPS = 4
HEADS_PER_GROUP = N_HEADS // KV_GROUPS
ROPE_DIM = HEAD_DIM // 4
ROPE_HALF = ROPE_DIM // 2
ROPE_THETA = 500000.0
CMP_STRIDE = 16
CMP_BLOCK = 2 * CMP_STRIDE
SLC_BLOCK = 64
N_SELECT = 16
WINDOW = 512
ATT_SCALE = HEAD_DIM ** -0.5
LOG2E = 1.4426950408889634
FORCE_SCORE = 1e9
M_HEADS = 4
M_INNER = D_MODEL
M_HEAD_DIM = M_INNER // M_HEADS
CONV_W = 4
NEG_BIG = -1e30
D_FF = 4 * D_MODEL
EPS = 1e-6
PAGE_SIZE = 128
KV_WIDTH = 2 * KV_GROUPS * HEAD_DIM
KG = 2 * KV_GROUPS

LANES = 128
VMEM_LIMIT = 56 * 1024 * 1024

OFF_Q = 0
OFF_KVC = OFF_Q + N_HEADS * HEAD_DIM
OFF_KVS = OFF_KVC + KV_WIDTH
OFF_KVW = OFF_KVS + KV_WIDTH
ATT_WIDTH = OFF_KVW + KV_WIDTH
REST_WIDTH = 2 * M_INNER + M_INNER + M_INNER
N_GATE = 3 * N_HEADS
R_MQK, R_MV, R_MO = 0, 2 * M_INNER, 3 * M_INNER


def _cparams(sem):
    return pltpu.CompilerParams(dimension_semantics=sem, vmem_limit_bytes=VMEM_LIMIT)


def _split3(x):
    hi = x.astype(BF16)
    r1 = x - hi.astype(F32)
    mid = r1.astype(BF16)
    lo = (r1 - mid.astype(F32)).astype(BF16)
    return hi, mid, lo


def _dot_exact01(x, e):
    hi, mid, lo = _split3(x)
    d = lambda a: jnp.dot(a, e, preferred_element_type=F32)
    return d(hi) + d(mid) + d(lo)


def _kv_proj_kernel(h_ref, w_ref, b_ref, c_ref, s1_ref, s2_ref, *refs, head_major):
    if head_major:
        o_ref, hm_ref = refs
    else:
        o_ref, = refs
    j = pl.program_id(1)
    acc = jnp.dot(h_ref[...], w_ref[...], preferred_element_type=F32) + b_ref[...]

    def emit(rope):
        c, s1, s2 = c_ref[...], s1_ref[...], s2_ref[...]
        for g in range(KV_GROUPS):
            a = acc[:, g * LANES:(g + 1) * LANES]
            if rope:
                a = a * c + pltpu.roll(a, ROPE_HALF, 1) * s1 + pltpu.roll(a, LANES - ROPE_HALF, 1) * s2
            o_ref[:, 0, g, :] = a
            if head_major:
                hm_ref[0, g] = a.astype(BF16)

    pl.when(j == 0)(functools.partial(emit, True))
    pl.when(j != 0)(functools.partial(emit, False))


def _kv_proj(h, w, b, rope, *, col0, tm, seq_len=None):
    m, d = h.shape
    tn = KV_GROUPS * HEAD_DIM
    jb = col0 // tn
    period = rope[0].shape[0] // tm
    assert m % tm == 0 and col0 % tn == 0 and rope[0].shape[0] % tm == 0
    in_specs = [
        pl.BlockSpec((tm, d), lambda i, j: (i, 0)),
        pl.BlockSpec((d, tn), lambda i, j: (0, j + jb)),
        pl.BlockSpec((1, tn), lambda i, j: (0, j + jb)),
    ] + [pl.BlockSpec((tm, LANES), lambda i, j: (i % period, 0))] * 3
    out_shape = [jax.ShapeDtypeStruct((m, 2, KV_GROUPS, HEAD_DIM), F32)]
    out_specs = [pl.BlockSpec((tm, 1, KV_GROUPS, HEAD_DIM), lambda i, j: (i, j, 0, 0))]
    if seq_len is not None:
        per = seq_len // tm
        assert seq_len % tm == 0
        out_shape.append(jax.ShapeDtypeStruct((m // seq_len, KG, seq_len, HEAD_DIM), BF16))
        out_specs.append(pl.BlockSpec((1, KV_GROUPS, tm, HEAD_DIM), lambda i, j: (i // per, j, i % per, 0)))
    return pl.pallas_call(
        functools.partial(_kv_proj_kernel, head_major=seq_len is not None),
        grid=(m // tm, 2),
        in_specs=in_specs,
        out_specs=out_specs,
        out_shape=out_shape,
        compiler_params=_cparams(("parallel", "arbitrary")),
    )(h, w, b, *rope)


def _proj_kernel(h_ref, w_ref, b_ref, *refs, tn, rope, out_scale):
    if rope:
        c_ref, s1_ref, s2_ref, o_ref = refs
    else:
        o_ref, = refs
    acc = jnp.dot(h_ref[...], w_ref[...], preferred_element_type=F32) + b_ref[...]
    if not rope:
        o_ref[...] = (acc * out_scale if out_scale != 1.0 else acc).astype(o_ref.dtype)
        return
    c, s1, s2 = c_ref[...], s1_ref[...], s2_ref[...]
    for hh in range(tn // LANES):
        a = acc[:, hh * LANES:(hh + 1) * LANES]
        r = a * c + pltpu.roll(a, ROPE_HALF, 1) * s1 + pltpu.roll(a, LANES - ROPE_HALF, 1) * s2
        o_ref[:, hh * LANES:(hh + 1) * LANES] = (r * out_scale if out_scale != 1.0 else r).astype(o_ref.dtype)


def _proj(h, w, b, *, col0, ncols, tm, tn, out_dtype, rope=None, out_scale=1.0):
    m, d = h.shape
    assert m % tm == 0 and ncols % tn == 0 and col0 % tn == 0
    jb = col0 // tn
    in_specs = [
        pl.BlockSpec((tm, d), lambda i, j: (i, 0)),
        pl.BlockSpec((d, tn), lambda i, j: (0, j + jb)),
        pl.BlockSpec((1, tn), lambda i, j: (0, j + jb)),
    ]
    args = [h, w, b]
    if rope is not None:
        period = rope[0].shape[0] // tm
        assert rope[0].shape[0] % tm == 0
        for tab in rope:
            in_specs.append(pl.BlockSpec((tm, LANES), lambda i, j: (i % period, 0)))
            args.append(tab)
    return pl.pallas_call(
        functools.partial(_proj_kernel, tn=tn, rope=rope is not None, out_scale=out_scale),
        grid=(m // tm, ncols // tn),
        in_specs=in_specs,
        out_specs=pl.BlockSpec((tm, tn), lambda i, j: (i, j)),
        out_shape=jax.ShapeDtypeStruct((m, ncols), out_dtype),
        compiler_params=_cparams(("parallel", "parallel")),
    )(*args)


def _norm_small_proj_kernel(x_ref, g_ref, w_ref, b_ref, o_ref, h_ref):
    xf = x_ref[...]
    h = xf * lax.rsqrt(jnp.mean(xf * xf, axis=-1, keepdims=True) + EPS) * g_ref[...]
    h_hi, h_mid, _ = _split3(h)
    w_hi, w_mid, _ = _split3(w_ref[...])
    d = lambda a, b: jnp.dot(a, b, preferred_element_type=F32)
    o_ref[...] = d(h_hi, w_hi) + (d(h_hi, w_mid) + d(h_mid, w_hi)) + b_ref[...]
    h_ref[...] = h_hi


def _norm_small_proj(x, g, w, b, *, tm):
    m, d = x.shape
    n = w.shape[1]
    return pl.pallas_call(
        _norm_small_proj_kernel,
        grid=(m // tm,),
        in_specs=[pl.BlockSpec((tm, d), lambda i: (i, 0)), pl.BlockSpec((1, d), lambda i: (0, 0)),
                  pl.BlockSpec((d, n), lambda i: (0, 0)), pl.BlockSpec((1, n), lambda i: (0, 0))],
        out_specs=[pl.BlockSpec((tm, n), lambda i: (i, 0)), pl.BlockSpec((tm, d), lambda i: (i, 0))],
        out_shape=[jax.ShapeDtypeStruct((m, n), F32), jax.ShapeDtypeStruct((m, d), BF16)],
        compiler_params=_cparams(("parallel",)),
    )(x, g, w, b)


def _rope_tables(pos):
    inv = ROPE_THETA ** (-jnp.arange(ROPE_HALF, dtype=F32) / ROPE_HALF)
    ang = pos.astype(F32)[:, None] * inv
    cos, sin = jnp.cos(ang), jnp.sin(ang)
    p = pos.shape[0]
    rest = HEAD_DIM - ROPE_DIM
    c = jnp.concatenate([cos, cos, jnp.ones((p, rest), F32)], axis=1)
    s1 = jnp.concatenate([jnp.zeros((p, ROPE_HALF), F32), sin, jnp.zeros((p, rest), F32)], axis=1)
    s2 = jnp.concatenate([-sin, jnp.zeros((p, ROPE_HALF + rest), F32)], axis=1)
    return c, s1, s2


def _log_sigmoid(x):
    return jnp.minimum(x, 0.0) - jnp.log1p(jnp.exp(-jnp.abs(x)))


def _mlstm_kernel(*refs, L, zero_init):
    if zero_init:
        (q_ref, k_ref, v_ref, o_ref, gc_ref, gr_ref, cwq_ref, cwk_ref, cbq_ref, cbk_ref, cvq_ref, cvk_ref, nw_ref,
         h_ref, c_ref, n_ref, m_ref, xq_scr, xk_scr) = refs
    else:
        (q_ref, k_ref, v_ref, o_ref, gc_ref, gr_ref, cwq_ref, cwk_ref, cbq_ref, cbk_ref, cvq_ref, cvk_ref, nw_ref,
         c0_ref, n0_ref, m0_ref, h_ref, c_ref, n_ref, m_ref, xq_scr, xk_scr) = refs
    ci = pl.program_id(2)
    PAD = 8

    @pl.when(ci == 0)
    def _():
        if zero_init:
            c_ref[...] = jnp.zeros_like(c_ref)
            n_ref[...] = jnp.zeros_like(n_ref)
            m_ref[...] = jnp.zeros_like(m_ref)
        else:
            c_ref[...] = c0_ref[...]
            n_ref[...] = n0_ref[...]
            m_ref[...] = m0_ref[...]
        xq_scr[0:PAD, :] = cvq_ref[0]
        xk_scr[0:PAD, :] = cvk_ref[0]

    xq_scr[PAD:PAD + L, :] = q_ref[...]
    xk_scr[PAD:PAD + L, :] = k_ref[...]

    def conv_silu(x_scr, w_ref, b_ref):
        acc = b_ref[...] + jnp.zeros((L, M_HEAD_DIM), F32)
        for j in range(CONV_W):
            acc = acc + x_scr[pl.ds(PAD - (CONV_W - 1) + j, L), :] * w_ref[j:j + 1, :]
        return acc * jax.nn.sigmoid(acc)

    qs = conv_silu(xq_scr, cwq_ref, cbq_ref)
    ks = conv_silu(xk_scr, cwk_ref, cbk_ref) * (M_HEAD_DIM ** -0.5)
    xq_scr[0:PAD, :] = xq_scr[L:L + PAD, :]
    xk_scr[0:PAD, :] = xk_scr[L:L + PAD, :]

    gc = gc_ref[0, 0]
    gr = gr_ref[0, 0]
    i_col, f_col = gc[:, 0:1], _log_sigmoid(gc[:, 1:2])
    i_row, f_row = gr[0:1, :], _log_sigmoid(gr[1:2, :])
    t_idx = lax.broadcasted_iota(jnp.int32, (L, L), 0)
    s_idx = lax.broadcasted_iota(jnp.int32, (L, L), 1)
    tril = s_idx <= t_idx
    b_col = jnp.sum(jnp.where(tril, f_row, 0.0), axis=1, keepdims=True)
    b_row = jnp.sum(jnp.where(t_idx <= s_idx, f_col, 0.0), axis=0, keepdims=True)
    m_prev = m_ref[0, 0]
    dmat = jnp.where(tril, b_col - b_row + i_row, NEG_BIG)
    m_t = jnp.maximum(m_prev + b_col, jnp.max(dmat, axis=1, keepdims=True))
    w = jnp.exp(dmat - m_t)
    inter = jnp.exp(m_prev + b_col - m_t)

    c_prev = c_ref[0, 0]
    n_prev = n_ref[0, 0]
    qb, kb, vb = qs.astype(BF16), ks.astype(BF16), v_ref[...].astype(BF16)
    s = lax.dot_general(qb, kb, (((1,), (1,)), ((), ())), preferred_element_type=F32) * w
    num = inter * jnp.dot(qb, c_prev.astype(BF16), preferred_element_type=F32) \
        + jnp.dot(s.astype(BF16), vb, preferred_element_type=F32)
    den = inter * jnp.sum(qs * n_prev, axis=1, keepdims=True) + jnp.sum(s, axis=1, keepdims=True)
    h = num / jnp.maximum(jnp.abs(den), jnp.exp(-m_t))

    m_new = m_t[L - 1:L, :]
    b_last = b_col[L - 1:L, :]
    decay = jnp.exp(m_prev + b_last - m_new)
    g_col = jnp.exp(i_col + b_last - b_col - m_new)
    kg = ks * g_col
    c_ref[0, 0] = decay * c_prev + lax.dot_general(kg.astype(BF16), vb, (((0,), (0,)), ((), ())),
                                                   preferred_element_type=F32)
    n_ref[0, 0] = decay * n_prev + jnp.sum(kg, axis=0, keepdims=True)
    m_ref[0, 0] = m_new

    ho = jax.nn.sigmoid(o_ref[...]) * h
    mu = jnp.mean(ho, axis=1, keepdims=True)
    var = jnp.mean(jnp.square(ho - mu), axis=1, keepdims=True)
    h_ref[...] = ((ho - mu) * lax.rsqrt(var + EPS) * nw_ref[...]).astype(h_ref.dtype)


def _mlstm(rest, gates_col, gates_row, conv_w, conv_b, conv0, norm_w, state, *, nb, t, L):
    nc = t // L
    hb = M_HEAD_DIM
    zero_init = state is None
    row = lambda b, h, c: b * nc + c
    in_specs = [
        pl.BlockSpec((L, hb), lambda b, h, c: (row(b, h, c), R_MQK // hb + h)),
        pl.BlockSpec((L, hb), lambda b, h, c: (row(b, h, c), (R_MQK + M_INNER) // hb + h)),
        pl.BlockSpec((L, hb), lambda b, h, c: (row(b, h, c), R_MV // hb + h)),
        pl.BlockSpec((L, hb), lambda b, h, c: (row(b, h, c), R_MO // hb + h)),
        pl.BlockSpec((1, 1, L, 2), lambda b, h, c: (b, h, c, 0)),
        pl.BlockSpec((1, 1, 2, L), lambda b, h, c: (b, h, 0, c)),
        pl.BlockSpec((CONV_W, hb), lambda b, h, c: (0, h)),
        pl.BlockSpec((CONV_W, hb), lambda b, h, c: (0, M_HEADS + h)),
        pl.BlockSpec((1, hb), lambda b, h, c: (0, h)),
        pl.BlockSpec((1, hb), lambda b, h, c: (0, M_HEADS + h)),
        pl.BlockSpec((1, 8, hb), lambda b, h, c: (b, 0, h)),
        pl.BlockSpec((1, 8, hb), lambda b, h, c: (b, 0, M_HEADS + h)),
        pl.BlockSpec((1, hb), lambda b, h, c: (0, h)),
    ]
    args = [rest, rest, rest, rest, gates_col, gates_row, conv_w, conv_w, conv_b, conv_b, conv0, conv0, norm_w]
    st_specs = [
        pl.BlockSpec((1, 1, hb, hb), lambda b, h, c: (b, h, 0, 0)),
        pl.BlockSpec((1, 1, 1, hb), lambda b, h, c: (b, h, 0, 0)),
        pl.BlockSpec((1, 1, 1, 1), lambda b, h, c: (b, h, 0, 0)),
    ]
    if not zero_init:
        in_specs += st_specs
        args += list(state)
    out_shape = [
        jax.ShapeDtypeStruct((nb * t, M_INNER), BF16),
        jax.ShapeDtypeStruct((nb, M_HEADS, hb, hb), F32),
        jax.ShapeDtypeStruct((nb, M_HEADS, 1, hb), F32),
        jax.ShapeDtypeStruct((nb, M_HEADS, 1, 1), F32),
    ]
    out_specs = [pl.BlockSpec((L, hb), lambda b, h, c: (row(b, h, c), h))] + st_specs
    return pl.pallas_call(
        functools.partial(_mlstm_kernel, L=L, zero_init=zero_init),
        grid=(nb, M_HEADS, nc),
        in_specs=in_specs,
        out_specs=out_specs,
        out_shape=out_shape,
        scratch_shapes=[pltpu.VMEM((L + 8, hb), F32), pltpu.VMEM((L + 8, hb), F32)],
        compiler_params=_cparams(("parallel", "parallel", "arbitrary")),
    )(*args)


def _rms(y, g):
    return y * lax.rsqrt(jnp.mean(y * y, axis=-1, keepdims=True) + EPS) * g


def _merge_out_kernel(ga_ref, gm_ref, att_ref, mh_ref, x_ref, w_ref, g_ref, o_ref):
    f = lambda r: r[...].astype(F32)
    mixed = jax.nn.sigmoid(f(ga_ref)) * f(att_ref) + jax.nn.sigmoid(f(gm_ref)) * f(mh_ref)
    y = jnp.dot(mixed.astype(BF16), w_ref[...], preferred_element_type=F32)
    o_ref[...] = x_ref[...] + _rms(y, g_ref[...])


def _merge_out(merge, att, mh, x, w_out, g_post, *, tm):
    m, d = x.shape
    row = pl.BlockSpec((tm, d), lambda i: (i, 0))
    return pl.pallas_call(
        _merge_out_kernel,
        grid=(m // tm,),
        in_specs=[
            row,
            pl.BlockSpec((tm, d), lambda i: (i, 1)),
            row, row, row,
            pl.BlockSpec((d, d), lambda i: (0, 0)),
            pl.BlockSpec((1, d), lambda i: (0, 0)),
        ],
        out_specs=row,
        out_shape=jax.ShapeDtypeStruct((m, d), F32),
        compiler_params=_cparams(("parallel",)),
    )(merge, merge, att, mh, x, w_out, g_post)


def _mlp_kernel(x_ref, gpre_ref, wu_ref, wd_ref, gpost_ref, o_ref, h_scr, acc_scr):
    f = pl.program_id(1)

    @pl.when(f == 0)
    def _():
        h_scr[...] = _rms(x_ref[...], gpre_ref[...]).astype(BF16)
        acc_scr[...] = jnp.zeros_like(acc_scr)

    u = jnp.dot(h_scr[...], wu_ref[...], preferred_element_type=F32)
    u = jnp.square(jnp.maximum(u, 0.0))
    acc_scr[...] += jnp.dot(u.astype(BF16), wd_ref[...], preferred_element_type=F32)

    @pl.when(f == pl.num_programs(1) - 1)
    def _():
        o_ref[...] = x_ref[...] + _rms(acc_scr[...], gpost_ref[...])


def _mlp(x, g_pre, w_up, w_down, g_post, *, tm, tf):
    m, d = x.shape
    ff = w_up.shape[1]
    return pl.pallas_call(
        _mlp_kernel,
        grid=(m // tm, ff // tf),
        in_specs=[
            pl.BlockSpec((tm, d), lambda i, f: (i, 0)),
            pl.BlockSpec((1, d), lambda i, f: (0, 0)),
            pl.BlockSpec((d, tf), lambda i, f: (0, f)),
            pl.BlockSpec((tf, d), lambda i, f: (f, 0)),
            pl.BlockSpec((1, d), lambda i, f: (0, 0)),
        ],
        out_specs=pl.BlockSpec((tm, d), lambda i, f: (i, 0)),
        out_shape=jax.ShapeDtypeStruct((m, d), F32),
        scratch_shapes=[pltpu.VMEM((tm, d), BF16), pltpu.VMEM((tm, d), F32)],
        compiler_params=_cparams(("parallel", "arbitrary")),
    )(x, g_pre, w_up, w_down, g_post)


CMP_TAIL_SLOTS = 16


def _pe_bias_kernel(pet_ref, w1_ref, o_ref):
    rows = []
    for k in range(2):
        acc = jnp.zeros((1, HEAD_DIM), F32)
        for l in range(CMP_BLOCK):
            acc = acc + jnp.sum(pet_ref[k][:, l:l + 1] * w1_ref[k, l], axis=0, keepdims=True)
        rows += [acc] * KV_GROUPS
    o_ref[...] = jnp.concatenate(rows, axis=0)


def _pe_bias(cmp_pe, cmp_w1):
    pet = jnp.transpose(cmp_pe, (0, 2, 1))
    return pl.pallas_call(
        _pe_bias_kernel,
        out_shape=jax.ShapeDtypeStruct((KG, HEAD_DIM), F32),
        compiler_params=pltpu.CompilerParams(vmem_limit_bytes=VMEM_LIMIT),
    )(pet, cmp_w1)


def _cmp_kernel(pt_ref, *refs, P, has_tail):
    page_refs = refs[:P]
    if has_tail:
        tail_ref, wc_ref, w2_ref, peb_ref, o_ref, ot_ref, carry, res_scr = refs[P:]
    else:
        wc_ref, w2_ref, peb_ref, o_ref, carry, res_scr = refs[P:]
    p = pl.program_id(1)

    def store_head_major(res, out_ref, n_out):
        n = res.shape[0] // KG
        res_scr[0:n * KG, :] = res
        for kg in range(KG):
            rows = res_scr[pl.ds(kg, n, stride=KG), :]
            if n_out > n:
                rows = jnp.concatenate([rows, jnp.zeros((n_out - n, HEAD_DIM), F32)], axis=0)
            out_ref[0, kg] = rows.astype(out_ref.dtype)

    @pl.when(p == 0)
    def _():
        carry[...] = jnp.zeros_like(carry)

    def is_k(rows):
        return (lax.broadcasted_iota(jnp.int32, (rows, 1), 0) % KG) < KV_GROUPS

    def half_proj(xs):
        rows = xs[0].shape[0]
        lhs = jnp.concatenate([x.astype(BF16) for x in xs], axis=1)
        acc = jnp.dot(lhs, wc_ref[...], preferred_element_type=F32)
        ik = is_k(rows)
        a = jnp.where(ik, acc[:, 0:128], acc[:, 256:384])
        b = jnp.where(ik, acc[:, 128:256], acc[:, 384:512])
        return a, b

    def finish(a_prev, b):
        rows = b.shape[0]
        peb = jnp.concatenate([peb_ref[...]] * (rows // KG), axis=0)
        hid = jax.nn.gelu(a_prev + b + peb)
        o2 = jnp.dot(hid.astype(BF16), w2_ref[...], preferred_element_type=F32)
        return jnp.where(is_k(rows), o2[:, 0:128], o2[:, 128:256])

    xs = [jnp.concatenate([r[0, :, l].reshape(CMP_STRIDE // 2 * KG, HEAD_DIM) for r in page_refs], axis=0)
          for l in range(CMP_STRIDE)]
    a, b = half_proj(xs)
    a_prev = jnp.concatenate([carry[...], a[:-KG]], axis=0)
    store_head_major(finish(a_prev, b), o_ref, o_ref.shape[2])
    carry[...] = a[-KG:]

    if has_tail:
        @pl.when(p == pl.num_programs(1) - 1)
        def _():
            nct = tail_ref.shape[1]
            xt = [tail_ref[0, :, l].reshape(nct * KG, HEAD_DIM) for l in range(CMP_STRIDE)]
            at, bt = half_proj(xt)
            ap = jnp.concatenate([a[-KG:], at[:-KG]], axis=0)
            store_head_major(finish(ap, bt), ot_ref, ot_ref.shape[2])


def _cmp_summaries(pool, page_table, tail, wc, w2c, peb, *, P):
    ns, npg = page_table.shape
    cpp = PAGE_SIZE // CMP_STRIDE
    assert npg % P == 0
    has_tail = tail is not None
    blk = (1, cpp, CMP_STRIDE, KG, HEAD_DIM)
    in_specs = [pl.BlockSpec(blk, functools.partial(lambda s, p, pt, u: (pt[s * npg + p * P + u], 0, 0, 0, 0), u=u))
                for u in range(P)]
    args = [pool] * P
    if has_tail:
        nct = tail.shape[1]
        in_specs.append(pl.BlockSpec((1, nct, CMP_STRIDE, KG, HEAD_DIM), lambda s, p, pt: (s, 0, 0, 0, 0)))
        args.append(tail)
    in_specs += [
        pl.BlockSpec(wc.shape, lambda s, p, pt: (0, 0)),
        pl.BlockSpec(w2c.shape, lambda s, p, pt: (0, 0)),
        pl.BlockSpec(peb.shape, lambda s, p, pt: (0, 0)),
    ]
    args += [wc, w2c, peb]
    out_shape = [jax.ShapeDtypeStruct((ns, KG, npg * cpp, HEAD_DIM), BF16)]
    out_specs = [pl.BlockSpec((1, KG, P * cpp, HEAD_DIM), lambda s, p, pt: (s, 0, p, 0))]
    if has_tail:
        out_shape.append(jax.ShapeDtypeStruct((ns, KG, CMP_TAIL_SLOTS, HEAD_DIM), BF16))
        out_specs.append(pl.BlockSpec((1, KG, CMP_TAIL_SLOTS, HEAD_DIM), lambda s, p, pt: (s, 0, 0, 0)))
    return pl.pallas_call(
        functools.partial(_cmp_kernel, P=P, has_tail=has_tail),
        grid_spec=pltpu.PrefetchScalarGridSpec(
            num_scalar_prefetch=1, grid=(ns, npg // P), in_specs=in_specs, out_specs=out_specs,
            scratch_shapes=[pltpu.VMEM((KG, HEAD_DIM), F32), pltpu.VMEM((P * cpp * KG, HEAD_DIM), F32)]),
        out_shape=out_shape,
        compiler_params=_cparams(("parallel", "arbitrary")),
    )(page_table.reshape(-1), *args)


def _pack_cmp_weights(cmp_w1, cmp_w2):
    def cols(l):
        return jnp.concatenate([cmp_w1[0, l], cmp_w1[0, CMP_STRIDE + l], cmp_w1[1, l], cmp_w1[1, CMP_STRIDE + l]], axis=1)
    wc = jnp.concatenate([cols(l) for l in range(CMP_STRIDE)], axis=0).astype(BF16)
    w2c = jnp.concatenate([cmp_w2[0], cmp_w2[1]], axis=1).astype(BF16)
    return wc, w2c


def _overlap_matrix(n_slots, n_sb, width):
    i = jnp.arange(n_slots)[:, None] - 1
    j = jnp.arange(width)[None, :]
    ov = (i >= 0) & (j < n_sb) & (i * CMP_STRIDE < (j + 1) * SLC_BLOCK) & (i * CMP_STRIDE + CMP_BLOCK > j * SLC_BLOCK)
    return ov.astype(BF16)


def _masked_softmax(s, mask):
    s = jnp.where(mask, s, -jnp.inf)
    m = jnp.max(s, axis=-1, keepdims=True)
    m = jnp.where(m > -jnp.inf, m, 0.0)
    e = jnp.exp2(s - m)
    return e / jnp.maximum(jnp.sum(e, axis=-1, keepdims=True), 1e-30)


def _dot_nt(a, b):
    return lax.dot_general(a, b, (((1,), (1,)), ((), ())), preferred_element_type=F32)


def _attn_prompt_kernel(q_ref, kc_ref, vc_ref, ks_ref, vs_ref, kw_ref, vw_ref, gate_ref, ov_ref, e_ref, ge_ref,
                        o_ref, *, tq, n_sb, sel_step):
    i = pl.program_id(2)
    hpg = HEADS_PER_GROUP
    rows = hpg * tq
    s_len = ks_ref.shape[2]

    q = jnp.concatenate([q_ref[:, h * HEAD_DIM:(h + 1) * HEAD_DIM] for h in range(hpg)], axis=0)
    t0 = i * tq
    t_row = t0 + lax.broadcasted_iota(jnp.int32, (rows, 1), 0) % tq
    t_tok = t0 + lax.broadcasted_iota(jnp.int32, (tq, 1), 0)

    def compressed_branch():
        n_slots = kc_ref.shape[2]
        s_c = _dot_nt(q, kc_ref[0, 0])
        slot = lax.broadcasted_iota(jnp.int32, (1, n_slots), 1)
        mask_c = (slot >= 1) & ((slot - 1) * CMP_STRIDE + CMP_BLOCK - 1 <= t_row)
        p_c = _masked_softmax(s_c, mask_c)
        o_c = jnp.dot(p_c.astype(BF16), vc_ref[0, 0], preferred_element_type=F32)
        p_grp = p_c[0:tq]
        for h in range(1, hpg):
            p_grp = p_grp + p_c[h * tq:(h + 1) * tq]
        return o_c, p_grp

    def select_blocks(p_grp):
        score = _dot_exact01(p_grp, ov_ref[...])
        jl = lax.broadcasted_iota(jnp.int32, (1, LANES), 1)
        cur = t_tok // SLC_BLOCK
        valid = jl <= cur
        forced = (jl == 0) | (jl == cur) | (jl == cur - 1)
        score = jnp.where(forced, FORCE_SCORE, score)
        score = jnp.where(valid, score, -jnp.inf)
        rank = jnp.zeros((tq, LANES), F32)
        for j2 in range(n_sb):
            col = score[:, j2:j2 + 1]
            later = jnp.where(jl > j2, 1.0, 0.0)
            rank = rank + jnp.where(col > score, 1.0, 0.0) + jnp.where(col == score, later, 0.0)
        return jnp.where((rank < N_SELECT) & valid, 1.0, 0.0).astype(BF16)

    def attend(k, v, keep):
        bias = jnp.where(keep, 0.0, NEG_BIG)
        s = _dot_nt(q, k)
        es = []
        for h in range(hpg):
            sh = s[h * tq:(h + 1) * tq] + bias
            es.append(jnp.exp2(sh - jnp.max(sh, axis=1, keepdims=True)).astype(BF16))
        ones = (lax.broadcasted_iota(jnp.int32, (v.shape[0], LANES), 1) == 0).astype(BF16)
        oa = jnp.dot(jnp.concatenate(es, axis=0), jnp.concatenate([v, ones], axis=1), preferred_element_type=F32)
        return oa[:, :HEAD_DIM] / oa[:, HEAD_DIM:HEAD_DIM + 1]

    def window_branch():
        band = WINDOW + tq
        start = pl.multiple_of(jnp.maximum(t0 - WINDOW, 0), tq)
        kpos = start + lax.broadcasted_iota(jnp.int32, (1, band), 1)
        return attend(kw_ref[0, 0, pl.ds(start, band), :], vw_ref[0, 0, pl.ds(start, band), :],
                      (kpos <= t_tok) & (kpos > t_tok - WINDOW))

    def combine(o_c, o_s, o_w):
        sig = jax.nn.sigmoid(gate_ref[...])
        g_c, g_s, g_w = (_dot_exact01(sig, ge_ref[k, 0]) for k in range(3))
        for h in range(hpg):
            hs = slice(h * HEAD_DIM, (h + 1) * HEAD_DIM)
            rs = slice(h * tq, (h + 1) * tq)
            o_ref[:, hs] = (g_c[:, hs] * o_c[rs] + g_s[:, hs] * o_s[rs] + g_w[:, hs] * o_w[rs]).astype(o_ref.dtype)

    for var in range(s_len // sel_step):
        width = (var + 1) * sel_step

        @pl.when((t0 + tq - 1) // sel_step == var)
        def _(width=width):
            o_c, p_grp = compressed_branch()
            kpos = lax.broadcasted_iota(jnp.int32, (1, width), 1)
            keep = kpos <= t_tok
            if width > N_SELECT * SLC_BLOCK:
                selx = jnp.dot(select_blocks(p_grp), e_ref[:, 0:width], preferred_element_type=F32)
                keep = keep & (selx > 0.5)
            o_s = attend(ks_ref[0, 0, 0:width, :], vs_ref[0, 0, 0:width, :], keep)
            combine(o_c, o_s, window_branch())


def _gate_expand(width_heads):
    c = jnp.arange(LANES)[None, :, None]
    k = jnp.arange(3)[:, None, None]
    hh = (jnp.arange(width_heads * HEAD_DIM) // HEAD_DIM)[None, None, :]
    return (c == k * N_HEADS + hh).astype(BF16)


def _attn_prompt(q, kvc_slots, kvs, kvw, small, *, nb, s_len, tq=128, sel_step=256):
    nq = s_len // tq
    n_slots = kvc_slots.shape[2]
    n_sb = s_len // SLC_BLOCK
    assert n_sb <= LANES and n_slots <= LANES and s_len % sel_step == 0 and sel_step % tq == 0
    assert s_len >= WINDOW + tq and WINDOW % tq == 0
    hd = HEAD_DIM
    gw = HEADS_PER_GROUP * hd
    ov = _overlap_matrix(n_slots, n_sb, LANES)
    e = (jnp.arange(LANES)[:, None] == (jnp.arange(s_len) // SLC_BLOCK)[None, :]).astype(BF16)
    ge = _gate_expand(N_HEADS).reshape(3, LANES, KV_GROUPS, gw).transpose(0, 2, 1, 3)
    kspec = lambda kv: pl.BlockSpec((1, 1, s_len, hd), lambda b, g, i: (b, kv * KV_GROUPS + g, 0, 0))
    cspec = lambda kv: pl.BlockSpec((1, 1, n_slots, hd), lambda b, g, i: (b, kv * KV_GROUPS + g, 0, 0))
    return pl.pallas_call(
        functools.partial(_attn_prompt_kernel, tq=tq, n_sb=n_sb, sel_step=sel_step),
        grid=(nb, KV_GROUPS, nq),
        in_specs=[
            pl.BlockSpec((tq, gw), lambda b, g, i: (b * nq + i, g)),
            cspec(0), cspec(1), kspec(0), kspec(1), kspec(0), kspec(1),
            pl.BlockSpec((tq, LANES), lambda b, g, i: (b * nq + i, 0)),
            pl.BlockSpec(ov.shape, lambda b, g, i: (0, 0)),
            pl.BlockSpec(e.shape, lambda b, g, i: (0, 0)),
            pl.BlockSpec((3, 1, LANES, gw), lambda b, g, i: (0, g, 0, 0)),
        ],
        out_specs=pl.BlockSpec((tq, gw), lambda b, g, i: (b * nq + i, g)),
        out_shape=jax.ShapeDtypeStruct((nb * s_len, N_HEADS * hd), BF16),
        compiler_params=_cparams(("parallel", "parallel", "parallel")),
    )(q, kvc_slots, kvc_slots, kvs, kvs, kvw, kvw, small, ov, e, ge)


T_PAD = 8


def _attn_sample_a_kernel(q_ref, kcm_ref, kct_ref, wc_ref, wn_ref, ov_ref, oc_ref, ow_ref, idx_ref,
                          *, past, t_new, n_blocks, n_sb, wb):
    hpg = HEADS_PER_GROUP
    rows = hpg * T_PAD
    t_pos = past + lax.broadcasted_iota(jnp.int32, (rows, 1), 0) % T_PAD
    tp8 = past + lax.broadcasted_iota(jnp.int32, (T_PAD, 1), 0)
    n_slots = kcm_ref.shape[2] + kct_ref.shape[2]
    slot = lax.broadcasted_iota(jnp.int32, (1, n_slots), 1)
    mask_c = (slot >= 1) & (slot <= n_blocks) & ((slot - 1) * CMP_STRIDE + CMP_BLOCK - 1 <= t_pos)
    width = ov_ref.shape[1]
    jl = lax.broadcasted_iota(jnp.int32, (1, width), 1)
    jf = jl.astype(F32)
    cur = tp8 // SLC_BLOCK
    valid = (jl <= cur) & (jl < n_sb)
    forced = (jl == 0) | (jl == cur) | (jl == cur - 1)
    lane = lax.broadcasted_iota(jnp.int32, (1, LANES), 1)
    wj = lax.broadcasted_iota(jnp.int32, (1, wb + T_PAD), 1)
    kp = past - wb + wj
    mask_w = (kp <= t_pos) & (kp > t_pos - WINDOW) & (kp >= 0) & (wj < wb + t_new)

    for g in range(KV_GROUPS):
        q = q_ref[0, g].astype(BF16)
        kc = jnp.concatenate([kcm_ref[0, g], kct_ref[0, g]], axis=0)
        vc = jnp.concatenate([kcm_ref[0, KV_GROUPS + g], kct_ref[0, KV_GROUPS + g]], axis=0)
        p_c = _masked_softmax(_dot_nt(q, kc), mask_c)
        oc_ref[0, g] = jnp.dot(p_c.astype(BF16), vc, preferred_element_type=F32)
        p_grp = p_c[0:T_PAD]
        for h in range(1, hpg):
            p_grp = p_grp + p_c[h * T_PAD:(h + 1) * T_PAD]
        score = _dot_exact01(p_grp, ov_ref[...])
        score = jnp.where(forced, FORCE_SCORE, score)
        score = jnp.where(valid, score, -jnp.inf)
        idx = jnp.zeros((T_PAD, LANES), jnp.int32)
        for k in range(N_SELECT):
            mx = jnp.max(score, axis=1, keepdims=True)
            am = jnp.min(jnp.where(score == mx, jf, 1e9), axis=1, keepdims=True)
            pick = jnp.where(mx > -jnp.inf, am, -1.0).astype(jnp.int32)
            idx = jnp.where(lane == k, pick, idx)
            score = jnp.where(jf == am, -jnp.inf, score)
        idx_ref[0, g] = idx

        kw = jnp.concatenate([wc_ref[0, :, 0, g, :], wn_ref[0, :, 0, g, :]], axis=0).astype(BF16)
        vw = jnp.concatenate([wc_ref[0, :, 1, g, :], wn_ref[0, :, 1, g, :]], axis=0).astype(BF16)
        p_w = _masked_softmax(_dot_nt(q, kw), mask_w)
        ow_ref[0, g] = jnp.dot(p_w.astype(BF16), vw, preferred_element_type=F32)


def _attn_sample_a(q4, slots_main, slots_tail, win_cache, win_new, *, past, t_new, n_blocks, n_sb):
    ns = q4.shape[0]
    wb = win_cache.shape[1]
    n_slots = slots_main.shape[2] + slots_tail.shape[2]
    width = -(-n_sb // LANES) * LANES
    ov = _overlap_matrix(n_slots, n_sb, width)
    rows = HEADS_PER_GROUP * T_PAD
    full = lambda a: pl.BlockSpec((1,) + a.shape[1:], lambda s: (s,) + (0,) * (a.ndim - 1))
    o_sds = jax.ShapeDtypeStruct((ns, KV_GROUPS, rows, HEAD_DIM), F32)
    o_spec = pl.BlockSpec((1, KV_GROUPS, rows, HEAD_DIM), lambda s: (s, 0, 0, 0))
    return pl.pallas_call(
        functools.partial(_attn_sample_a_kernel, past=past, t_new=t_new, n_blocks=n_blocks, n_sb=n_sb, wb=wb),
        grid=(ns,),
        in_specs=[full(q4), full(slots_main), full(slots_tail), full(win_cache), full(win_new),
                  pl.BlockSpec(ov.shape, lambda s: (0, 0))],
        out_specs=[o_spec, o_spec, pl.BlockSpec((1, KV_GROUPS, T_PAD, LANES), lambda s: (s, 0, 0, 0))],
        out_shape=[o_sds, o_sds, jax.ShapeDtypeStruct((ns, KV_GROUPS, T_PAD, LANES), jnp.int32)],
        compiler_params=_cparams(("parallel",)),
    )(q4, slots_main, slots_tail, win_cache, win_new, ov)


def _attn_sample_b_kernel(idx_ref, pt_ref, q_ref, pool_ref, tail_ref, o_ref, buf, sem, *, past, nb_past, t_new, npg):
    n = N_SELECT
    G = KV_GROUPS
    bpp = PAGE_SIZE // SLC_BLOCK
    nblk = t_new * n
    s, g = pl.program_id(0), pl.program_id(1)
    step = s * G + g
    nsteps = pl.num_programs(0) * G
    slot = step % 2

    def start_copies(step_, slot_):
        s_, g_ = step_ // G, step_ % G

        def body(j, carry):
            jj = idx_ref[step_ * nblk + j]
            jp = jnp.clip(jj, 0, nb_past - 1)
            page = pt_ref[s_ * npg + jp // bpp]
            r0 = pl.multiple_of((jp % bpp) * SLC_BLOCK, SLC_BLOCK)
            for kv in range(2):
                @pl.when(jj < nb_past)
                def _(kv=kv):
                    pltpu.make_async_copy(pool_ref.at[page, pl.ds(r0, SLC_BLOCK), kv, g_], buf.at[slot_, kv, j],
                                          sem.at[slot_]).start()

                @pl.when(jj >= nb_past)
                def _(kv=kv):
                    pltpu.make_async_copy(tail_ref.at[s_, :, kv, g_], buf.at[slot_, kv, j], sem.at[slot_]).start()
            return carry

        lax.fori_loop(0, nblk, body, 0)

    def wait_copies(slot_):
        def body(j, carry):
            for kv in range(2):
                pltpu.make_async_copy(tail_ref.at[0, :, kv, 0], buf.at[slot_, kv, j], sem.at[slot_]).wait()
            return carry

        lax.fori_loop(0, nblk, body, 0)

    @pl.when(step == 0)
    def _():
        start_copies(step, slot)

    @pl.when(step + 1 < nsteps)
    def _():
        start_copies(step + 1, 1 - slot)

    wait_copies(slot)

    q = q_ref[0, 0].astype(BF16)
    rows = q.shape[0]
    t8 = lax.broadcasted_iota(jnp.int32, (rows, 1), 0) % T_PAD
    jl = lax.broadcasted_iota(jnp.int32, (1, n * SLC_BLOCK), 1)
    kslot = jl // SLC_BLOCK
    out = jnp.zeros((rows, HEAD_DIM), F32)
    for t in range(t_new):
        start = jnp.zeros((1, n * SLC_BLOCK), jnp.int32)
        ok = jnp.zeros((1, n * SLC_BLOCK), jnp.int32)
        for k in range(n):
            jj = idx_ref[step * nblk + t * n + k]
            start = jnp.where(kslot == k, jj * SLC_BLOCK, start)
            ok = jnp.where(kslot == k, (jj >= 0).astype(jnp.int32), ok)
        kpos = start + jl % SLC_BLOCK
        mask = (ok > 0) & (kpos <= past + t)
        kk = buf[slot, 0, t * n:(t + 1) * n].reshape(n * SLC_BLOCK, HEAD_DIM).astype(BF16)
        vv = buf[slot, 1, t * n:(t + 1) * n].reshape(n * SLC_BLOCK, HEAD_DIM).astype(BF16)
        p = _masked_softmax(_dot_nt(q, kk), mask)
        o_t = jnp.dot(p.astype(BF16), vv, preferred_element_type=F32)
        out = jnp.where(t8 == t, o_t, out)
    o_ref[0, 0] = out


def _attn_sample_b(idx_flat, pt_flat, q4, tail, pool, *, past, npg, t_new):
    ns = q4.shape[0]
    G = KV_GROUPS
    rows = q4.shape[2]
    assert tail.shape[1] == SLC_BLOCK
    qspec = pl.BlockSpec((1, 1, rows, HEAD_DIM), lambda s, g, idx, pt: (s, g, 0, 0))
    return pl.pallas_call(
        functools.partial(_attn_sample_b_kernel, past=past, nb_past=past // SLC_BLOCK, t_new=t_new, npg=npg),
        grid_spec=pltpu.PrefetchScalarGridSpec(
            num_scalar_prefetch=2, grid=(ns, G),
            in_specs=[qspec, pl.BlockSpec(memory_space=pl.ANY), pl.BlockSpec(memory_space=pl.ANY)],
            out_specs=qspec,
            scratch_shapes=[pltpu.VMEM((2, 2, t_new * N_SELECT, SLC_BLOCK, HEAD_DIM), F32),
                            pltpu.SemaphoreType.DMA((2,))]),
        out_shape=jax.ShapeDtypeStruct((ns, G, rows, HEAD_DIM), F32),
        compiler_params=_cparams(("arbitrary", "arbitrary")),
    )(idx_flat, pt_flat, q4, pool, tail)


def _nsa_combine_kernel(oc_ref, os_ref, ow_ref, gate_ref, ge_ref, o_ref):
    sig = jax.nn.sigmoid(gate_ref[...])
    o_ref[...] = (_dot_exact01(sig, ge_ref[0]) * oc_ref[...] + _dot_exact01(sig, ge_ref[1]) * os_ref[...]
                  + _dot_exact01(sig, ge_ref[2]) * ow_ref[...])


def _nsa_combine(oc, os_, ow, small):
    ge = _gate_expand(N_HEADS)
    return pl.pallas_call(
        _nsa_combine_kernel,
        out_shape=jax.ShapeDtypeStruct(oc.shape, F32),
        compiler_params=pltpu.CompilerParams(vmem_limit_bytes=VMEM_LIMIT),
    )(oc, os_, ow, small, ge)


def _repack_kernel(a_ref, b_ref, o_ref, *, shift):
    if shift == 0:
        o_ref[...] = a_ref[...].astype(o_ref.dtype)
        return
    tn = a_ref.shape[1]
    cat = jnp.concatenate([a_ref[...], b_ref[...]], axis=1)
    o_ref[...] = pltpu.roll(cat, 2 * tn - shift, 1)[:, :tn].astype(o_ref.dtype)


def _repack(w, *, col_start, ncols, tk=512, tn=512):
    k, n = w.shape
    base, shift = divmod(col_start, tn)
    last = pl.cdiv(n, tn) - 1
    assert k % tk == 0 and ncols % tn == 0 and base + ncols // tn - 1 + (shift > 0) <= last
    return pl.pallas_call(
        functools.partial(_repack_kernel, shift=shift),
        grid=(k // tk, ncols // tn),
        in_specs=[pl.BlockSpec((tk, tn), lambda i, j: (i, base + j)),
                  pl.BlockSpec((tk, tn), lambda i, j: (i, jnp.minimum(base + j + 1, last)))],
        out_specs=pl.BlockSpec((tk, tn), lambda i, j: (i, j)),
        out_shape=jax.ShapeDtypeStruct((k, ncols), BF16),
        compiler_params=_cparams(("parallel", "parallel")),
    )(w, w)


def _small_cols_kernel(a_ref, b_ref, o_ref, *, n_a, n_b):
    lane = lax.broadcasted_iota(jnp.int32, (1, LANES), 1)
    o_ref[...] = jnp.where(lane < n_a, a_ref[...], jnp.where(lane < n_a + n_b, b_ref[...], 0.0))


def _small_cols(w, *, col_a, n_a, col_b, n_b):
    k = w.shape[0]
    assert col_a % LANES == 0 and col_b % LANES == n_a and n_a + n_b <= LANES
    return pl.pallas_call(
        functools.partial(_small_cols_kernel, n_a=n_a, n_b=n_b),
        grid=(1,),
        in_specs=[pl.BlockSpec((k, LANES), lambda i: (0, col_a // LANES)),
                  pl.BlockSpec((k, LANES), lambda i: (0, col_b // LANES))],
        out_specs=pl.BlockSpec((k, LANES), lambda i: (0, 0)),
        out_shape=jax.ShapeDtypeStruct((k, LANES), F32),
        compiler_params=_cparams(("arbitrary",)),
    )(w, w)


def _pack_in_proj(w_in, b_in):
    sizes = (N_HEADS * HEAD_DIM, KV_WIDTH, KV_WIDTH, KV_WIDTH, N_GATE, 2 * M_INNER, M_INNER, M_INNER, 2 * M_HEADS,
             2 * D_MODEL)
    offs = [0]
    for sz in sizes:
        offs.append(offs[-1] + sz)
    o_attg, o_mqk, o_mif, o_merge, end = offs[4], offs[5], offs[8], offs[9], offs[10]
    pad = LANES - N_GATE - 2 * M_HEADS

    b = b_in[None, :]
    b_slabs = [b[:, :o_attg], b[:, o_mqk:o_mif], b[:, o_merge:end]]
    b_small = jnp.concatenate([b[:, o_attg:o_mqk], b[:, o_mif:o_merge], jnp.zeros((1, pad), F32)], axis=-1)
    w_slabs = [_repack(w_in, col_start=0, ncols=o_attg), _repack(w_in, col_start=o_mqk, ncols=o_mif - o_mqk),
               _repack(w_in, col_start=o_merge, ncols=end - o_merge)]
    w_small = _small_cols(w_in, col_a=o_attg, n_a=o_mqk - o_attg, col_b=o_mif, n_b=o_merge - o_mif)
    return w_slabs, b_slabs, w_small, b_small


def _project(x, pos, g_pre, packed, *, tm, seq_len=None):
    (w_att, w_rest, w_merge), (b_att, b_rest, b_merge), w_small, b_small = packed
    tabs = _rope_tables(pos)
    small, h = _norm_small_proj(x, g_pre, w_small, b_small, tm=min(tm, 512))
    q = _proj(h, w_att, b_att, col0=OFF_Q, ncols=N_HEADS * HEAD_DIM, out_dtype=BF16, tm=tm, tn=512, rope=tabs,
              out_scale=ATT_SCALE * LOG2E)
    kv = [_kv_proj(h, w_att, b_att, tabs, col0=off, tm=tm, seq_len=None if off == OFF_KVC else seq_len)
          for off in (OFF_KVC, OFF_KVS, OFF_KVW)]
    rest = _proj(h, w_rest, b_rest, col0=0, ncols=REST_WIDTH, tm=tm, tn=1024, out_dtype=F32)
    merge = _proj(h, w_merge, b_merge, col0=0, ncols=2 * D_MODEL, tm=tm, tn=1024, out_dtype=BF16)
    return q, kv, rest, merge, small


def _gate_layouts(small, nb, t, t_pad):
    mif = small[:, N_GATE:N_GATE + 2 * M_HEADS].reshape(nb, t, 2, M_HEADS)
    if t_pad > t:
        fill = jnp.broadcast_to(jnp.array([NEG_BIG, 1e4], F32)[None, None, :, None], (nb, t_pad - t, 2, M_HEADS))
        mif = jnp.concatenate([mif, fill], axis=1)
    return jnp.transpose(mif, (0, 3, 1, 2)), jnp.transpose(mif, (0, 3, 2, 1))


def _pad_rows(a, n):
    return jnp.concatenate([a, jnp.zeros((a.shape[0], n - a.shape[1]) + a.shape[2:], a.dtype)], axis=1)


def _nsa_sample(q_s, kvc_s, kvs_s, kvw_s, small_s, cache_kv_cmp, cache_kv_slc, cache_win_kv, page_table,
                wc, w2c, peb, *, DB, T):
    npg = page_table.shape[1]
    past = npg * PAGE_SIZE
    n_phys = cache_kv_cmp.shape[0]
    cpp = PAGE_SIZE // CMP_STRIDE
    t_blk = -(-T // SLC_BLOCK) * SLC_BLOCK
    assert t_blk == SLC_BLOCK
    new_rows = lambda a, n: _pad_rows(a.reshape(DB, T, 2, KV_GROUPS, HEAD_DIM), n)
    cmp_tail = new_rows(kvc_s, t_blk).reshape(DB, t_blk // CMP_STRIDE, CMP_STRIDE, KG, HEAD_DIM)
    slots_s, slots_t = _cmp_summaries(cache_kv_cmp.reshape(n_phys, cpp, CMP_STRIDE, KG, HEAD_DIM), page_table,
                                      cmp_tail, wc, w2c, peb, P=16)
    n_blocks = (past + t_blk) // CMP_STRIDE - 1
    nb_past = past // SLC_BLOCK
    n_sb = nb_past + t_blk // SLC_BLOCK
    q6 = q_s.astype(F32).reshape(DB, T, KV_GROUPS, HEADS_PER_GROUP, HEAD_DIM)
    q4 = _pad_rows(jnp.transpose(q6, (0, 2, 3, 1, 4)).reshape(DB * KV_GROUPS * HEADS_PER_GROUP, T, HEAD_DIM), T_PAD)
    q4 = q4.reshape(DB, KV_GROUPS, HEADS_PER_GROUP * T_PAD, HEAD_DIM)
    oc, ow, idx = _attn_sample_a(q4, slots_s, slots_t, cache_win_kv, new_rows(kvw_s, T_PAD),
                                 past=past, t_new=T, n_blocks=n_blocks, n_sb=n_sb)
    idx_flat = idx[:, :, :T, :N_SELECT].reshape(-1)
    os4 = _attn_sample_b(idx_flat, page_table.reshape(-1), q4, new_rows(kvs_s, SLC_BLOCK), cache_kv_slc,
                         past=past, npg=npg, t_new=T)
    tok_major = lambda o: jnp.transpose(o.reshape(DB, KV_GROUPS, HEADS_PER_GROUP, T_PAD, HEAD_DIM)[:, :, :, :T],
                                        (0, 3, 1, 2, 4)).reshape(DB * T, N_HEADS * HEAD_DIM)
    return _nsa_combine(tok_major(oc), tok_major(os4), tok_major(ow), small_s)


def kernel(x_prompt, x_sample, cache_kv_cmp, cache_kv_slc, cache_win_kv, state_mlstm_c, state_mlstm_n, state_mlstm_m, state_mlstm_conv, page_table, g_mix_pre, w_in, b_in, cmp_pe, cmp_w1, cmp_w2, conv_w, conv_b, mlstm_norm_w, w_out, g_mix_post, g_mlp_pre, w_up, w_down, g_mlp_post):
    B, S, D = x_prompt.shape
    DB, T, _ = x_sample.shape
    npg = page_table.shape[1]
    past = npg * PAGE_SIZE
    n_phys = cache_kv_cmp.shape[0]
    cpp = PAGE_SIZE // CMP_STRIDE
    kv5 = (2, KV_GROUPS, HEAD_DIM)

    packed = _pack_in_proj(w_in, b_in)
    w_out_b, w_up_b, w_down_b = w_out.astype(BF16), w_up.astype(BF16), w_down.astype(BF16)
    row = lambda v: v[None, :]
    peb = _pe_bias(cmp_pe, cmp_w1)
    wc, w2c = _pack_cmp_weights(cmp_w1, cmp_w2)

    def tail(merge, att, mh, x, tm, tm_mlp):
        x1 = _merge_out(merge, att, mh, x, w_out_b, row(g_mix_post), tm=tm)
        return _mlp(x1, row(g_mlp_pre), w_up_b, w_down_b, row(g_mlp_post), tm=tm_mlp, tf=1024)

    xp = x_prompt.reshape(B * S, D)
    q_p, kv_p, rest_p, merge_p, small_p = _project(xp, jnp.arange(S), row(g_mix_pre), packed, tm=1024, seq_len=S)
    (kvc_p,), (kvs_p, kvs_hm), (kvw_p, kvw_hm) = kv_p
    pt_p = jnp.arange(B * S // PAGE_SIZE, dtype=jnp.int32).reshape(B, S // PAGE_SIZE)
    slots_p, = _cmp_summaries(kvc_p.reshape(B * S // PAGE_SIZE, cpp, CMP_STRIDE, KG, HEAD_DIM), pt_p, None,
                              wc, w2c, peb, P=16)
    att_p = _attn_prompt(q_p, slots_p, kvs_hm, kvw_hm, small_p, nb=B, s_len=S)
    gcol, grow = _gate_layouts(small_p, B, S, S)
    mh_p, c_p, n_p, m_p = _mlstm(rest_p, gcol, grow, conv_w, row(conv_b), jnp.zeros((B, 8, 2 * M_INNER), F32),
                                 row(mlstm_norm_w), None, nb=B, t=S, L=256)
    y_p = tail(merge_p, att_p, mh_p, xp, 256, 512)
    wlen = min(WINDOW, S)
    conv_p = rest_p.reshape(B, S, REST_WIDTH)[:, S - (CONV_W - 1):, R_MQK:R_MQK + 2 * M_INNER]

    xs = x_sample.reshape(DB * T, D)
    pos_s = jnp.tile(past + jnp.arange(T), DB)
    q_s, ((kvc_s,), (kvs_s,), (kvw_s,)), rest_s, merge_s, small_s = _project(xs, pos_s, row(g_mix_pre), packed, tm=DB * T)
    att_s = _nsa_sample(q_s, kvc_s, kvs_s, kvw_s, small_s, cache_kv_cmp, cache_kv_slc, cache_win_kv, page_table,
                        wc, w2c, peb, DB=DB, T=T)

    LS = 16
    gcol_s, grow_s = _gate_layouts(small_s, DB, T, LS)
    rest_pad = _pad_rows(rest_s.reshape(DB, T, REST_WIDTH), LS).reshape(DB * LS, REST_WIDTH)
    conv0 = jnp.concatenate([jnp.zeros((DB, 8 - (CONV_W - 1), 2 * M_INNER), F32), state_mlstm_conv], axis=1)
    state = (state_mlstm_c, state_mlstm_n[:, :, None, :], state_mlstm_m[:, :, None, None])
    mh_pad, c_s, n_s, m_s = _mlstm(rest_pad, gcol_s, grow_s, conv_w, row(conv_b), conv0, row(mlstm_norm_w), state,
                                   nb=DB, t=LS, L=LS)
    mh_s = mh_pad.reshape(DB, LS, M_INNER)[:, :T].reshape(DB * T, M_INNER)
    y_s = tail(merge_s, att_s, mh_s, xs, DB * T, DB * T)
    win_s = jnp.concatenate([cache_win_kv[:, T:], kvw_s.reshape((DB, T) + kv5).astype(cache_win_kv.dtype)], axis=1)
    mqk_s = rest_s.reshape(DB, T, REST_WIDTH)[:, :, R_MQK:R_MQK + 2 * M_INNER]
    conv_s = jnp.concatenate([state_mlstm_conv, mqk_s], axis=1)[:, T:]

    return (y_p.reshape(B, S, D), y_s.reshape(DB, T, D),
            kvc_p.reshape((B, S) + kv5), kvc_s.reshape((DB, T) + kv5),
            kvs_p.reshape((B, S) + kv5), kvs_s.reshape((DB, T) + kv5),
            kvw_p.reshape((B, S) + kv5)[:, S - wlen:], win_s,
            c_p, c_s, n_p[:, :, 0], n_s[:, :, 0], m_p[:, :, 0, 0], m_s[:, :, 0, 0], conv_p, conv_s)
```

```python
import functools

import jax
import jax.numpy as jnp
from jax import lax
from jax.experimental import pallas as pl
from jax.experimental.pallas import tpu as pltpu

F32 = jnp.float32
BF16 = jnp.bfloat16

D_MODEL = 2048
N_HEADS = 16
HEAD_DIM = 128
KV_GROUPS = 4
HEADS_PER_GROUP = N_HEADS // KV_GROUPS
ROPE_DIM = HEAD_DIM // 4
ROPE_HALF = ROPE_DIM // 2
ROPE_THETA = 500000.0
CMP_STRIDE = 16
CMP_BLOCK = 2 * CMP_STRIDE
SLC_BLOCK = 64
N_SELECT = 16
WINDOW = 512
ATT_SCALE = HEAD_DIM ** -0.5
LOG2E = 1.4426950408889634
FORCE_SCORE = 1e9
M_HEADS = 4
M_INNER = D_MODEL
M_HEAD_DIM = M_INNER // M_HEADS
CONV_W = 4
NEG_BIG = -1e30
D_FF = 4 * D_MODEL
EPS = 1e-6
PAGE_SIZE = 128
KV_WIDTH = 2 * KV_GROUPS * HEAD_DIM
KG = 2 * KV_GROUPS

LANES = 128
VMEM_LIMIT = 56 * 1024 * 1024

OFF_Q = 0
OFF_KVC = OFF_Q + N_HEADS * HEAD_DIM
OFF_KVS = OFF_KVC + KV_WIDTH
OFF_KVW = OFF_KVS + KV_WIDTH
ATT_WIDTH = OFF_KVW + KV_WIDTH
REST_WIDTH = 2 * M_INNER + M_INNER + M_INNER
N_GATE = 3 * N_HEADS
R_MQK, R_MV, R_MO = 0, 2 * M_INNER, 3 * M_INNER


def _cparams(sem):
    return pltpu.CompilerParams(dimension_semantics=sem, vmem_limit_bytes=VMEM_LIMIT)


def _split3(x):
    hi = x.astype(BF16)
    r1 = x - hi.astype(F32)
    mid = r1.astype(BF16)
    lo = (r1 - mid.astype(F32)).astype(BF16)
    return hi, mid, lo


def _dot_exact01(x, e):
    hi, mid, lo = _split3(x)
    d = lambda a: jnp.dot(a, e, preferred_element_type=F32)
    return d(hi) + d(mid) + d(lo)


def _kv_proj_kernel(h_ref, w_ref, b_ref, c_ref, s1_ref, s2_ref, *refs, head_major):
    if head_major:
        o_ref, hm_ref = refs
    else:
        o_ref, = refs
    j = pl.program_id(1)
    acc = jnp.dot(h_ref[...], w_ref[...], preferred_element_type=F32) + b_ref[...]

    def emit(rope):
        c, s1, s2 = c_ref[...], s1_ref[...], s2_ref[...]
        for g in range(KV_GROUPS):
            a = acc[:, g * LANES:(g + 1) * LANES]
            if rope:
                a = a * c + pltpu.roll(a, ROPE_HALF, 1) * s1 + pltpu.roll(a, LANES - ROPE_HALF, 1) * s2
            o_ref[:, 0, g, :] = a
            if head_major:
                hm_ref[0, g] = a.astype(BF16)

    pl.when(j == 0)(functools.partial(emit, True))
    pl.when(j != 0)(functools.partial(emit, False))


def _kv_proj(h, w, b, rope, *, col0, tm, seq_len=None):
    m, d = h.shape
    tn = KV_GROUPS * HEAD_DIM
    jb = col0 // tn
    period = rope[0].shape[0] // tm
    assert m % tm == 0 and col0 % tn == 0 and rope[0].shape[0] % tm == 0
    in_specs = [
        pl.BlockSpec((tm, d), lambda i, j: (i, 0)),
        pl.BlockSpec((d, tn), lambda i, j: (0, j + jb)),
        pl.BlockSpec((1, tn), lambda i, j: (0, j + jb)),
    ] + [pl.BlockSpec((tm, LANES), lambda i, j: (i % period, 0))] * 3
    out_shape = [jax.ShapeDtypeStruct((m, 2, KV_GROUPS, HEAD_DIM), F32)]
    out_specs = [pl.BlockSpec((tm, 1, KV_GROUPS, HEAD_DIM), lambda i, j: (i, j, 0, 0))]
    if seq_len is not None:
        per = seq_len // tm
        assert seq_len % tm == 0
        out_shape.append(jax.ShapeDtypeStruct((m // seq_len, KG, seq_len, HEAD_DIM), BF16))
        out_specs.append(pl.BlockSpec((1, KV_GROUPS, tm, HEAD_DIM), lambda i, j: (i // per, j, i % per, 0)))
    return pl.pallas_call(
        functools.partial(_kv_proj_kernel, head_major=seq_len is not None),
        grid=(m // tm, 2),
        in_specs=in_specs,
        out_specs=out_specs,
        out_shape=out_shape,
        compiler_params=_cparams(("parallel", "arbitrary")),
    )(h, w, b, *rope)


def _proj_kernel(h_ref, w_ref, b_ref, *refs, tn, rope, out_scale):
    if rope:
        c_ref, s1_ref, s2_ref, o_ref = refs
    else:
        o_ref, = refs
    acc = jnp.dot(h_ref[...], w_ref[...], preferred_element_type=F32) + b_ref[...]
    if not rope:
        o_ref[...] = (acc * out_scale if out_scale != 1.0 else acc).astype(o_ref.dtype)
        return
    c, s1, s2 = c_ref[...], s1_ref[...], s2_ref[...]
    for hh in range(tn // LANES):
        a = acc[:, hh * LANES:(hh + 1) * LANES]
        r = a * c + pltpu.roll(a, ROPE_HALF, 1) * s1 + pltpu.roll(a, LANES - ROPE_HALF, 1) * s2
        o_ref[:, hh * LANES:(hh + 1) * LANES] = (r * out_scale if out_scale != 1.0 else r).astype(o_ref.dtype)


def _proj(h, w, b, *, col0, ncols, tm, tn, out_dtype, rope=None, out_scale=1.0):
    m, d = h.shape
    assert m % tm == 0 and ncols % tn == 0 and col0 % tn == 0
    jb = col0 // tn
    in_specs = [
        pl.BlockSpec((tm, d), lambda i, j: (i, 0)),
        pl.BlockSpec((d, tn), lambda i, j: (0, j + jb)),
        pl.BlockSpec((1, tn), lambda i, j: (0, j + jb)),
    ]
    args = [h, w, b]
    if rope is not None:
        period = rope[0].shape[0] // tm
        assert rope[0].shape[0] % tm == 0
        for tab in rope:
            in_specs.append(pl.BlockSpec((tm, LANES), lambda i, j: (i % period, 0)))
            args.append(tab)
    return pl.pallas_call(
        functools.partial(_proj_kernel, tn=tn, rope=rope is not None, out_scale=out_scale),
        grid=(m // tm, ncols // tn),
        in_specs=in_specs,
        out_specs=pl.BlockSpec((tm, tn), lambda i, j: (i, j)),
        out_shape=jax.ShapeDtypeStruct((m, ncols), out_dtype),
        compiler_params=_cparams(("parallel", "parallel")),
    )(*args)


def _norm_small_proj_kernel(x_ref, g_ref, w_ref, b_ref, o_ref, h_ref):
    xf = x_ref[...]
    h = xf * lax.rsqrt(jnp.mean(xf * xf, axis=-1, keepdims=True) + EPS) * g_ref[...]
    h_hi, h_mid, _ = _split3(h)
    w_hi, w_mid, _ = _split3(w_ref[...])
    d = lambda a, b: jnp.dot(a, b, preferred_element_type=F32)
    o_ref[...] = d(h_hi, w_hi) + (d(h_hi, w_mid) + d(h_mid, w_hi)) + b_ref[...]
    h_ref[...] = h_hi


def _norm_small_proj(x, g, w, b, *, tm):
    m, d = x.shape
    n = w.shape[1]
    return pl.pallas_call(
        _norm_small_proj_kernel,
        grid=(m // tm,),
        in_specs=[pl.BlockSpec((tm, d), lambda i: (i, 0)), pl.BlockSpec((1, d), lambda i: (0, 0)),
                  pl.BlockSpec((d, n), lambda i: (0, 0)), pl.BlockSpec((1, n), lambda i: (0, 0))],
        out_specs=[pl.BlockSpec((tm, n), lambda i: (i, 0)), pl.BlockSpec((tm, d), lambda i: (i, 0))],
        out_shape=[jax.ShapeDtypeStruct((m, n), F32), jax.ShapeDtypeStruct((m, d), BF16)],
        compiler_params=_cparams(("parallel",)),
    )(x, g, w, b)


def _rope_tables(pos):
    inv = ROPE_THETA ** (-jnp.arange(ROPE_HALF, dtype=F32) / ROPE_HALF)
    ang = pos.astype(F32)[:, None] * inv
    cos, sin = jnp.cos(ang), jnp.sin(ang)
    p = pos.shape[0]
    rest = HEAD_DIM - ROPE_DIM
    c = jnp.concatenate([cos, cos, jnp.ones((p, rest), F32)], axis=1)
    s1 = jnp.concatenate([jnp.zeros((p, ROPE_HALF), F32), sin, jnp.zeros((p, rest), F32)], axis=1)
    s2 = jnp.concatenate([-sin, jnp.zeros((p, ROPE_HALF + rest), F32)], axis=1)
    return c, s1, s2


def _log_sigmoid(x):
    return jnp.minimum(x, 0.0) - jnp.log1p(jnp.exp(-jnp.abs(x)))


def _mlstm_kernel(*refs, L, zero_init):
    if zero_init:
        (q_ref, k_ref, v_ref, o_ref, gc_ref, gr_ref, cwq_ref, cwk_ref, cbq_ref, cbk_ref, cvq_ref, cvk_ref, nw_ref,
         h_ref, c_ref, n_ref, m_ref, xq_scr, xk_scr) = refs
    else:
        (q_ref, k_ref, v_ref, o_ref, gc_ref, gr_ref, cwq_ref, cwk_ref, cbq_ref, cbk_ref, cvq_ref, cvk_ref, nw_ref,
         c0_ref, n0_ref, m0_ref, h_ref, c_ref, n_ref, m_ref, xq_scr, xk_scr) = refs
    ci = pl.program_id(2)
    PAD = 8

    @pl.when(ci == 0)
    def _():
        if zero_init:
            c_ref[...] = jnp.zeros_like(c_ref)
            n_ref[...] = jnp.zeros_like(n_ref)
            m_ref[...] = jnp.zeros_like(m_ref)
        else:
            c_ref[...] = c0_ref[...]
            n_ref[...] = n0_ref[...]
            m_ref[...] = m0_ref[...]
        xq_scr[0:PAD, :] = cvq_ref[0]
        xk_scr[0:PAD, :] = cvk_ref[0]

    xq_scr[PAD:PAD + L, :] = q_ref[...]
    xk_scr[PAD:PAD + L, :] = k_ref[...]

    def conv_silu(x_scr, w_ref, b_ref):
        acc = b_ref[...] + jnp.zeros((L, M_HEAD_DIM), F32)
        for j in range(CONV_W):
            acc = acc + x_scr[pl.ds(PAD - (CONV_W - 1) + j, L), :] * w_ref[j:j + 1, :]
        return acc * jax.nn.sigmoid(acc)

    qs = conv_silu(xq_scr, cwq_ref, cbq_ref)
    ks = conv_silu(xk_scr, cwk_ref, cbk_ref) * (M_HEAD_DIM ** -0.5)
    xq_scr[0:PAD, :] = xq_scr[L:L + PAD, :]
    xk_scr[0:PAD, :] = xk_scr[L:L + PAD, :]

    gc = gc_ref[0, 0]
    gr = gr_ref[0, 0]
    i_col, f_col = gc[:, 0:1], _log_sigmoid(gc[:, 1:2])
    i_row, f_row = gr[0:1, :], _log_sigmoid(gr[1:2, :])
    t_idx = lax.broadcasted_iota(jnp.int32, (L, L), 0)
    s_idx = lax.broadcasted_iota(jnp.int32, (L, L), 1)
    tril = s_idx <= t_idx
    b_col = jnp.sum(jnp.where(tril, f_row, 0.0), axis=1, keepdims=True)
    b_row = jnp.sum(jnp.where(t_idx <= s_idx, f_col, 0.0), axis=0, keepdims=True)
    m_prev = m_ref[0, 0]
    dmat = jnp.where(tril, b_col - b_row + i_row, NEG_BIG)
    m_t = jnp.maximum(m_prev + b_col, jnp.max(dmat, axis=1, keepdims=True))
    w = jnp.exp(dmat - m_t)
    inter = jnp.exp(m_prev + b_col - m_t)

    c_prev = c_ref[0, 0]
    n_prev = n_ref[0, 0]
    qb, kb, vb = qs.astype(BF16), ks.astype(BF16), v_ref[...].astype(BF16)
    s = lax.dot_general(qb, kb, (((1,), (1,)), ((), ())), preferred_element_type=F32) * w
    num = inter * jnp.dot(qb, c_prev.astype(BF16), preferred_element_type=F32) \
        + jnp.dot(s.astype(BF16), vb, preferred_element_type=F32)
    den = inter * jnp.sum(qs * n_prev, axis=1, keepdims=True) + jnp.sum(s, axis=1, keepdims=True)
    h = num / jnp.maximum(jnp.abs(den), jnp.exp(-m_t))

    m_new = m_t[L - 1:L, :]
    b_last = b_col[L - 1:L, :]
    decay = jnp.exp(m_prev + b_last - m_new)
    g_col = jnp.exp(i_col + b_last - b_col - m_new)
    kg = ks * g_col
    c_ref[0, 0] = decay * c_prev + lax.dot_general(kg.astype(BF16), vb, (((0,), (0,)), ((), ())),
                                                   preferred_element_type=F32)
    n_ref[0, 0] = decay * n_prev + jnp.sum(kg, axis=0, keepdims=True)
    m_ref[0, 0] = m_new

    ho = jax.nn.sigmoid(o_ref[...]) * h
    mu = jnp.mean(ho, axis=1, keepdims=True)
    var = jnp.mean(jnp.square(ho - mu), axis=1, keepdims=True)
    h_ref[...] = ((ho - mu) * lax.rsqrt(var + EPS) * nw_ref[...]).astype(h_ref.dtype)


def _mlstm(rest, gates_col, gates_row, conv_w, conv_b, conv0, norm_w, state, *, nb, t, L):
    nc = t // L
    hb = M_HEAD_DIM
    zero_init = state is None
    row = lambda b, h, c: b * nc + c
    in_specs = [
        pl.BlockSpec((L, hb), lambda b, h, c: (row(b, h, c), R_MQK // hb + h)),
        pl.BlockSpec((L, hb), lambda b, h, c: (row(b, h, c), (R_MQK + M_INNER) // hb + h)),
        pl.BlockSpec((L, hb), lambda b, h, c: (row(b, h, c), R_MV // hb + h)),
        pl.BlockSpec((L, hb), lambda b, h, c: (row(b, h, c), R_MO // hb + h)),
        pl.BlockSpec((1, 1, L, 2), lambda b, h, c: (b, h, c, 0)),
        pl.BlockSpec((1, 1, 2, L), lambda b, h, c: (b, h, 0, c)),
        pl.BlockSpec((CONV_W, hb), lambda b, h, c: (0, h)),
        pl.BlockSpec((CONV_W, hb), lambda b, h, c: (0, M_HEADS + h)),
        pl.BlockSpec((1, hb), lambda b, h, c: (0, h)),
        pl.BlockSpec((1, hb), lambda b, h, c: (0, M_HEADS + h)),
        pl.BlockSpec((1, 8, hb), lambda b, h, c: (b, 0, h)),
        pl.BlockSpec((1, 8, hb), lambda b, h, c: (b, 0, M_HEADS + h)),
        pl.BlockSpec((1, hb), lambda b, h, c: (0, h)),
    ]
    args = [rest, rest, rest, rest, gates_col, gates_row, conv_w, conv_w, conv_b, conv_b, conv0, conv0, norm_w]
    st_specs = [
        pl.BlockSpec((1, 1, hb, hb), lambda b, h, c: (b, h, 0, 0)),
        pl.BlockSpec((1, 1, 1, hb), lambda b, h, c: (b, h, 0, 0)),
        pl.BlockSpec((1, 1, 1, 1), lambda b, h, c: (b, h, 0, 0)),
    ]
    if not zero_init:
        in_specs += st_specs
        args += list(state)
    out_shape = [
        jax.ShapeDtypeStruct((nb * t, M_INNER), BF16),
        jax.ShapeDtypeStruct((nb, M_HEADS, hb, hb), F32),
        jax.ShapeDtypeStruct((nb, M_HEADS, 1, hb), F32),
        jax.ShapeDtypeStruct((nb, M_HEADS, 1, 1), F32),
    ]
    out_specs = [pl.BlockSpec((L, hb), lambda b, h, c: (row(b, h, c), h))] + st_specs
    return pl.pallas_call(
        functools.partial(_mlstm_kernel, L=L, zero_init=zero_init),
        grid=(nb, M_HEADS, nc),
        in_specs=in_specs,
        out_specs=out_specs,
        out_shape=out_shape,
        scratch_shapes=[pltpu.VMEM((L + 8, hb), F32), pltpu.VMEM((L + 8, hb), F32)],
        compiler_params=_cparams(("parallel", "parallel", "arbitrary")),
    )(*args)


def _rms(y, g):
    return y * lax.rsqrt(jnp.mean(y * y, axis=-1, keepdims=True) + EPS) * g


def _merge_out_kernel(ga_ref, gm_ref, att_ref, mh_ref, x_ref, w_ref, g_ref, o_ref):
    f = lambda r: r[...].astype(F32)
    mixed = jax.nn.sigmoid(f(ga_ref)) * f(att_ref) + jax.nn.sigmoid(f(gm_ref)) * f(mh_ref)
    y = jnp.dot(mixed.astype(BF16), w_ref[...], preferred_element_type=F32)
    o_ref[...] = x_ref[...] + _rms(y, g_ref[...])


def _merge_out(merge, att, mh, x, w_out, g_post, *, tm):
    m, d = x.shape
    row = pl.BlockSpec((tm, d), lambda i: (i, 0))
    return pl.pallas_call(
        _merge_out_kernel,
        grid=(m // tm,),
        in_specs=[
            row,
            pl.BlockSpec((tm, d), lambda i: (i, 1)),
            row, row, row,
            pl.BlockSpec((d, d), lambda i: (0, 0)),
            pl.BlockSpec((1, d), lambda i: (0, 0)),
        ],
        out_specs=row,
        out_shape=jax.ShapeDtypeStruct((m, d), F32),
        compiler_params=_cparams(("parallel",)),
    )(merge, merge, att, mh, x, w_out, g_post)


def _mlp_kernel(x_ref, gpre_ref, wu_ref, wd_ref, gpost_ref, o_ref, h_scr, acc_scr):
    f = pl.program_id(1)

    @pl.when(f == 0)
    def _():
        h_scr[...] = _rms(x_ref[...], gpre_ref[...]).astype(BF16)
        acc_scr[...] = jnp.zeros_like(acc_scr)

    u = jnp.dot(h_scr[...], wu_ref[...], preferred_element_type=F32)
    u = jnp.square(jnp.maximum(u, 0.0))
    acc_scr[...] += jnp.dot(u.astype(BF16), wd_ref[...], preferred_element_type=F32)

    @pl.when(f == pl.num_programs(1) - 1)
    def _():
        o_ref[...] = x_ref[...] + _rms(acc_scr[...], gpost_ref[...])


def _mlp(x, g_pre, w_up, w_down, g_post, *, tm, tf):
    m, d = x.shape
    ff = w_up.shape[1]
    return pl.pallas_call(
        _mlp_kernel,
        grid=(m // tm, ff // tf),
        in_specs=[
            pl.BlockSpec((tm, d), lambda i, f: (i, 0)),
            pl.BlockSpec((1, d), lambda i, f: (0, 0)),
            pl.BlockSpec((d, tf), lambda i, f: (0, f)),
            pl.BlockSpec((tf, d), lambda i, f: (f, 0)),
            pl.BlockSpec((1, d), lambda i, f: (0, 0)),
        ],
        out_specs=pl.BlockSpec((tm, d), lambda i, f: (i, 0)),
        out_shape=jax.ShapeDtypeStruct((m, d), F32),
        scratch_shapes=[pltpu.VMEM((tm, d), BF16), pltpu.VMEM((tm, d), F32)],
        compiler_params=_cparams(("parallel", "arbitrary")),
    )(x, g_pre, w_up, w_down, g_post)


CMP_TAIL_SLOTS = 16


def _pe_bias_kernel(pet_ref, w1_ref, o_ref):
    rows = []
    for k in range(2):
        acc = jnp.zeros((1, HEAD_DIM), F32)
        for l in range(CMP_BLOCK):
            acc = acc + jnp.sum(pet_ref[k][:, l:l + 1] * w1_ref[k, l], axis=0, keepdims=True)
        rows += [acc] * KV_GROUPS
    o_ref[...] = jnp.concatenate(rows, axis=0)


def _pe_bias(cmp_pe, cmp_w1):
    pet = jnp.transpose(cmp_pe, (0, 2, 1))
    return pl.pallas_call(
        _pe_bias_kernel,
        out_shape=jax.ShapeDtypeStruct((KG, HEAD_DIM), F32),
        compiler_params=pltpu.CompilerParams(vmem_limit_bytes=VMEM_LIMIT),
    )(pet, cmp_w1)


def _cmp_kernel(pt_ref, *refs, P, has_tail):
    page_refs = refs[:P]
    if has_tail:
        tail_ref, wc_ref, w2_ref, peb_ref, o_ref, ot_ref, carry, res_scr = refs[P:]
    else:
        wc_ref, w2_ref, peb_ref, o_ref, carry, res_scr = refs[P:]
    p = pl.program_id(1)

    def store_head_major(res, out_ref, n_out):
        n = res.shape[0] // KG
        res_scr[0:n * KG, :] = res
        for kg in range(KG):
            rows = res_scr[pl.ds(kg, n, stride=KG), :]
            if n_out > n:
                rows = jnp.concatenate([rows, jnp.zeros((n_out - n, HEAD_DIM), F32)], axis=0)
            out_ref[0, kg] = rows.astype(out_ref.dtype)

    @pl.when(p == 0)
    def _():
        carry[...] = jnp.zeros_like(carry)

    def is_k(rows):
        return (lax.broadcasted_iota(jnp.int32, (rows, 1), 0) % KG) < KV_GROUPS

    def half_proj(xs):
        rows = xs[0].shape[0]
        lhs = jnp.concatenate([x.astype(BF16) for x in xs], axis=1)
        acc = jnp.dot(lhs, wc_ref[...], preferred_element_type=F32)
        ik = is_k(rows)
        a = jnp.where(ik, acc[:, 0:128], acc[:, 256:384])
        b = jnp.where(ik, acc[:, 128:256], acc[:, 384:512])
        return a, b

    def finish(a_prev, b):
        rows = b.shape[0]
        peb = jnp.concatenate([peb_ref[...]] * (rows // KG), axis=0)
        hid = jax.nn.gelu(a_prev + b + peb)
        o2 = jnp.dot(hid.astype(BF16), w2_ref[...], preferred_element_type=F32)
        return jnp.where(is_k(rows), o2[:, 0:128], o2[:, 128:256])

    xs = [jnp.concatenate([r[0, :, l].reshape(CMP_STRIDE // 2 * KG, HEAD_DIM) for r in page_refs], axis=0)
          for l in range(CMP_STRIDE)]
    a, b = half_proj(xs)
    a_prev = jnp.concatenate([carry[...], a[:-KG]], axis=0)
    store_head_major(finish(a_prev, b), o_ref, o_ref.shape[2])
    carry[...] = a[-KG:]

    if has_tail:
        @pl.when(p == pl.num_programs(1) - 1)
        def _():
            nct = tail_ref.shape[1]
            xt = [tail_ref[0, :, l].reshape(nct * KG, HEAD_DIM) for l in range(CMP_STRIDE)]
            at, bt = half_proj(xt)
            ap = jnp.concatenate([a[-KG:], at[:-KG]], axis=0)
            store_head_major(finish(ap, bt), ot_ref, ot_ref.shape[2])


def _cmp_summaries(pool, page_table, tail, wc, w2c, peb, *, P):
    ns, npg = page_table.shape
    cpp = PAGE_SIZE // CMP_STRIDE
    assert npg % P == 0
    has_tail = tail is not None
    blk = (1, cpp, CMP_STRIDE, KG, HEAD_DIM)
    in_specs = [pl.BlockSpec(blk, functools.partial(lambda s, p, pt, u: (pt[s * npg + p * P + u], 0, 0, 0, 0), u=u))
                for u in range(P)]
    args = [pool] * P
    if has_tail:
        nct = tail.shape[1]
        in_specs.append(pl.BlockSpec((1, nct, CMP_STRIDE, KG, HEAD_DIM), lambda s, p, pt: (s, 0, 0, 0, 0)))
        args.append(tail)
    in_specs += [
        pl.BlockSpec(wc.shape, lambda s, p, pt: (0, 0)),
        pl.BlockSpec(w2c.shape, lambda s, p, pt: (0, 0)),
        pl.BlockSpec(peb.shape, lambda s, p, pt: (0, 0)),
    ]
    args += [wc, w2c, peb]
    out_shape = [jax.ShapeDtypeStruct((ns, KG, npg * cpp, HEAD_DIM), BF16)]
    out_specs = [pl.BlockSpec((1, KG, P * cpp, HEAD_DIM), lambda s, p, pt: (s, 0, p, 0))]
    if has_tail:
        out_shape.append(jax.ShapeDtypeStruct((ns, KG, CMP_TAIL_SLOTS, HEAD_DIM), BF16))
        out_specs.append(pl.BlockSpec((1, KG, CMP_TAIL_SLOTS, HEAD_DIM), lambda s, p, pt: (s, 0, 0, 0)))
    return pl.pallas_call(
        functools.partial(_cmp_kernel, P=P, has_tail=has_tail),
        grid_spec=pltpu.PrefetchScalarGridSpec(
            num_scalar_prefetch=1, grid=(ns, npg // P), in_specs=in_specs, out_specs=out_specs,
            scratch_shapes=[pltpu.VMEM((KG, HEAD_DIM), F32), pltpu.VMEM((P * cpp * KG, HEAD_DIM), F32)]),
        out_shape=out_shape,
        compiler_params=_cparams(("parallel", "arbitrary")),
    )(page_table.reshape(-1), *args)


def _pack_cmp_weights(cmp_w1, cmp_w2):
    def cols(l):
        return jnp.concatenate([cmp_w1[0, l], cmp_w1[0, CMP_STRIDE + l], cmp_w1[1, l], cmp_w1[1, CMP_STRIDE + l]], axis=1)
    wc = jnp.concatenate([cols(l) for l in range(CMP_STRIDE)], axis=0).astype(BF16)
    w2c = jnp.concatenate([cmp_w2[0], cmp_w2[1]], axis=1).astype(BF16)
    return wc, w2c


def _overlap_matrix(n_slots, n_sb, width):
    i = jnp.arange(n_slots)[:, None] - 1
    j = jnp.arange(width)[None, :]
    ov = (i >= 0) & (j < n_sb) & (i * CMP_STRIDE < (j + 1) * SLC_BLOCK) & (i * CMP_STRIDE + CMP_BLOCK > j * SLC_BLOCK)
    return ov.astype(BF16)


def _masked_softmax(s, mask):
    s = jnp.where(mask, s, -jnp.inf)
    m = jnp.max(s, axis=-1, keepdims=True)
    m = jnp.where(m > -jnp.inf, m, 0.0)
    e = jnp.exp2(s - m)
    return e / jnp.maximum(jnp.sum(e, axis=-1, keepdims=True), 1e-30)


def _dot_nt(a, b):
    return lax.dot_general(a, b, (((1,), (1,)), ((), ())), preferred_element_type=F32)


def _attn_prompt_kernel(q_ref, kc_ref, vc_ref, ks_ref, vs_ref, kw_ref, vw_ref, gate_ref, ov_ref, e_ref, ge_ref,
                        o_ref, *, tq, n_sb, sel_step):
    i = pl.program_id(0)
    hpg = HEADS_PER_GROUP
    rows = hpg * tq
    s_len = ks_ref.shape[2]

    q = jnp.concatenate([q_ref[:, h * HEAD_DIM:(h + 1) * HEAD_DIM] for h in range(hpg)], axis=0)
    t0 = i * tq
    t_row = t0 + lax.broadcasted_iota(jnp.int32, (rows, 1), 0) % tq
    t_tok = t0 + lax.broadcasted_iota(jnp.int32, (tq, 1), 0)

    def compressed_branch():
        n_slots = kc_ref.shape[2]
        s_c = _dot_nt(q, kc_ref[0, 0])
        slot = lax.broadcasted_iota(jnp.int32, (1, n_slots), 1)
        mask_c = (slot >= 1) & ((slot - 1) * CMP_STRIDE + CMP_BLOCK - 1 <= t_row)
        p_c = _masked_softmax(s_c, mask_c)
        o_c = jnp.dot(p_c.astype(BF16), vc_ref[0, 0], preferred_element_type=F32)
        p_grp = p_c[0:tq]
        for h in range(1, hpg):
            p_grp = p_grp + p_c[h * tq:(h + 1) * tq]
        return o_c, p_grp

    def select_blocks(p_grp):
        score = _dot_exact01(p_grp, ov_ref[...])
        jl = lax.broadcasted_iota(jnp.int32, (1, LANES), 1)
        cur = t_tok // SLC_BLOCK
        valid = jl <= cur
        forced = (jl == 0) | (jl == cur) | (jl == cur - 1)
        score = jnp.where(forced, FORCE_SCORE, score)
        score = jnp.where(valid, score, -jnp.inf)
        rank = jnp.zeros((tq, LANES), F32)
        for j2 in range(n_sb):
            col = score[:, j2:j2 + 1]
            later = jnp.where(jl > j2, 1.0, 0.0)
            rank = rank + jnp.where(col > score, 1.0, 0.0) + jnp.where(col == score, later, 0.0)
        return jnp.where((rank < N_SELECT) & valid, 1.0, 0.0).astype(BF16)

    def attend(k, v, keep):
        bias = jnp.where(keep, 0.0, NEG_BIG)
        s = _dot_nt(q, k)
        es = []
        for h in range(hpg):
            sh = s[h * tq:(h + 1) * tq] + bias
            es.append(jnp.exp2(sh - jnp.max(sh, axis=1, keepdims=True)).astype(BF16))
        ones = (lax.broadcasted_iota(jnp.int32, (v.shape[0], LANES), 1) == 0).astype(BF16)
        oa = jnp.dot(jnp.concatenate(es, axis=0), jnp.concatenate([v, ones], axis=1), preferred_element_type=F32)
        return oa[:, :HEAD_DIM] / oa[:, HEAD_DIM:HEAD_DIM + 1]

    def window_branch():
        band = WINDOW + tq
        start = pl.multiple_of(jnp.maximum(t0 - WINDOW, 0), tq)
        kpos = start + lax.broadcasted_iota(jnp.int32, (1, band), 1)
        return attend(kw_ref[0, 0, pl.ds(start, band), :], vw_ref[0, 0, pl.ds(start, band), :],
                      (kpos <= t_tok) & (kpos > t_tok - WINDOW))

    def combine(o_c, o_s, o_w):
        sig = jax.nn.sigmoid(gate_ref[...])
        g_c, g_s, g_w = (_dot_exact01(sig, ge_ref[k, 0]) for k in range(3))
        for h in range(hpg):
            hs = slice(h * HEAD_DIM, (h + 1) * HEAD_DIM)
            rs = slice(h * tq, (h + 1) * tq)
            o_ref[:, hs] = (g_c[:, hs] * o_c[rs] + g_s[:, hs] * o_s[rs] + g_w[:, hs] * o_w[rs]).astype(o_ref.dtype)

    for var in range(s_len // sel_step):
        width = (var + 1) * sel_step

        @pl.when((t0 + tq - 1) // sel_step == var)
        def _(width=width):
            o_c, p_grp = compressed_branch()
            kpos = lax.broadcasted_iota(jnp.int32, (1, width), 1)
            keep = kpos <= t_tok
            if width > N_SELECT * SLC_BLOCK:
                selx = jnp.dot(select_blocks(p_grp), e_ref[:, 0:width], preferred_element_type=F32)
                keep = keep & (selx > 0.5)
            o_s = attend(ks_ref[0, 0, 0:width, :], vs_ref[0, 0, 0:width, :], keep)
            combine(o_c, o_s, window_branch())


def _gate_expand(width_heads):
    c = jnp.arange(LANES)[None, :, None]
    k = jnp.arange(3)[:, None, None]
    hh = (jnp.arange(width_heads * HEAD_DIM) // HEAD_DIM)[None, None, :]
    return (c == k * N_HEADS + hh).astype(BF16)


def _attn_prompt(q, kvc_slots, kvs, kvw, small, *, nb, s_len, tq=128, sel_step=256):
    nq = s_len // tq
    n_slots = kvc_slots.shape[2]
    n_sb = s_len // SLC_BLOCK
    assert n_sb <= LANES and n_slots <= LANES and s_len % sel_step == 0 and sel_step % tq == 0
    assert s_len >= WINDOW + tq and WINDOW % tq == 0
    hd = HEAD_DIM
    gw = HEADS_PER_GROUP * hd
    ov = _overlap_matrix(n_slots, n_sb, LANES)
    e = (jnp.arange(LANES)[:, None] == (jnp.arange(s_len) // SLC_BLOCK)[None, :]).astype(BF16)
    ge = _gate_expand(N_HEADS).reshape(3, LANES, KV_GROUPS, gw).transpose(0, 2, 1, 3)
    kspec = lambda kv: pl.BlockSpec((1, 1, s_len, hd), lambda i, b, g: (b, kv * KV_GROUPS + g, 0, 0))
    cspec = lambda kv: pl.BlockSpec((1, 1, n_slots, hd), lambda i, b, g: (b, kv * KV_GROUPS + g, 0, 0))
    return pl.pallas_call(
        functools.partial(_attn_prompt_kernel, tq=tq, n_sb=n_sb, sel_step=sel_step),
        grid=(nq, nb, KV_GROUPS),
        in_specs=[
            pl.BlockSpec((tq, gw), lambda i, b, g: (b * nq + i, g)),
            cspec(0), cspec(1), kspec(0), kspec(1), kspec(0), kspec(1),
            pl.BlockSpec((tq, LANES), lambda i, b, g: (b * nq + i, 0)),
            pl.BlockSpec(ov.shape, lambda i, b, g: (0, 0)),
            pl.BlockSpec(e.shape, lambda i, b, g: (0, 0)),
            pl.BlockSpec((3, 1, LANES, gw), lambda i, b, g: (0, g, 0, 0)),
        ],
        out_specs=pl.BlockSpec((tq, gw), lambda i, b, g: (b * nq + i, g)),
        out_shape=jax.ShapeDtypeStruct((nb * s_len, N_HEADS * hd), BF16),
        compiler_params=_cparams(("parallel", "parallel", "parallel")),
    )(q, kvc_slots, kvc_slots, kvs, kvs, kvw, kvw, small, ov, e, ge)


T_PAD = 8


def _attn_sample_a_kernel(q_ref, kcm_ref, kct_ref, wc_ref, wn_ref, ov_ref, oc_ref, ow_ref, idx_ref,
                          *, past, t_new, n_blocks, n_sb, wb):
    hpg = HEADS_PER_GROUP
    rows = hpg * T_PAD
    t_pos = past + lax.broadcasted_iota(jnp.int32, (rows, 1), 0) % T_PAD
    tp8 = past + lax.broadcasted_iota(jnp.int32, (T_PAD, 1), 0)
    n_slots = kcm_ref.shape[2] + kct_ref.shape[2]
    slot = lax.broadcasted_iota(jnp.int32, (1, n_slots), 1)
    mask_c = (slot >= 1) & (slot <= n_blocks) & ((slot - 1) * CMP_STRIDE + CMP_BLOCK - 1 <= t_pos)
    width = ov_ref.shape[1]
    jl = lax.broadcasted_iota(jnp.int32, (1, width), 1)
    jf = jl.astype(F32)
    cur = tp8 // SLC_BLOCK
    valid = (jl <= cur) & (jl < n_sb)
    forced = (jl == 0) | (jl == cur) | (jl == cur - 1)
    lane = lax.broadcasted_iota(jnp.int32, (1, LANES), 1)
    wj = lax.broadcasted_iota(jnp.int32, (1, wb + T_PAD), 1)
    kp = past - wb + wj
    mask_w = (kp <= t_pos) & (kp > t_pos - WINDOW) & (kp >= 0) & (wj < wb + t_new)

    for g in range(KV_GROUPS):
        q = q_ref[0, g].astype(BF16)
        kc = jnp.concatenate([kcm_ref[0, g], kct_ref[0, g]], axis=0)
        vc = jnp.concatenate([kcm_ref[0, KV_GROUPS + g], kct_ref[0, KV_GROUPS + g]], axis=0)
        p_c = _masked_softmax(_dot_nt(q, kc), mask_c)
        oc_ref[0, g] = jnp.dot(p_c.astype(BF16), vc, preferred_element_type=F32)
        p_grp = p_c[0:T_PAD]
        for h in range(1, hpg):
            p_grp = p_grp + p_c[h * T_PAD:(h + 1) * T_PAD]
        score = _dot_exact01(p_grp, ov_ref[...])
        score = jnp.where(forced, FORCE_SCORE, score)
        score = jnp.where(valid, score, -jnp.inf)
        idx = jnp.zeros((T_PAD, LANES), jnp.int32)
        for k in range(N_SELECT):
            mx = jnp.max(score, axis=1, keepdims=True)
            am = jnp.min(jnp.where(score == mx, jf, 1e9), axis=1, keepdims=True)
            pick = jnp.where(mx > -jnp.inf, am, -1.0).astype(jnp.int32)
            idx = jnp.where(lane == k, pick, idx)
            score = jnp.where(jf == am, -jnp.inf, score)
        idx_ref[0, g] = idx

        kw = jnp.concatenate([wc_ref[0, :, 0, g, :], wn_ref[0, :, 0, g, :]], axis=0).astype(BF16)
        vw = jnp.concatenate([wc_ref[0, :, 1, g, :], wn_ref[0, :, 1, g, :]], axis=0).astype(BF16)
        p_w = _masked_softmax(_dot_nt(q, kw), mask_w)
        ow_ref[0, g] = jnp.dot(p_w.astype(BF16), vw, preferred_element_type=F32)


def _attn_sample_a(q4, slots_main, slots_tail, win_cache, win_new, *, past, t_new, n_blocks, n_sb):
    ns = q4.shape[0]
    wb = win_cache.shape[1]
    n_slots = slots_main.shape[2] + slots_tail.shape[2]
    width = -(-n_sb // LANES) * LANES
    ov = _overlap_matrix(n_slots, n_sb, width)
    rows = HEADS_PER_GROUP * T_PAD
    full = lambda a: pl.BlockSpec((1,) + a.shape[1:], lambda s: (s,) + (0,) * (a.ndim - 1))
    o_sds = jax.ShapeDtypeStruct((ns, KV_GROUPS, rows, HEAD_DIM), F32)
    o_spec = pl.BlockSpec((1, KV_GROUPS, rows, HEAD_DIM), lambda s: (s, 0, 0, 0))
    return pl.pallas_call(
        functools.partial(_attn_sample_a_kernel, past=past, t_new=t_new, n_blocks=n_blocks, n_sb=n_sb, wb=wb),
        grid=(ns,),
        in_specs=[full(q4), full(slots_main), full(slots_tail), full(win_cache), full(win_new),
                  pl.BlockSpec(ov.shape, lambda s: (0, 0))],
        out_specs=[o_spec, o_spec, pl.BlockSpec((1, KV_GROUPS, T_PAD, LANES), lambda s: (s, 0, 0, 0))],
        out_shape=[o_sds, o_sds, jax.ShapeDtypeStruct((ns, KV_GROUPS, T_PAD, LANES), jnp.int32)],
        compiler_params=_cparams(("parallel",)),
    )(q4, slots_main, slots_tail, win_cache, win_new, ov)


def _attn_sample_b_kernel(idx_ref, pt_ref, q_ref, pool_ref, tail_ref, o_ref, buf, sem, *, past, nb_past, t_new, npg):
    n = N_SELECT
    G = KV_GROUPS
    bpp = PAGE_SIZE // SLC_BLOCK
    nblk = t_new * n
    s, g = pl.program_id(0), pl.program_id(1)
    step = s * G + g
    nsteps = pl.num_programs(0) * G
    slot = step % 2

    def start_copies(step_, slot_):
        s_, g_ = step_ // G, step_ % G

        def body(j, carry):
            jj = idx_ref[step_ * nblk + j]
            jp = jnp.clip(jj, 0, nb_past - 1)
            page = pt_ref[s_ * npg + jp // bpp]
            r0 = pl.multiple_of((jp % bpp) * SLC_BLOCK, SLC_BLOCK)
            for kv in range(2):
                @pl.when(jj < nb_past)
                def _(kv=kv):
                    pltpu.make_async_copy(pool_ref.at[page, pl.ds(r0, SLC_BLOCK), kv, g_], buf.at[slot_, kv, j],
                                          sem.at[slot_]).start()

                @pl.when(jj >= nb_past)
                def _(kv=kv):
                    pltpu.make_async_copy(tail_ref.at[s_, :, kv, g_], buf.at[slot_, kv, j], sem.at[slot_]).start()
            return carry

        lax.fori_loop(0, nblk, body, 0)

    def wait_copies(slot_):
        def body(j, carry):
            for kv in range(2):
                pltpu.make_async_copy(tail_ref.at[0, :, kv, 0], buf.at[slot_, kv, j], sem.at[slot_]).wait()
            return carry

        lax.fori_loop(0, nblk, body, 0)

    @pl.when(step == 0)
    def _():
        start_copies(step, slot)

    @pl.when(step + 1 < nsteps)
    def _():
        start_copies(step + 1, 1 - slot)

    wait_copies(slot)

    q = q_ref[0, 0].astype(BF16)
    rows = q.shape[0]
    t8 = lax.broadcasted_iota(jnp.int32, (rows, 1), 0) % T_PAD
    jl = lax.broadcasted_iota(jnp.int32, (1, n * SLC_BLOCK), 1)
    kslot = jl // SLC_BLOCK
    out = jnp.zeros((rows, HEAD_DIM), F32)
    for t in range(t_new):
        start = jnp.zeros((1, n * SLC_BLOCK), jnp.int32)
        ok = jnp.zeros((1, n * SLC_BLOCK), jnp.int32)
        for k in range(n):
            jj = idx_ref[step * nblk + t * n + k]
            start = jnp.where(kslot == k, jj * SLC_BLOCK, start)
            ok = jnp.where(kslot == k, (jj >= 0).astype(jnp.int32), ok)
        kpos = start + jl % SLC_BLOCK
        mask = (ok > 0) & (kpos <= past + t)
        kk = buf[slot, 0, t * n:(t + 1) * n].reshape(n * SLC_BLOCK, HEAD_DIM).astype(BF16)
        vv = buf[slot, 1, t * n:(t + 1) * n].reshape(n * SLC_BLOCK, HEAD_DIM).astype(BF16)
        p = _masked_softmax(_dot_nt(q, kk), mask)
        o_t = jnp.dot(p.astype(BF16), vv, preferred_element_type=F32)
        out = jnp.where(t8 == t, o_t, out)
    o_ref[0, 0] = out


def _attn_sample_b(idx_flat, pt_flat, q4, tail, pool, *, past, npg, t_new):
    ns = q4.shape[0]
    G = KV_GROUPS
    rows = q4.shape[2]
    assert tail.shape[1] == SLC_BLOCK
    qspec = pl.BlockSpec((1, 1, rows, HEAD_DIM), lambda s, g, idx, pt: (s, g, 0, 0))
    return pl.pallas_call(
        functools.partial(_attn_sample_b_kernel, past=past, nb_past=past // SLC_BLOCK, t_new=t_new, npg=npg),
        grid_spec=pltpu.PrefetchScalarGridSpec(
            num_scalar_prefetch=2, grid=(ns, G),
            in_specs=[qspec, pl.BlockSpec(memory_space=pl.ANY), pl.BlockSpec(memory_space=pl.ANY)],
            out_specs=qspec,
            scratch_shapes=[pltpu.VMEM((2, 2, t_new * N_SELECT, SLC_BLOCK, HEAD_DIM), F32),
                            pltpu.SemaphoreType.DMA((2,))]),
        out_shape=jax.ShapeDtypeStruct((ns, G, rows, HEAD_DIM), F32),
        compiler_params=_cparams(("arbitrary", "arbitrary")),
    )(idx_flat, pt_flat, q4, pool, tail)


def _nsa_combine_kernel(oc_ref, os_ref, ow_ref, gate_ref, ge_ref, o_ref):
    sig = jax.nn.sigmoid(gate_ref[...])
    o_ref[...] = (_dot_exact01(sig, ge_ref[0]) * oc_ref[...] + _dot_exact01(sig, ge_ref[1]) * os_ref[...]
                  + _dot_exact01(sig, ge_ref[2]) * ow_ref[...])


def _nsa_combine(oc, os_, ow, small):
    ge = _gate_expand(N_HEADS)
    return pl.pallas_call(
        _nsa_combine_kernel,
        out_shape=jax.ShapeDtypeStruct(oc.shape, F32),
        compiler_params=pltpu.CompilerParams(vmem_limit_bytes=VMEM_LIMIT),
    )(oc, os_, ow, small, ge)


def _repack_kernel(a_ref, b_ref, o_ref, *, shift):
    if shift == 0:
        o_ref[...] = a_ref[...].astype(o_ref.dtype)
        return
    tn = a_ref.shape[1]
    cat = jnp.concatenate([a_ref[...], b_ref[...]], axis=1)
    o_ref[...] = pltpu.roll(cat, 2 * tn - shift, 1)[:, :tn].astype(o_ref.dtype)


def _repack(w, *, col_start, ncols, tk=512, tn=512):
    k, n = w.shape
    base, shift = divmod(col_start, tn)
    last = pl.cdiv(n, tn) - 1
    assert k % tk == 0 and ncols % tn == 0 and base + ncols // tn - 1 + (shift > 0) <= last
    return pl.pallas_call(
        functools.partial(_repack_kernel, shift=shift),
        grid=(k // tk, ncols // tn),
        in_specs=[pl.BlockSpec((tk, tn), lambda i, j: (i, base + j)),
                  pl.BlockSpec((tk, tn), lambda i, j: (i, jnp.minimum(base + j + 1, last)))],
        out_specs=pl.BlockSpec((tk, tn), lambda i, j: (i, j)),
        out_shape=jax.ShapeDtypeStruct((k, ncols), BF16),
        compiler_params=_cparams(("parallel", "parallel")),
    )(w, w)


def _small_cols_kernel(a_ref, b_ref, o_ref, *, n_a, n_b):
    lane = lax.broadcasted_iota(jnp.int32, (1, LANES), 1)
    o_ref[...] = jnp.where(lane < n_a, a_ref[...], jnp.where(lane < n_a + n_b, b_ref[...], 0.0))


def _small_cols(w, *, col_a, n_a, col_b, n_b):
    k = w.shape[0]
    assert col_a % LANES == 0 and col_b % LANES == n_a and n_a + n_b <= LANES
    return pl.pallas_call(
        functools.partial(_small_cols_kernel, n_a=n_a, n_b=n_b),
        grid=(1,),
        in_specs=[pl.BlockSpec((k, LANES), lambda i: (0, col_a // LANES)),
                  pl.BlockSpec((k, LANES), lambda i: (0, col_b // LANES))],
        out_specs=pl.BlockSpec((k, LANES), lambda i: (0, 0)),
        out_shape=jax.ShapeDtypeStruct((k, LANES), F32),
        compiler_params=_cparams(("arbitrary",)),
    )(w, w)


def _pack_in_proj(w_in, b_in):
    sizes = (N_HEADS * HEAD_DIM, KV_WIDTH, KV_WIDTH, KV_WIDTH, N_GATE, 2 * M_INNER, M_INNER, M_INNER, 2 * M_HEADS,
             2 * D_MODEL)
    offs = [0]
    for sz in sizes:
        offs.append(offs[-1] + sz)
    o_attg, o_mqk, o_mif, o_merge, end = offs[4], offs[5], offs[8], offs[9], offs[10]
    pad = LANES - N_GATE - 2 * M_HEADS

    b = b_in[None, :]
    b_slabs = [b[:, :o_attg], b[:, o_mqk:o_mif], b[:, o_merge:end]]
    b_small = jnp.concatenate([b[:, o_attg:o_mqk], b[:, o_mif:o_merge], jnp.zeros((1, pad), F32)], axis=-1)
    w_slabs = [_repack(w_in, col_start=0, ncols=o_attg), _repack(w_in, col_start=o_mqk, ncols=o_mif - o_mqk),
               _repack(w_in, col_start=o_merge, ncols=end - o_merge)]
    w_small = _small_cols(w_in, col_a=o_attg, n_a=o_mqk - o_attg, col_b=o_mif, n_b=o_merge - o_mif)
    return w_slabs, b_slabs, w_small, b_small


def _project(x, pos, g_pre, packed, *, tm, seq_len=None):
    (w_att, w_rest, w_merge), (b_att, b_rest, b_merge), w_small, b_small = packed
    tabs = _rope_tables(pos)
    small, h = _norm_small_proj(x, g_pre, w_small, b_small, tm=min(tm, 512))
    q = _proj(h, w_att, b_att, col0=OFF_Q, ncols=N_HEADS * HEAD_DIM, out_dtype=BF16, tm=tm, tn=512, rope=tabs,
              out_scale=ATT_SCALE * LOG2E)
    kv = [_kv_proj(h, w_att, b_att, tabs, col0=off, tm=tm, seq_len=None if off == OFF_KVC else seq_len)
          for off in (OFF_KVC, OFF_KVS, OFF_KVW)]
    rest = _proj(h, w_rest, b_rest, col0=0, ncols=REST_WIDTH, tm=tm, tn=1024, out_dtype=F32)
    merge = _proj(h, w_merge, b_merge, col0=0, ncols=2 * D_MODEL, tm=tm, tn=1024, out_dtype=BF16)
    return q, kv, rest, merge, small


def _gate_layouts(small, nb, t, t_pad):
    mif = small[:, N_GATE:N_GATE + 2 * M_HEADS].reshape(nb, t, 2, M_HEADS)
    if t_pad > t:
        fill = jnp.broadcast_to(jnp.array([NEG_BIG, 1e4], F32)[None, None, :, None], (nb, t_pad - t, 2, M_HEADS))
        mif = jnp.concatenate([mif, fill], axis=1)
    return jnp.transpose(mif, (0, 3, 1, 2)), jnp.transpose(mif, (0, 3, 2, 1))


def _pad_rows(a, n):
    return jnp.concatenate([a, jnp.zeros((a.shape[0], n - a.shape[1]) + a.shape[2:], a.dtype)], axis=1)


def _nsa_sample(q_s, kvc_s, kvs_s, kvw_s, small_s, cache_kv_cmp, cache_kv_slc, cache_win_kv, page_table,
                wc, w2c, peb, *, DB, T):
    npg = page_table.shape[1]
    past = npg * PAGE_SIZE
    n_phys = cache_kv_cmp.shape[0]
    cpp = PAGE_SIZE // CMP_STRIDE
    t_blk = -(-T // SLC_BLOCK) * SLC_BLOCK
    assert t_blk == SLC_BLOCK
    new_rows = lambda a, n: _pad_rows(a.reshape(DB, T, 2, KV_GROUPS, HEAD_DIM), n)
    cmp_tail = new_rows(kvc_s, t_blk).reshape(DB, t_blk // CMP_STRIDE, CMP_STRIDE, KG, HEAD_DIM)
    slots_s, slots_t = _cmp_summaries(cache_kv_cmp.reshape(n_phys, cpp, CMP_STRIDE, KG, HEAD_DIM), page_table,
                                      cmp_tail, wc, w2c, peb, P=16)
    n_blocks = (past + t_blk) // CMP_STRIDE - 1
    nb_past = past // SLC_BLOCK
    n_sb = nb_past + t_blk // SLC_BLOCK
    q6 = q_s.astype(F32).reshape(DB, T, KV_GROUPS, HEADS_PER_GROUP, HEAD_DIM)
    q4 = _pad_rows(jnp.transpose(q6, (0, 2, 3, 1, 4)).reshape(DB * KV_GROUPS * HEADS_PER_GROUP, T, HEAD_DIM), T_PAD)
    q4 = q4.reshape(DB, KV_GROUPS, HEADS_PER_GROUP * T_PAD, HEAD_DIM)
    oc, ow, idx = _attn_sample_a(q4, slots_s, slots_t, cache_win_kv, new_rows(kvw_s, T_PAD),
                                 past=past, t_new=T, n_blocks=n_blocks, n_sb=n_sb)
    idx_flat = idx[:, :, :T, :N_SELECT].reshape(-1)
    os4 = _attn_sample_b(idx_flat, page_table.reshape(-1), q4, new_rows(kvs_s, SLC_BLOCK), cache_kv_slc,
                         past=past, npg=npg, t_new=T)
    tok_major = lambda o: jnp.transpose(o.reshape(DB, KV_GROUPS, HEADS_PER_GROUP, T_PAD, HEAD_DIM)[:, :, :, :T],
                                        (0, 3, 1, 2, 4)).reshape(DB * T, N_HEADS * HEAD_DIM)
    return _nsa_combine(tok_major(oc), tok_major(os4), tok_major(ow), small_s)


def kernel(x_prompt, x_sample, cache_kv_cmp, cache_kv_slc, cache_win_kv, state_mlstm_c, state_mlstm_n, state_mlstm_m, state_mlstm_conv, page_table, g_mix_pre, w_in, b_in, cmp_pe, cmp_w1, cmp_w2, conv_w, conv_b, mlstm_norm_w, w_out, g_mix_post, g_mlp_pre, w_up, w_down, g_mlp_post):
    B, S, D = x_prompt.shape
    DB, T, _ = x_sample.shape
    npg = page_table.shape[1]
    past = npg * PAGE_SIZE
    n_phys = cache_kv_cmp.shape[0]
    cpp = PAGE_SIZE // CMP_STRIDE
    kv5 = (2, KV_GROUPS, HEAD_DIM)

    packed = _pack_in_proj(w_in, b_in)
    w_out_b, w_up_b, w_down_b = w_out.astype(BF16), w_up.astype(BF16), w_down.astype(BF16)
    row = lambda v: v[None, :]
    peb = _pe_bias(cmp_pe, cmp_w1)
    wc, w2c = _pack_cmp_weights(cmp_w1, cmp_w2)

    def tail(merge, att, mh, x, tm, tm_mlp):
        x1 = _merge_out(merge, att, mh, x, w_out_b, row(g_mix_post), tm=tm)
        return _mlp(x1, row(g_mlp_pre), w_up_b, w_down_b, row(g_mlp_post), tm=tm_mlp, tf=1024)

    xp = x_prompt.reshape(B * S, D)
    q_p, kv_p, rest_p, merge_p, small_p = _project(xp, jnp.arange(S), row(g_mix_pre), packed, tm=1024, seq_len=S)
    (kvc_p,), (kvs_p, kvs_hm), (kvw_p, kvw_hm) = kv_p
    pt_p = jnp.arange(B * S // PAGE_SIZE, dtype=jnp.int32).reshape(B, S // PAGE_SIZE)
    slots_p, = _cmp_summaries(kvc_p.reshape(B * S // PAGE_SIZE, cpp, CMP_STRIDE, KG, HEAD_DIM), pt_p, None,
                              wc, w2c, peb, P=16)
    att_p = _attn_prompt(q_p, slots_p, kvs_hm, kvw_hm, small_p, nb=B, s_len=S)
    gcol, grow = _gate_layouts(small_p, B, S, S)
    mh_p, c_p, n_p, m_p = _mlstm(rest_p, gcol, grow, conv_w, row(conv_b), jnp.zeros((B, 8, 2 * M_INNER), F32),
                                 row(mlstm_norm_w), None, nb=B, t=S, L=256)
    y_p = tail(merge_p, att_p, mh_p, xp, 256, 512)
    wlen = min(WINDOW, S)
    conv_p = rest_p.reshape(B, S, REST_WIDTH)[:, S - (CONV_W - 1):, R_MQK:R_MQK + 2 * M_INNER]

    xs = x_sample.reshape(DB * T, D)
    pos_s = jnp.tile(past + jnp.arange(T), DB)
    q_s, ((kvc_s,), (kvs_s,), (kvw_s,)), rest_s, merge_s, small_s = _project(xs, pos_s, row(g_mix_pre), packed, tm=DB * T)
    att_s = _nsa_sample(q_s, kvc_s, kvs_s, kvw_s, small_s, cache_kv_cmp, cache_kv_slc, cache_win_kv, page_table,
                        wc, w2c, peb, DB=DB, T=T)

    LS = 16
    gcol_s, grow_s = _gate_layouts(small_s, DB, T, LS)
    rest_pad = _pad_rows(rest_s.reshape(DB, T, REST_WIDTH), LS).reshape(DB * LS, REST_WIDTH)
    conv0 = jnp.concatenate([jnp.zeros((DB, 8 - (CONV_W - 1), 2 * M_INNER), F32), state_mlstm_conv], axis=1)
    state = (state_mlstm_c, state_mlstm_n[:, :, None, :], state_mlstm_m[:, :, None, None])
    mh_pad, c_s, n_s, m_s = _mlstm(rest_pad, gcol_s, grow_s, conv_w, row(conv_b), conv0, row(mlstm_norm_w), state,
                                   nb=DB, t=LS, L=LS)
    mh_s = mh_pad.reshape(DB, LS, M_INNER)[:, :T].reshape(DB * T, M_INNER)
    y_s = tail(merge_s, att_s, mh_s, xs, DB * T, DB * T)
    win_s = jnp.concatenate([cache_win_kv[:, T:], kvw_s.reshape((DB, T) + kv5).astype(cache_win_kv.dtype)], axis=1)
    mqk_s = rest_s.reshape(DB, T, REST_WIDTH)[:, :, R_MQK:R_MQK + 2 * M_INNER]
    conv_s = jnp.concatenate([state_mlstm_conv, mqk_s], axis=1)[:, T:]

    return (y_p.reshape(B, S, D), y_s.reshape(DB, T, D),
            kvc_p.reshape((B, S) + kv5), kvc_s.reshape((DB, T) + kv5),
            kvs_p.reshape((B, S) + kv5), kvs_s.reshape((DB, T) + kv5),
            kvw_p.reshape((B, S) + kv5)[:, S - wlen:], win_s,
            c_p, c_s, n_p[:, :, 0], n_s[:, :, 0], m_p[:, :, 0, 0], m_s[:, :, 0, 0], conv_p, conv_s)
```
